```python
import math
import jax, jax.numpy as jnp
from jax import lax
import numpy as np


D_MODEL = 1024
BATCH = 16
SEQ = 2048
DEPTH = 1

N_MEM = 256
MLA_HEADS = 4
MLA_NOPE = 128
MLA_ROPE = 64
MLA_V = 128
Q_LORA_RANK = 384
KV_LORA_RANK = 256
Q_BLOCK = 128
ROPE_THETA = 10000.0
GDN_HEADS = 4
GDN_DK = 128
GDN_DV = 128
GDN_CONV = 4
GDN_CHUNK = 64
MEM_HEADS = 4
MEM_DH = 128
D_MIX = MLA_HEADS * MLA_V + GDN_HEADS * GDN_DV + MEM_HEADS * MEM_DH
GDN_QKV = 2 * GDN_HEADS * GDN_DK + GDN_HEADS * GDN_DV
IN_SPLITS = (Q_LORA_RANK, KV_LORA_RANK, MLA_ROPE, GDN_QKV, GDN_HEADS, GDN_HEADS, MEM_HEADS * MEM_DH, D_MIX)
D_IN = sum(IN_SPLITS)
EPS = 1e-6

kernel_name = 'hybrid_mla_gdn_memory_parallel_heads'


def rms_norm(t, gain):
    tf = t.astype(jnp.float32)
    y = tf * lax.rsqrt(jnp.mean(tf * tf, axis=-1, keepdims=True) + EPS)
    return (y * gain.astype(jnp.float32)).astype(t.dtype)


def l2_normalize(t):
    return t * lax.rsqrt(jnp.sum(t * t, axis=-1, keepdims=True) + EPS)


def split_cols(t, sizes):
    idx = [int(i) for i in np.cumsum(sizes)[:-1]]
    return jnp.split(t, idx, axis=-1)


def rope_tables(positions):
    half = MLA_ROPE // 2
    inv_freq = 1.0 / (ROPE_THETA ** (jnp.arange(half, dtype=jnp.float32) / half))
    ang = positions.astype(jnp.float32)[..., None] * inv_freq
    return jnp.cos(ang), jnp.sin(ang)


def apply_rope(t, cos, sin):
    half = t.shape[-1] // 2
    tf = t.astype(jnp.float32)
    t1, t2 = tf[..., :half], tf[..., half:]
    return jnp.concatenate([t1 * cos - t2 * sin, t2 * cos + t1 * sin], axis=-1).astype(t.dtype)


def causal_mla(q_nope, q_rope, k_nope, k_rope, v):
    S = q_nope.shape[1]
    scale = (MLA_NOPE + MLA_ROPE) ** -0.5
    outs = []
    for i in range(S // Q_BLOCK):
        lo, hi = i * Q_BLOCK, (i + 1) * Q_BLOCK
        s = (jnp.einsum('bqhd,bkhd->bhqk', q_nope[:, lo:hi], k_nope[:, :hi])
             + jnp.einsum('bqhr,bkr->bhqk', q_rope[:, lo:hi], k_rope[:, :hi])).astype(jnp.float32) * scale
        mask = jnp.arange(lo, hi)[:, None] >= jnp.arange(hi)[None, :]
        p = jax.nn.softmax(jnp.where(mask, s, -jnp.inf), axis=-1)
        outs.append(jnp.einsum('bhqk,bkhd->bqhd', p.astype(v.dtype), v[:, :hi]))
    return jnp.concatenate(outs, axis=1)


def causal_depthwise_conv(t, w):
    K, C = w.shape
    return lax.conv_general_dilated(t, w[:, None, :].astype(t.dtype), window_strides=(1,),
                                    padding=[(K - 1, 0)], dimension_numbers=('NWC', 'WIO', 'NWC'),
                                    feature_group_count=C)


def gated_delta_rule_chunked(q, k, v, g, beta):
    B, S, H, Dk = q.shape
    Dv = v.shape[-1]
    C = GDN_CHUNK
    N = S // C

    def chunks(t):
        return t.reshape(B, N, C, H, -1).transpose(0, 3, 1, 2, 4)

    q = chunks(q) * (Dk ** -0.5)
    k = chunks(k)
    v = chunks(v)
    g = jnp.cumsum(g.reshape(B, N, C, H).transpose(0, 3, 1, 2), axis=-1)
    beta = beta.reshape(B, N, C, H).transpose(0, 3, 1, 2)[..., None]
    incl = jnp.tril(jnp.ones((C, C), dtype=bool))
    strict = jnp.tril(jnp.ones((C, C), dtype=bool), -1)
    decay = jnp.exp(jnp.where(incl, g[..., :, None] - g[..., None, :], -jnp.inf))
    k_beta = k * beta
    L = jnp.where(strict, jnp.einsum('bhncd,bhnjd->bhncj', k_beta, k) * decay, 0.0)
    eye = jnp.eye(C, dtype=jnp.float32)
    T = lax.linalg.triangular_solve(L + eye, jnp.broadcast_to(eye, L.shape), left_side=True,
                                    lower=True, unit_diagonal=True)
    u = jnp.einsum('bhncj,bhnjv->bhncv', T, v * beta)
    w = jnp.einsum('bhncj,bhnjk->bhnck', T, k_beta * jnp.exp(g)[..., None])
    a_intra = jnp.einsum('bhncd,bhnjd->bhncj', q, k) * decay

    def step(state, xs):
        q_c, k_c, u_c, w_c, g_c, a_c = xs
        v_new = u_c - jnp.einsum('bhck,bhkv->bhcv', w_c, state)
        o = (jnp.einsum('bhck,bhkv->bhcv', q_c * jnp.exp(g_c)[..., None], state)
             + jnp.einsum('bhcj,bhjv->bhcv', a_c, v_new))
        g_last = g_c[..., -1:]
        state = (state * jnp.exp(g_last)[..., None]
                 + jnp.einsum('bhck,bhcv->bhkv', k_c * jnp.exp(g_last - g_c)[..., None], v_new))
        return state, o

    xs = tuple(jnp.moveaxis(t, 2, 0) for t in (q, k, u, w, g, a_intra))
    state0 = jnp.zeros((B, H, Dk, Dv), jnp.float32)
    _, o = lax.scan(step, state0, xs)
    return o.transpose(1, 0, 3, 2, 4).reshape(B, S, H, Dv)


def memory_attention(q, k, v):
    s = jnp.einsum('bqhd,bmhd->bhqm', q, k).astype(jnp.float32) * (MEM_DH ** -0.5)
    p = jax.nn.softmax(s, axis=-1)
    return jnp.einsum('bhqm,bmhd->bqhd', p.astype(v.dtype), v)


def setup_inputs(seed: int = 0) -> dict:
    key = jax.random.key(seed)
    ks = jax.random.split(key, 20)
    f32 = jnp.float32

    def normal(k, shape, scale):
        return jax.random.normal(k, shape, f32) * scale

    def gain(k, shape):
        return 1.0 + 0.02 * jax.random.normal(k, shape, f32)

    x = normal(ks[0], (BATCH, SEQ, D_MODEL), 1.0)
    mem = normal(ks[1], (BATCH, N_MEM, D_MODEL), 1.0)
    positions = (jax.random.randint(ks[2], (BATCH, 1), 0, 4096) + jnp.arange(SEQ)[None, :]).astype(jnp.int32)
    norm_in = gain(ks[3], (DEPTH, D_MODEL))
    w_in = normal(ks[4], (DEPTH, D_MODEL, D_IN), D_MODEL ** -0.5)
    q_a_norm = gain(ks[5], (DEPTH, Q_LORA_RANK))
    w_q_b = normal(ks[6], (DEPTH, Q_LORA_RANK, MLA_HEADS * (MLA_NOPE + MLA_ROPE)), Q_LORA_RANK ** -0.5)
    kv_a_norm = gain(ks[7], (DEPTH, KV_LORA_RANK))
    w_kv_b = normal(ks[8], (DEPTH, KV_LORA_RANK, MLA_HEADS * (MLA_NOPE + MLA_V)), KV_LORA_RANK ** -0.5)
    gdn_conv = normal(ks[9], (DEPTH, GDN_CONV, GDN_QKV), GDN_CONV ** -0.5)
    gdn_a_log = jnp.log(jax.random.uniform(ks[10], (DEPTH, GDN_HEADS), f32, minval=1.0, maxval=16.0))
    dt = jnp.exp(jax.random.uniform(ks[11], (DEPTH, GDN_HEADS), f32, minval=math.log(1e-3), maxval=math.log(1e-1)))
    gdn_dt_bias = dt + jnp.log(-jnp.expm1(-dt))
    gdn_norm = gain(ks[12], (DEPTH, GDN_DV))
    mem_norm = gain(ks[13], (DEPTH, D_MODEL))
    w_mem_kv = normal(ks[14], (DEPTH, D_MODEL, 2 * MEM_HEADS * MEM_DH), D_MODEL ** -0.5)
    w_out = normal(ks[15], (DEPTH, D_MIX, D_MODEL), D_MIX ** -0.5)
    norm_final = gain(ks[16], (D_MODEL,))
    return {'x': x, 'mem': mem, 'positions': positions, 'norm_in': norm_in, 'w_in': w_in,
            'q_a_norm': q_a_norm, 'w_q_b': w_q_b, 'kv_a_norm': kv_a_norm, 'w_kv_b': w_kv_b,
            'gdn_conv': gdn_conv, 'gdn_a_log': gdn_a_log, 'gdn_dt_bias': gdn_dt_bias,
            'gdn_norm': gdn_norm, 'mem_norm': mem_norm, 'w_mem_kv': w_mem_kv, 'w_out': w_out,
            'norm_final': norm_final}


def reference(x, mem, positions, norm_in, w_in, q_a_norm, w_q_b, kv_a_norm, w_kv_b, gdn_conv,
              gdn_a_log, gdn_dt_bias, gdn_norm, mem_norm, w_mem_kv, w_out, norm_final):
    B, S, _ = x.shape
    M = mem.shape[1]
    cos, sin = rope_tables(positions)
    for l in range(DEPTH):
        h = rms_norm(x, norm_in[l])
        c_q, c_kv, k_rope, gdn_qkv, gdn_a, gdn_b, mem_q, gate = split_cols(h @ w_in[l], IN_SPLITS)

        q = (rms_norm(c_q, q_a_norm[l]) @ w_q_b[l]).reshape(B, S, MLA_HEADS, MLA_NOPE + MLA_ROPE)
        q_nope = q[..., :MLA_NOPE]
        q_rope = apply_rope(q[..., MLA_NOPE:], cos[:, :, None], sin[:, :, None])
        kv = (rms_norm(c_kv, kv_a_norm[l]) @ w_kv_b[l]).reshape(B, S, MLA_HEADS, MLA_NOPE + MLA_V)
        k_nope, v_mla = kv[..., :MLA_NOPE], kv[..., MLA_NOPE:]
        k_rope = apply_rope(k_rope, cos, sin)
        o_mla = causal_mla(q_nope, q_rope, k_nope, k_rope, v_mla).reshape(B, S, MLA_HEADS * MLA_V)

        qkv = jax.nn.silu(causal_depthwise_conv(gdn_qkv, gdn_conv[l])).astype(jnp.float32)
        gq, gk, gv = split_cols(qkv, (GDN_HEADS * GDN_DK, GDN_HEADS * GDN_DK, GDN_HEADS * GDN_DV))
        gq = l2_normalize(gq.reshape(B, S, GDN_HEADS, GDN_DK))
        gk = l2_normalize(gk.reshape(B, S, GDN_HEADS, GDN_DK))
        gv = gv.reshape(B, S, GDN_HEADS, GDN_DV)
        beta = jax.nn.sigmoid(gdn_b.astype(jnp.float32))
        g = -jnp.exp(gdn_a_log[l].astype(jnp.float32)) * jax.nn.softplus(
            gdn_a.astype(jnp.float32) + gdn_dt_bias[l].astype(jnp.float32))
        o_gdn = gated_delta_rule_chunked(gq, gk, gv, g, beta)
        o_gdn = rms_norm(o_gdn, gdn_norm[l]).astype(x.dtype).reshape(B, S, GDN_HEADS * GDN_DV)

        mk, mv = split_cols(rms_norm(mem, mem_norm[l]) @ w_mem_kv[l], (MEM_HEADS * MEM_DH, MEM_HEADS * MEM_DH))
        o_mem = memory_attention(mem_q.reshape(B, S, MEM_HEADS, MEM_DH),
                                 mk.reshape(B, M, MEM_HEADS, MEM_DH),
                                 mv.reshape(B, M, MEM_HEADS, MEM_DH)).reshape(B, S, MEM_HEADS * MEM_DH)

        mixed = jnp.concatenate([o_mla, o_gdn, o_mem], axis=-1) * jax.nn.silu(gate)
        x = x + mixed @ w_out[l]
    return rms_norm(x, norm_final)
```

```python
import functools

import jax
import jax.numpy as jnp
import numpy as np
from jax import lax
from jax.experimental import pallas as pl
from jax.experimental.pallas import tpu as pltpu

F32 = jnp.float32
BF16 = jnp.bfloat16

D_MODEL = 1024
HEADS = 4
HEAD_DIM = 128
MLA_ROPE = 64
Q_LORA = 384
KV_LORA = 256
ROPE_THETA = 10000.0
GDN_CONV = 4
GDN_CHUNK = 64
GDN_QKV = 3 * HEADS * HEAD_DIM
D_MIX = 3 * HEADS * HEAD_DIM
IN_SPLITS = (Q_LORA, KV_LORA, MLA_ROPE, GDN_QKV, HEADS, HEADS, HEADS * HEAD_DIM, D_MIX)
EPS = 1e-6
MLA_SCALE = (HEAD_DIM + MLA_ROPE) ** -0.5
MEM_SCALE = HEAD_DIM ** -0.5
GDN_QSCALE = HEAD_DIM ** -0.5

LANES = 128
QK_WIDTH = 2 * LANES

C_CQ = 0
C_CKV = C_CQ + Q_LORA
C_KR = C_CKV + KV_LORA
C_GDN = C_KR + LANES
C_AB = C_GDN + GDN_QKV
C_MQ = C_AB + LANES
C_GATE = C_MQ + HEADS * HEAD_DIM
C_END = C_GATE + D_MIX

VMEM_LIMIT = 56 * 1024 * 1024


def _dot(a, b, precision=None):
    return jnp.dot(a, b, preferred_element_type=F32, precision=precision)


def _dot_nt(a, b):
    return lax.dot_general(a, b, (((1,), (1,)), ((), ())), preferred_element_type=F32)


def _dot_tn(a, b):
    return lax.dot_general(a, b, (((0,), (0,)), ((), ())), preferred_element_type=F32)


def _rms(t, gain):
    return t * lax.rsqrt(jnp.mean(t * t, axis=-1, keepdims=True) + EPS) * gain


def _sigmoid(t):
    return 1.0 / (1.0 + jnp.exp(-t))


def _silu(t):
    return t * _sigmoid(t)


def _memkv_kernel(mem_ref, gain_ref, w_ref, mk_out, mv_out):
    hm = _rms(mem_ref[0], gain_ref[...]).astype(BF16)
    kv = _dot(hm, w_ref[...])
    half = HEADS * HEAD_DIM
    mk_out[0] = kv[:, :half].astype(BF16)
    mv_out[0] = kv[:, half:].astype(BF16)


def _proj_kernel(x_ref, pos_ref, nin_ref, w_ref, qan_ref, wqb_ref, kvan_ref, wkk_ref, wkv_ref,
                 conv_ref, alog_ref, dtb_ref, invf_ref, sgn_ref, mk_ref, mv_ref,
                 q_out, k_out, v_out, gq_out, gk_out, gv_out, gg_out, om_out, gate_out,
                 cbuf, *, ts):
    j = pl.program_id(1)
    hb = _rms(x_ref[0], nin_ref[...]).astype(BF16)

    ang = pos_ref[0].astype(F32) * invf_ref[...]
    sgn = sgn_ref[...]
    cosm = jnp.cos(ang) * jnp.abs(sgn)
    sinm = jnp.sin(ang) * sgn

    def rope(r):
        return r * cosm + pltpu.roll(r, MLA_ROPE, 1) * sinm

    cq = _dot(hb, w_ref[:, C_CQ:C_CKV])
    qf = _dot(_rms(cq, qan_ref[...]).astype(BF16), wqb_ref[...])
    for h in range(HEADS):
        lo = QK_WIDTH * h
        q_out[0, :, lo:lo + LANES] = (qf[:, lo:lo + LANES] * MLA_SCALE).astype(BF16)
        q_out[0, :, lo + LANES:lo + QK_WIDTH] = (rope(qf[:, lo + LANES:lo + QK_WIDTH]) * MLA_SCALE).astype(BF16)

    ckv = _dot(hb, w_ref[:, C_CKV:C_KR])
    ckvn = _rms(ckv, kvan_ref[...]).astype(BF16)
    kn = _dot(ckvn, wkk_ref[...])
    v_out[0] = _dot(ckvn, wkv_ref[...]).astype(BF16)
    kr = rope(_dot(hb, w_ref[:, C_KR:C_GDN])).astype(BF16)
    for h in range(HEADS):
        lo = QK_WIDTH * h
        k_out[0, :, lo:lo + LANES] = kn[:, LANES * h:LANES * (h + 1)].astype(BF16)
        k_out[0, :, lo + LANES:lo + QK_WIDTH] = kr

    @pl.when(j == 0)
    def _():
        cbuf[0:8, :] = jnp.zeros((8, GDN_QKV), F32)

    @pl.when(j > 0)
    def _():
        cbuf[0:8, :] = cbuf[ts:ts + 8, :]

    cbuf[8:ts + 8, :] = _dot(hb, w_ref[:, C_GDN:C_AB])
    outs = (gq_out, gk_out, gv_out)
    for g in range(GDN_QKV // LANES):
        cols = slice(LANES * g, LANES * (g + 1))
        base = 8 - (GDN_CONV - 1)
        acc = conv_ref[0:1, cols] * cbuf[base:base + ts, cols]
        for t in range(1, GDN_CONV):
            acc = acc + conv_ref[t:t + 1, cols] * cbuf[base + t:base + t + ts, cols]
        y = _silu(acc)
        if g < 2 * HEADS:
            y = y * lax.rsqrt(jnp.sum(y * y, axis=-1, keepdims=True) + EPS)
        if g < HEADS:
            y = y * GDN_QSCALE
        outs[g // HEADS][0, :, LANES * (g % HEADS):LANES * (g % HEADS + 1)] = y.astype(BF16)

    ab = _dot(hb, w_ref[:, C_AB:C_MQ])
    z = ab + dtb_ref[...]
    softplus = jnp.maximum(z, 0.0) + jnp.log1p(jnp.exp(-jnp.abs(z)))
    gval = -jnp.exp(alog_ref[...]) * softplus
    lane = lax.broadcasted_iota(jnp.int32, ab.shape, 1)
    gg_out[0] = jnp.where(lane < HEADS, gval, _sigmoid(ab))

    gs = _silu(_dot(hb, w_ref[:, C_GATE:C_END]))
    gate_out[0] = gs[:, :2 * HEADS * HEAD_DIM].astype(BF16)
    mq = _dot(hb, w_ref[:, C_MQ:C_GATE])
    for h in range(HEADS):
        cols = slice(HEAD_DIM * h, HEAD_DIM * (h + 1))
        s = _dot_nt(mq[:, cols].astype(BF16), mk_ref[0, :, cols]) * MEM_SCALE
        m = jnp.max(s, axis=-1, keepdims=True)
        p = jnp.exp(s - m)
        l = jnp.sum(p, axis=-1, keepdims=True)
        o = _dot(p.astype(BF16), mv_ref[0, :, cols]) / l
        gcol = slice(2 * HEADS * HEAD_DIM + HEAD_DIM * h, 2 * HEADS * HEAD_DIM + HEAD_DIM * (h + 1))
        om_out[0, :, cols] = (o * gs[:, gcol]).astype(BF16)


def _mla_kernel(q_ref, k_ref, v_ref, o_ref, *, tq):
    i = pl.program_id(2)
    q = q_ref[0]

    def step(jk, carry, masked):
        m, l, acc = carry
        start = pl.multiple_of(jk * tq, tq)
        ks = k_ref[0, pl.ds(start, tq), :]
        vs = v_ref[0, pl.ds(start, tq), :]
        s = _dot_nt(q, ks)
        if masked:
            row = lax.broadcasted_iota(jnp.int32, s.shape, 0)
            col = lax.broadcasted_iota(jnp.int32, s.shape, 1)
            s = jnp.where(row >= col, s, -jnp.inf)
        m_new = jnp.maximum(m, jnp.max(s, axis=-1, keepdims=True))
        alpha = jnp.exp(m - m_new)
        p = jnp.exp(s - m_new)
        l = alpha * l + jnp.sum(p, axis=-1, keepdims=True)
        acc = alpha * acc + _dot(p.astype(BF16), vs)
        return m_new, l, acc

    init = (jnp.full((tq, 1), -jnp.inf, F32), jnp.zeros((tq, 1), F32), jnp.zeros((tq, HEAD_DIM), F32))
    carry = lax.fori_loop(0, i, lambda jk, c: step(jk, c, False), init)
    _, l, acc = step(i, carry, True)
    o_ref[0] = (acc / l).astype(BF16)


def _unit_lower_inverse(low):
    n = low.shape[0]
    row = lax.broadcasted_iota(jnp.int32, (n, n), 0)
    col = lax.broadcasted_iota(jnp.int32, (n, n), 1)
    m = -low
    p = jnp.where(row == col, 1.0, 0.0) + m
    levels = int(np.log2(n)) - 1
    for _ in range(levels):
        m = _dot(m, m, precision=lax.Precision.HIGHEST)
        p = p + _dot(p, m, precision=lax.Precision.HIGHEST)
    return p


def _gdn_kernel(q_ref, k_ref, v_ref, gg_ref, gn_ref, o_ref, s_ref, *, ts):
    j = pl.program_id(1)

    @pl.when(j == 0)
    def _():
        s_ref[...] = jnp.zeros(s_ref.shape, F32)

    c = GDN_CHUNK
    row = lax.broadcasted_iota(jnp.int32, (c, c), 0)
    col = lax.broadcasted_iota(jnp.int32, (c, c), 1)
    incl = row >= col
    strict = row > col
    tri = jnp.where(incl, 1.0, 0.0)
    for ci in range(ts // c):
        rows = slice(c * ci, c * (ci + 1))
        gg = gg_ref[0, rows, :]
        gcum = _dot(tri, gg, precision=lax.Precision.HIGHEST)
        gcum_t = gcum.T
        for h in range(HEADS):
            cols = slice(HEAD_DIM * h, HEAD_DIM * (h + 1))
            gcol = gcum[:, h:h + 1]
            grow = gcum_t[h:h + 1, :]
            glast = gcum[c - 1:c, h:h + 1]
            beta = gg[:, HEADS + h:HEADS + h + 1]
            decay = jnp.exp(jnp.where(incl, gcol - grow, -jnp.inf))
            qh = q_ref[0, rows, cols]
            kh = k_ref[0, rows, cols]
            kf = kh.astype(F32)
            kbeta = kf * beta
            kbeta_b = kbeta.astype(BF16)
            vbeta_b = (v_ref[0, rows, cols].astype(F32) * beta).astype(BF16)
            eg = jnp.exp(gcol)
            kq = _dot_nt(jnp.concatenate([kbeta_b, qh], axis=0), kh)
            low = jnp.where(strict, kq[:c] * decay, 0.0)
            a_intra = (kq[c:] * decay).astype(BF16)
            t_inv = _unit_lower_inverse(low).astype(BF16)
            uw = _dot(t_inv, jnp.concatenate([vbeta_b, (kbeta * eg).astype(BF16)], axis=1))
            u = uw[:, :HEAD_DIM]
            w = uw[:, HEAD_DIM:].astype(BF16)
            qg = (qh.astype(F32) * eg).astype(BF16)
            kdec = (kf * jnp.exp(glast - gcol)).astype(BF16)

            state = s_ref[h]
            ws_qs = _dot(jnp.concatenate([w, qg], axis=0), state.astype(BF16))
            v_new = (u - ws_qs[:c]).astype(BF16)
            o = ws_qs[c:] + _dot(a_intra, v_new)
            s_ref[h] = state * jnp.exp(glast) + _dot_tn(kdec, v_new)
            o_ref[0, rows, cols] = _rms(o, gn_ref[...]).astype(BF16)


def _merge_kernel(x_ref, omla_ref, ogdn_ref, om_ref, gate_ref, wout_ref, nf_ref, out_ref):
    half = HEADS * HEAD_DIM
    gate = gate_ref[0]
    acc = _dot(omla_ref[0] * gate[:, :half], wout_ref[0:half, :])
    acc = acc + _dot(ogdn_ref[0] * gate[:, half:], wout_ref[half:2 * half, :])
    acc = acc + _dot(om_ref[0], wout_ref[2 * half:, :])
    out_ref[0] = _rms(x_ref[0] + acc, nf_ref[...])


def _const_spec(shape):
    nd = len(shape)
    return pl.BlockSpec(shape, lambda *_: (0,) * nd)


def _pack_weights(w_in, w_q_b, w_kv_b):
    o = np.cumsum((0,) + IN_SPLITS)
    w_cq, w_ckv, w_kr, w_gdn, w_a, w_b, w_mq, w_gate = (w_in[:, o[i]:o[i + 1]] for i in range(8))
    half = MLA_ROPE // 2
    w_kr2 = jnp.concatenate([w_kr, w_kr[:, half:], w_kr[:, :half]], axis=1)
    w_ab = jnp.concatenate([w_a, w_b, jnp.zeros((D_MODEL, LANES - 2 * HEADS), w_in.dtype)], axis=1)
    w_all = jnp.concatenate([w_cq, w_ckv, w_kr2, w_gdn, w_ab, w_mq, w_gate], axis=1).astype(BF16)
    wq = w_q_b.reshape(Q_LORA, HEADS, HEAD_DIM + MLA_ROPE)
    nope, ropec = wq[..., :HEAD_DIM], wq[..., HEAD_DIM:]
    wqb = jnp.concatenate([nope, ropec, ropec[..., half:], ropec[..., :half]], axis=-1)
    wqb = wqb.reshape(Q_LORA, HEADS * QK_WIDTH).astype(BF16)
    wkv = w_kv_b.reshape(KV_LORA, HEADS, 2 * HEAD_DIM)
    wkk = wkv[..., :HEAD_DIM].reshape(KV_LORA, HEADS * HEAD_DIM).astype(BF16)
    wkvv = wkv[..., HEAD_DIM:].reshape(KV_LORA, HEADS * HEAD_DIM).astype(BF16)
    return w_all, wqb, wkk, wkvv


def _lane_row(vec):
    return jnp.zeros((1, LANES), F32).at[0, :vec.shape[0]].set(vec.astype(F32))


def _layer(x, mk, mv, pos3, invf, sgn, norm_in, w_in, q_a_norm, w_q_b, kv_a_norm, w_kv_b, gdn_conv,
           gdn_a_log, gdn_dt_bias, gdn_norm, w_out, out_gain, *, ts_proj, ts_gdn, tq, ts_merge):
    B, S, D = x.shape
    hd = HEADS * HEAD_DIM
    w_all, wqb, wkk, wkvv = _pack_weights(w_in, w_q_b, w_kv_b)
    M = mk.shape[1]
    arb2 = pltpu.CompilerParams(dimension_semantics=("arbitrary", "arbitrary"), vmem_limit_bytes=VMEM_LIMIT)

    def row_spec(width, ts):
        return pl.BlockSpec((1, ts, width), lambda b, j: (b, j, 0))

    def bs_shape(width, dtype=BF16):
        return jax.ShapeDtypeStruct((B, S, width), dtype)

    q, k, v, gq, gk, gv, gg, om, gate = pl.pallas_call(
        functools.partial(_proj_kernel, ts=ts_proj),
        grid=(B, S // ts_proj),
        in_specs=[row_spec(D, ts_proj), row_spec(1, ts_proj), _const_spec((1, D)), _const_spec(w_all.shape),
                  _const_spec((1, Q_LORA)), _const_spec(wqb.shape), _const_spec((1, KV_LORA)),
                  _const_spec(wkk.shape), _const_spec(wkvv.shape), _const_spec((GDN_CONV, GDN_QKV)),
                  _const_spec((1, LANES)), _const_spec((1, LANES)), _const_spec((1, LANES)), _const_spec((1, LANES)),
                  pl.BlockSpec((1, M, hd), lambda b, j: (b, 0, 0)), pl.BlockSpec((1, M, hd), lambda b, j: (b, 0, 0))],
        out_specs=[row_spec(HEADS * QK_WIDTH, ts_proj), row_spec(HEADS * QK_WIDTH, ts_proj), row_spec(hd, ts_proj),
                   row_spec(hd, ts_proj), row_spec(hd, ts_proj), row_spec(hd, ts_proj), row_spec(LANES, ts_proj),
                   row_spec(hd, ts_proj), row_spec(2 * hd, ts_proj)],
        out_shape=[bs_shape(HEADS * QK_WIDTH), bs_shape(HEADS * QK_WIDTH), bs_shape(hd), bs_shape(hd), bs_shape(hd),
                   bs_shape(hd), bs_shape(LANES, F32), bs_shape(hd), bs_shape(2 * hd)],
        scratch_shapes=[pltpu.VMEM((ts_proj + 8, GDN_QKV), F32)],
        compiler_params=arb2,
        name="proj",
    )(x, pos3, norm_in.reshape(1, D), w_all, q_a_norm.reshape(1, Q_LORA), wqb, kv_a_norm.reshape(1, KV_LORA),
      wkk, wkvv, gdn_conv, _lane_row(gdn_a_log), _lane_row(gdn_dt_bias), invf, sgn, mk, mv)

    o_mla = pl.pallas_call(
        functools.partial(_mla_kernel, tq=tq),
        grid=(B, HEADS, S // tq),
        in_specs=[pl.BlockSpec((1, tq, QK_WIDTH), lambda b, h, i: (b, i, h)),
                  pl.BlockSpec((1, S, QK_WIDTH), lambda b, h, i: (b, 0, h)),
                  pl.BlockSpec((1, S, HEAD_DIM), lambda b, h, i: (b, 0, h))],
        out_specs=pl.BlockSpec((1, tq, HEAD_DIM), lambda b, h, i: (b, i, h)),
        out_shape=bs_shape(hd),
        compiler_params=pltpu.CompilerParams(dimension_semantics=("arbitrary",) * 3, vmem_limit_bytes=VMEM_LIMIT),
        name="mla",
    )(q, k, v)

    o_gdn = pl.pallas_call(
        functools.partial(_gdn_kernel, ts=ts_gdn),
        grid=(B, S // ts_gdn),
        in_specs=[row_spec(hd, ts_gdn), row_spec(hd, ts_gdn), row_spec(hd, ts_gdn), row_spec(LANES, ts_gdn),
                  _const_spec((1, HEAD_DIM))],
        out_specs=row_spec(hd, ts_gdn),
        out_shape=bs_shape(hd),
        scratch_shapes=[pltpu.VMEM((HEADS, HEAD_DIM, HEAD_DIM), F32)],
        compiler_params=arb2,
        name="gdn",
    )(gq, gk, gv, gg, gdn_norm.reshape(1, HEAD_DIM))

    return pl.pallas_call(
        _merge_kernel,
        grid=(B, S // ts_merge),
        in_specs=[row_spec(D, ts_merge), row_spec(hd, ts_merge), row_spec(hd, ts_merge), row_spec(hd, ts_merge),
                  row_spec(2 * hd, ts_merge), _const_spec((D_MIX, D)), _const_spec((1, D))],
        out_specs=row_spec(D, ts_merge),
        out_shape=bs_shape(D, F32),
        compiler_params=arb2,
        name="merge",
    )(x, o_mla, o_gdn, om, gate, w_out.astype(BF16), out_gain.reshape(1, D))


def _tile(n, pref):
    return pref if n % pref == 0 else n


def kernel(x, mem, positions, norm_in, w_in, q_a_norm, w_q_b, kv_a_norm, w_kv_b, gdn_conv, gdn_a_log,
           gdn_dt_bias, gdn_norm, mem_norm, w_mem_kv, w_out, norm_final):
    B, S, D = x.shape
    M = mem.shape[1]
    depth = norm_in.shape[0]
    assert depth == 1, "the final norm is fused into the single layer's merge kernel"
    hd = HEADS * HEAD_DIM
    half = MLA_ROPE // 2
    inv_freq = 1.0 / (ROPE_THETA ** (jnp.arange(half, dtype=F32) / half))
    invf = jnp.tile(inv_freq, LANES // half).reshape(1, LANES)
    sgn = jnp.concatenate([-jnp.ones((half,), F32), jnp.ones((half,), F32),
                           jnp.zeros((LANES - MLA_ROPE,), F32)]).reshape(1, LANES)
    pos3 = positions.reshape(B, S, 1)
    l = 0
    mk, mv = pl.pallas_call(
        _memkv_kernel,
        grid=(B,),
        in_specs=[pl.BlockSpec((1, M, D), lambda b: (b, 0, 0)), _const_spec((1, D)), _const_spec((D, 2 * hd))],
        out_specs=[pl.BlockSpec((1, M, hd), lambda b: (b, 0, 0))] * 2,
        out_shape=[jax.ShapeDtypeStruct((B, M, hd), BF16)] * 2,
        compiler_params=pltpu.CompilerParams(dimension_semantics=("arbitrary",), vmem_limit_bytes=VMEM_LIMIT),
        name="memkv",
    )(mem, mem_norm[l].reshape(1, D), w_mem_kv[l].astype(BF16))
    return _layer(x, mk, mv, pos3, invf, sgn, norm_in[l], w_in[l], q_a_norm[l], w_q_b[l], kv_a_norm[l], w_kv_b[l],
                  gdn_conv[l], gdn_a_log[l], gdn_dt_bias[l], gdn_norm[l], w_out[l], norm_final,
                  ts_proj=_tile(S, 512), ts_gdn=_tile(S, 256), tq=_tile(S, 256), ts_merge=_tile(S, 512))
```

```python
import functools

import jax
import jax.numpy as jnp
import numpy as np
from jax import lax
from jax.experimental import pallas as pl
from jax.experimental.pallas import tpu as pltpu

F32 = jnp.float32
BF16 = jnp.bfloat16

D_MODEL = 1024
HEADS = 4
HEAD_DIM = 128
MLA_ROPE = 64
Q_LORA = 384
KV_LORA = 256
ROPE_THETA = 10000.0
GDN_CONV = 4
GDN_CHUNK = 64
GDN_QKV = 3 * HEADS * HEAD_DIM
D_MIX = 3 * HEADS * HEAD_DIM
IN_SPLITS = (Q_LORA, KV_LORA, MLA_ROPE, GDN_QKV, HEADS, HEADS, HEADS * HEAD_DIM, D_MIX)
EPS = 1e-6
MLA_SCALE = (HEAD_DIM + MLA_ROPE) ** -0.5
LOG2E = 1.4426950408889634
MEM_SCALE = HEAD_DIM ** -0.5
GDN_QSCALE = HEAD_DIM ** -0.5

LANES = 128
SUBLANES = 8
QK_WIDTH = 2 * LANES

C_CQ = 0
C_CKV = C_CQ + Q_LORA
C_KR = C_CKV + KV_LORA
C_GDN = C_KR + LANES
C_AB = C_GDN + GDN_QKV
C_MQ = C_AB + LANES
C_GATE = C_MQ + HEADS * HEAD_DIM
C_END = C_GATE + D_MIX

VMEM_LIMIT = 56 * 1024 * 1024


def _dot(a, b, precision=None):
    return jnp.dot(a, b, preferred_element_type=F32, precision=precision)


def _dot_nt(a, b):
    return lax.dot_general(a, b, (((1,), (1,)), ((), ())), preferred_element_type=F32)


def _dot_tn(a, b):
    return lax.dot_general(a, b, (((0,), (0,)), ((), ())), preferred_element_type=F32)


def _rms(t, gain):
    return t * lax.rsqrt(jnp.mean(t * t, axis=-1, keepdims=True) + EPS) * gain


def _sigmoid(t):
    return 1.0 / (1.0 + jnp.exp(-t))


def _silu(t):
    return t * _sigmoid(t)


def _memkv_kernel(mem_ref, gain_ref, w_ref, mk_out, mv_out):
    hm = _rms(mem_ref[0], gain_ref[...]).astype(BF16)
    kv = _dot(hm, w_ref[...])
    half = HEADS * HEAD_DIM
    mk_out[0] = kv[:, :half].astype(BF16)
    mv_out[0] = kv[:, half:].astype(BF16)


def _proj_kernel(x_ref, pos_ref, nin_ref, w_ref, qan_ref, wqb_ref, kvan_ref, wkk_ref, wkv_ref,
                 conv_ref, alog_ref, dtb_ref, invf_ref, sgn_ref, mk_ref, mv_ref,
                 q_out, k_out, v_out, gq_out, gk_out, gv_out, gg_out, ggt_out, om_out, gate_out,
                 cbuf, *, ts):
    j = pl.program_id(1)
    hb = _rms(x_ref[0], nin_ref[...]).astype(BF16)

    ang = pos_ref[0].astype(F32) * invf_ref[...]
    sgn = sgn_ref[...]
    cosm = jnp.cos(ang) * jnp.abs(sgn)
    sinm = jnp.sin(ang) * sgn

    def rope(r):
        return r * cosm + pltpu.roll(r, MLA_ROPE, 1) * sinm

    qscale = MLA_SCALE * LOG2E
    cq = _dot(hb, w_ref[:, C_CQ:C_CKV])
    qf = _dot(_rms(cq, qan_ref[...]).astype(BF16), wqb_ref[...])
    for h in range(HEADS):
        lo = QK_WIDTH * h
        q_out[0, :, lo:lo + LANES] = (qf[:, lo:lo + LANES] * qscale).astype(BF16)
        q_out[0, :, lo + LANES:lo + QK_WIDTH] = (rope(qf[:, lo + LANES:lo + QK_WIDTH]) * qscale).astype(BF16)

    ckv = _dot(hb, w_ref[:, C_CKV:C_KR])
    ckvn = _rms(ckv, kvan_ref[...]).astype(BF16)
    kn = _dot(ckvn, wkk_ref[...])
    v_out[0] = _dot(ckvn, wkv_ref[...]).astype(BF16)
    kr = rope(_dot(hb, w_ref[:, C_KR:C_GDN])).astype(BF16)
    for h in range(HEADS):
        lo = QK_WIDTH * h
        k_out[0, :, lo:lo + LANES] = kn[:, LANES * h:LANES * (h + 1)].astype(BF16)
        k_out[0, :, lo + LANES:lo + QK_WIDTH] = kr

    @pl.when(j == 0)
    def _():
        cbuf[0:SUBLANES, :] = jnp.zeros((SUBLANES, GDN_QKV), F32)

    @pl.when(j > 0)
    def _():
        cbuf[0:SUBLANES, :] = cbuf[ts:ts + SUBLANES, :]

    cbuf[SUBLANES:ts + SUBLANES, :] = _dot(hb, w_ref[:, C_GDN:C_AB])
    outs = (gq_out, gk_out, gv_out)
    for g in range(GDN_QKV // LANES):
        cols = slice(LANES * g, LANES * (g + 1))
        base = SUBLANES - (GDN_CONV - 1)
        acc = conv_ref[0:1, cols] * cbuf[base:base + ts, cols]
        for t in range(1, GDN_CONV):
            acc = acc + conv_ref[t:t + 1, cols] * cbuf[base + t:base + t + ts, cols]
        y = _silu(acc)
        if g < 2 * HEADS:
            y = y * lax.rsqrt(jnp.sum(y * y, axis=-1, keepdims=True) + EPS)
        if g < HEADS:
            y = y * GDN_QSCALE
        outs[g // HEADS][0, :, LANES * (g % HEADS):LANES * (g % HEADS + 1)] = y.astype(BF16)

    ab = _dot(hb, w_ref[:, C_AB:C_MQ])
    z = ab + dtb_ref[...]
    softplus = jnp.maximum(z, 0.0) + jnp.log1p(jnp.exp(-jnp.abs(z)))
    gcum = -jnp.exp(alog_ref[...]) * softplus
    lane = lax.broadcasted_iota(jnp.int32, ab.shape, 1)
    pos_in_chunk = lax.broadcasted_iota(jnp.int32, ab.shape, 0) % GDN_CHUNK
    shift = 1
    while shift < GDN_CHUNK:
        gcum = gcum + jnp.where(pos_in_chunk >= shift, pltpu.roll(gcum, shift, 0), 0.0)
        shift *= 2
    gg_out[0] = jnp.where(lane < HEADS, gcum, _sigmoid(ab))
    ggt_out[0] = gcum.T[0:SUBLANES, :]

    gs = _silu(_dot(hb, w_ref[:, C_GATE:C_END]))
    gate_out[0] = gs[:, :2 * HEADS * HEAD_DIM].astype(BF16)
    mq = _dot(hb, w_ref[:, C_MQ:C_GATE])
    for h in range(HEADS):
        cols = slice(HEAD_DIM * h, HEAD_DIM * (h + 1))
        s = _dot_nt(mq[:, cols].astype(BF16), mk_ref[0, :, cols]) * MEM_SCALE
        m = jnp.max(s, axis=-1, keepdims=True)
        p = jnp.exp(s - m)
        l = jnp.sum(p, axis=-1, keepdims=True)
        o = _dot(p.astype(BF16), mv_ref[0, :, cols]) / l
        gcol = slice(2 * HEADS * HEAD_DIM + HEAD_DIM * h, 2 * HEADS * HEAD_DIM + HEAD_DIM * (h + 1))
        om_out[0, :, cols] = (o * gs[:, gcol]).astype(BF16)


def _mla_kernel(q_ref, k_ref, v_ref, o_ref, *, tq):
    i = pl.program_id(1)
    heads = range(HEADS)

    def step(jk, carry, masked):
        start = pl.multiple_of(jk * tq, tq)
        ss = [_dot_nt(q_ref[0, :, QK_WIDTH * h:QK_WIDTH * (h + 1)],
                      k_ref[0, pl.ds(start, tq), QK_WIDTH * h:QK_WIDTH * (h + 1)]) for h in heads]
        if masked:
            row = lax.broadcasted_iota(jnp.int32, (tq, tq), 0)
            col = lax.broadcasted_iota(jnp.int32, (tq, tq), 1)
            ss = [jnp.where(row >= col, s, -jnp.inf) for s in ss]
        m_new = [jnp.maximum(carry[h][0], jnp.max(ss[h], axis=-1, keepdims=True)) for h in heads]
        ps = [jnp.exp2(ss[h] - m_new[h]) for h in heads]
        alpha = [jnp.exp2(carry[h][0] - m_new[h]) for h in heads]
        ls = [alpha[h] * carry[h][1] + jnp.sum(ps[h], axis=-1, keepdims=True) for h in heads]
        pvs = [_dot(ps[h].astype(BF16), v_ref[0, pl.ds(start, tq), HEAD_DIM * h:HEAD_DIM * (h + 1)]) for h in heads]
        return tuple((m_new[h], ls[h], alpha[h] * carry[h][2] + pvs[h]) for h in heads)

    init = tuple((jnp.full((tq, 1), -jnp.inf, F32), jnp.zeros((tq, 1), F32), jnp.zeros((tq, HEAD_DIM), F32))
                 for _ in heads)
    carry = lax.fori_loop(0, i, lambda jk, c: step(jk, c, False), init)
    final = step(i, carry, True)
    for h in heads:
        _, l, acc = final[h]
        o_ref[0, :, HEAD_DIM * h:HEAD_DIM * (h + 1)] = (acc / l).astype(BF16)


def _pair_blockdiag(t, lo_half):
    return jnp.concatenate([jnp.where(lo_half, t, 0.0), jnp.where(lo_half, 0.0, t)], axis=0).astype(BF16)


def _gdn_kernel(q_ref, k_ref, v_ref, gg_ref, ggt_ref, gn_ref, o_ref, s_ref, *, ts):
    j = pl.program_id(1)

    @pl.when(j == 0)
    def _():
        s_ref[...] = jnp.zeros(s_ref.shape, F32)

    c = GDN_CHUNK
    c2 = 2 * c
    npairs = ts // c2
    row = lax.broadcasted_iota(jnp.int32, (c, c2), 0)
    lane = lax.broadcasted_iota(jnp.int32, (c, c2), 1)
    lo_half = lane < c
    col = jnp.where(lo_half, lane, lane - c)
    incl = row >= col
    strict = row > col
    eye = jnp.where(row == col, 1.0, 0.0)
    zeros_k = jnp.zeros((c, HEAD_DIM), BF16)
    units = [(p2, h) for p2 in range(npairs) for h in range(HEADS)]

    pre = {}
    for (p2, h) in units:
        r2 = slice(c2 * p2, c2 * (p2 + 1))
        cols = slice(HEAD_DIM * h, HEAD_DIM * (h + 1))
        gg = gg_ref[0, r2, :]
        gcol = gg[:, h:h + 1]
        beta = gg[:, HEADS + h:HEADS + h + 1]
        grow = ggt_ref[0, h:h + 1, r2]
        glast = (gg[c - 1:c, h:h + 1], gg[c2 - 1:c2, h:h + 1])
        glast_col = jnp.concatenate([jnp.broadcast_to(glast[0], (c, 1)), jnp.broadcast_to(glast[1], (c, 1))], axis=0)
        q2 = q_ref[0, r2, cols]
        k2 = k_ref[0, r2, cols]
        kf = k2.astype(F32)
        kbeta = kf * beta
        kbeta_b = kbeta.astype(BF16)
        vbeta_b = (v_ref[0, r2, cols].astype(F32) * beta).astype(BF16)
        eg = jnp.exp(gcol)
        lhs = jnp.concatenate([jnp.concatenate([kbeta_b[:c], kbeta_b[c:]], axis=1),
                               jnp.concatenate([q2[:c], q2[c:]], axis=1)], axis=0)
        rhs = jnp.concatenate([jnp.concatenate([k2[:c], zeros_k], axis=1),
                               jnp.concatenate([zeros_k, k2[c:]], axis=1)], axis=0)
        gcol_pair = jnp.where(lo_half, gcol[:c], gcol[c:])
        pre[(p2, h)] = dict(
            lhs=lhs, rhs=rhs, glast=glast,
            vk=jnp.concatenate([vbeta_b, (kbeta * eg).astype(BF16)], axis=1),
            qg=(q2.astype(F32) * eg).astype(BF16),
            kdec=(kf * jnp.exp(glast_col - gcol)).astype(BF16),
            decay=jnp.exp(jnp.where(incl, gcol_pair - grow, -jnp.inf)))

    kq = {u: _dot_nt(pre[u]["lhs"], pre[u]["rhs"]) for u in units}
    a_pair = {u: kq[u][c:] * pre[u]["decay"] for u in units}
    m = {u: -jnp.where(strict, kq[u][:c] * pre[u]["decay"], 0.0) for u in units}

    p = {u: eye + m[u] for u in units}
    m = {u: _dot(m[u].astype(BF16), _pair_blockdiag(m[u], lo_half)) for u in units}
    for _ in range(int(np.log2(c)) - 2):
        pm = {u: _dot(jnp.concatenate([p[u], m[u]], axis=0).astype(BF16), _pair_blockdiag(m[u], lo_half))
              for u in units}
        p = {u: p[u] + pm[u][:c] for u in units}
        m = {u: pm[u][c:] for u in units}
    pm = {u: _dot(p[u].astype(BF16), _pair_blockdiag(m[u], lo_half)) for u in units}
    t_pair = {u: p[u] + pm[u] for u in units}

    uw = {u: _dot(_pair_blockdiag(t_pair[u], lo_half), pre[u]["vk"]) for u in units}
    a_chunks = {u: (a_pair[u][:, :c].astype(BF16), pltpu.roll(a_pair[u], c, 1)[:, :c].astype(BF16)) for u in units}

    heads = range(HEADS)
    state = [s_ref[h] for h in heads]
    for p2 in range(npairs):
        for ci in range(2):
            rc = slice(c * ci, c * (ci + 1))
            ws_qs = [_dot(jnp.concatenate([uw[(p2, h)][rc, HEAD_DIM:].astype(BF16), pre[(p2, h)]["qg"][rc]], axis=0),
                          state[h].astype(BF16)) for h in heads]
            v_new = [(uw[(p2, h)][rc, :HEAD_DIM] - ws_qs[h][:c]).astype(BF16) for h in heads]
            o_intra = [_dot(a_chunks[(p2, h)][ci], v_new[h]) for h in heads]
            ds = [_dot_tn(pre[(p2, h)]["kdec"][rc], v_new[h]) for h in heads]
            state = [state[h] * jnp.exp(pre[(p2, h)]["glast"][ci]) + ds[h] for h in heads]
            for h in heads:
                o_ref[0, c2 * p2 + c * ci:c2 * p2 + c * (ci + 1), HEAD_DIM * h:HEAD_DIM * (h + 1)] = _rms(
                    ws_qs[h][c:] + o_intra[h], gn_ref[...]).astype(BF16)
    for h in heads:
        s_ref[h] = state[h]


def _merge_kernel(x_ref, omla_ref, ogdn_ref, om_ref, gate_ref, wout_ref, nf_ref, out_ref):
    half = HEADS * HEAD_DIM
    gate = gate_ref[0]
    acc = _dot(omla_ref[0] * gate[:, :half], wout_ref[0:half, :])
    acc = acc + _dot(ogdn_ref[0] * gate[:, half:], wout_ref[half:2 * half, :])
    acc = acc + _dot(om_ref[0], wout_ref[2 * half:, :])
    out_ref[0] = _rms(x_ref[0] + acc, nf_ref[...])


def _const_spec(shape):
    nd = len(shape)
    return pl.BlockSpec(shape, lambda *_: (0,) * nd)


def _pack_weights(w_in, w_q_b, w_kv_b):
    o = np.cumsum((0,) + IN_SPLITS)
    w_cq, w_ckv, w_kr, w_gdn, w_a, w_b, w_mq, w_gate = (w_in[:, o[i]:o[i + 1]] for i in range(8))
    half = MLA_ROPE // 2
    w_kr2 = jnp.concatenate([w_kr, w_kr[:, half:], w_kr[:, :half]], axis=1)
    w_ab = jnp.concatenate([w_a, w_b, jnp.zeros((D_MODEL, LANES - 2 * HEADS), w_in.dtype)], axis=1)
    w_all = jnp.concatenate([w_cq, w_ckv, w_kr2, w_gdn, w_ab, w_mq, w_gate], axis=1).astype(BF16)
    wq = w_q_b.reshape(Q_LORA, HEADS, HEAD_DIM + MLA_ROPE)
    nope, ropec = wq[..., :HEAD_DIM], wq[..., HEAD_DIM:]
    wqb = jnp.concatenate([nope, ropec, ropec[..., half:], ropec[..., :half]], axis=-1)
    wqb = wqb.reshape(Q_LORA, HEADS * QK_WIDTH).astype(BF16)
    wkv = w_kv_b.reshape(KV_LORA, HEADS, 2 * HEAD_DIM)
    wkk = wkv[..., :HEAD_DIM].reshape(KV_LORA, HEADS * HEAD_DIM).astype(BF16)
    wkvv = wkv[..., HEAD_DIM:].reshape(KV_LORA, HEADS * HEAD_DIM).astype(BF16)
    return w_all, wqb, wkk, wkvv


def _lane_row(vec):
    return jnp.zeros((1, LANES), F32).at[0, :vec.shape[0]].set(vec.astype(F32))


def _layer(x, mk, mv, pos3, invf, sgn, norm_in, w_in, q_a_norm, w_q_b, kv_a_norm, w_kv_b, gdn_conv,
           gdn_a_log, gdn_dt_bias, gdn_norm, w_out, out_gain, *, ts_proj, ts_gdn, tq, ts_merge):
    B, S, D = x.shape
    hd = HEADS * HEAD_DIM
    w_all, wqb, wkk, wkvv = _pack_weights(w_in, w_q_b, w_kv_b)
    M = mk.shape[1]
    arb2 = pltpu.CompilerParams(dimension_semantics=("arbitrary", "arbitrary"), vmem_limit_bytes=VMEM_LIMIT)

    def row_spec(width, ts):
        return pl.BlockSpec((1, ts, width), lambda b, j: (b, j, 0))

    def bs_shape(width, dtype=BF16):
        return jax.ShapeDtypeStruct((B, S, width), dtype)

    q, k, v, gq, gk, gv, gg, ggt, om, gate = pl.pallas_call(
        functools.partial(_proj_kernel, ts=ts_proj),
        grid=(B, S // ts_proj),
        in_specs=[row_spec(D, ts_proj), row_spec(1, ts_proj), _const_spec((1, D)), _const_spec(w_all.shape),
                  _const_spec((1, Q_LORA)), _const_spec(wqb.shape), _const_spec((1, KV_LORA)),
                  _const_spec(wkk.shape), _const_spec(wkvv.shape), _const_spec((GDN_CONV, GDN_QKV)),
                  _const_spec((1, LANES)), _const_spec((1, LANES)), _const_spec((1, LANES)), _const_spec((1, LANES)),
                  pl.BlockSpec((1, M, hd), lambda b, j: (b, 0, 0)), pl.BlockSpec((1, M, hd), lambda b, j: (b, 0, 0))],
        out_specs=[row_spec(HEADS * QK_WIDTH, ts_proj), row_spec(HEADS * QK_WIDTH, ts_proj), row_spec(hd, ts_proj),
                   row_spec(hd, ts_proj), row_spec(hd, ts_proj), row_spec(hd, ts_proj), row_spec(LANES, ts_proj),
                   pl.BlockSpec((1, SUBLANES, ts_proj), lambda b, j: (b, 0, j)),
                   row_spec(hd, ts_proj), row_spec(2 * hd, ts_proj)],
        out_shape=[bs_shape(HEADS * QK_WIDTH), bs_shape(HEADS * QK_WIDTH), bs_shape(hd), bs_shape(hd), bs_shape(hd),
                   bs_shape(hd), bs_shape(LANES, F32), jax.ShapeDtypeStruct((B, SUBLANES, S), F32),
                   bs_shape(hd), bs_shape(2 * hd)],
        scratch_shapes=[pltpu.VMEM((ts_proj + SUBLANES, GDN_QKV), F32)],
        compiler_params=arb2,
        name="proj",
    )(x, pos3, norm_in.reshape(1, D), w_all, q_a_norm.reshape(1, Q_LORA), wqb, kv_a_norm.reshape(1, KV_LORA),
      wkk, wkvv, gdn_conv, _lane_row(gdn_a_log), _lane_row(gdn_dt_bias), invf, sgn, mk, mv)

    o_mla = pl.pallas_call(
        functools.partial(_mla_kernel, tq=tq),
        grid=(B, S // tq),
        in_specs=[pl.BlockSpec((1, tq, HEADS * QK_WIDTH), lambda b, i: (b, i, 0)),
                  pl.BlockSpec((1, S, HEADS * QK_WIDTH), lambda b, i: (b, 0, 0)),
                  pl.BlockSpec((1, S, hd), lambda b, i: (b, 0, 0))],
        out_specs=pl.BlockSpec((1, tq, hd), lambda b, i: (b, i, 0)),
        out_shape=bs_shape(hd),
        compiler_params=arb2,
        name="mla",
    )(q, k, v)

    o_gdn = pl.pallas_call(
        functools.partial(_gdn_kernel, ts=ts_gdn),
        grid=(B, S // ts_gdn),
        in_specs=[row_spec(hd, ts_gdn), row_spec(hd, ts_gdn), row_spec(hd, ts_gdn), row_spec(LANES, ts_gdn),
                  pl.BlockSpec((1, SUBLANES, ts_gdn), lambda b, j: (b, 0, j)), _const_spec((1, HEAD_DIM))],
        out_specs=row_spec(hd, ts_gdn),
        out_shape=bs_shape(hd),
        scratch_shapes=[pltpu.VMEM((HEADS, HEAD_DIM, HEAD_DIM), F32)],
        compiler_params=arb2,
        name="gdn",
    )(gq, gk, gv, gg, ggt, gdn_norm.reshape(1, HEAD_DIM))

    return pl.pallas_call(
        _merge_kernel,
        grid=(B, S // ts_merge),
        in_specs=[row_spec(D, ts_merge), row_spec(hd, ts_merge), row_spec(hd, ts_merge), row_spec(hd, ts_merge),
                  row_spec(2 * hd, ts_merge), _const_spec((D_MIX, D)), _const_spec((1, D))],
        out_specs=row_spec(D, ts_merge),
        out_shape=bs_shape(D, F32),
        compiler_params=arb2,
        name="merge",
    )(x, o_mla, o_gdn, om, gate, w_out.astype(BF16), out_gain.reshape(1, D))


def _tile(n, pref):
    return pref if n % pref == 0 else n


def kernel(x, mem, positions, norm_in, w_in, q_a_norm, w_q_b, kv_a_norm, w_kv_b, gdn_conv, gdn_a_log,
           gdn_dt_bias, gdn_norm, mem_norm, w_mem_kv, w_out, norm_final):
    B, S, D = x.shape
    M = mem.shape[1]
    depth = norm_in.shape[0]
    assert depth == 1, "the final norm is fused into the single layer's merge kernel"
    hd = HEADS * HEAD_DIM
    half = MLA_ROPE // 2
    inv_freq = 1.0 / (ROPE_THETA ** (jnp.arange(half, dtype=F32) / half))
    invf = jnp.tile(inv_freq, LANES // half).reshape(1, LANES)
    sgn = jnp.concatenate([-jnp.ones((half,), F32), jnp.ones((half,), F32),
                           jnp.zeros((LANES - MLA_ROPE,), F32)]).reshape(1, LANES)
    pos3 = positions.reshape(B, S, 1)
    l = 0
    mk, mv = pl.pallas_call(
        _memkv_kernel,
        grid=(B,),
        in_specs=[pl.BlockSpec((1, M, D), lambda b: (b, 0, 0)), _const_spec((1, D)), _const_spec((D, 2 * hd))],
        out_specs=[pl.BlockSpec((1, M, hd), lambda b: (b, 0, 0))] * 2,
        out_shape=[jax.ShapeDtypeStruct((B, M, hd), BF16)] * 2,
        compiler_params=pltpu.CompilerParams(dimension_semantics=("arbitrary",), vmem_limit_bytes=VMEM_LIMIT),
        name="memkv",
    )(mem, mem_norm[l].reshape(1, D), w_mem_kv[l].astype(BF16))
    return _layer(x, mk, mv, pos3, invf, sgn, norm_in[l], w_in[l], q_a_norm[l], w_q_b[l], kv_a_norm[l], w_kv_b[l],
                  gdn_conv[l], gdn_a_log[l], gdn_dt_bias[l], gdn_norm[l], w_out[l], norm_final,
                  ts_proj=_tile(S, 512), ts_gdn=_tile(S, 256), tq=_tile(S, 256), ts_merge=_tile(S, 512))
```

```python
import functools

import jax
import jax.numpy as jnp
import numpy as np
from jax import lax
from jax.experimental import pallas as pl
from jax.experimental.pallas import tpu as pltpu

F32 = jnp.float32
BF16 = jnp.bfloat16

D_MODEL = 1024
HEADS = 4
HEAD_DIM = 128
MLA_ROPE = 64
Q_LORA = 384
KV_LORA = 256
ROPE_THETA = 10000.0
GDN_CONV = 4
GDN_CHUNK = 64
GDN_QKV = 3 * HEADS * HEAD_DIM
D_MIX = 3 * HEADS * HEAD_DIM
IN_SPLITS = (Q_LORA, KV_LORA, MLA_ROPE, GDN_QKV, HEADS, HEADS, HEADS * HEAD_DIM, D_MIX)
EPS = 1e-6
MLA_SCALE = (HEAD_DIM + MLA_ROPE) ** -0.5
LOG2E = 1.4426950408889634
MEM_SCALE = HEAD_DIM ** -0.5
GDN_QSCALE = HEAD_DIM ** -0.5

LANES = 128
SUBLANES = 8
QK_WIDTH = 2 * LANES

C_CQ = 0
C_CKV = C_CQ + Q_LORA
C_KR = C_CKV + KV_LORA
C_GDN = C_KR + LANES
C_AB = C_GDN + GDN_QKV
C_MQ = C_AB + LANES
C_GATE = C_MQ + HEADS * HEAD_DIM
C_END = C_GATE + D_MIX

VMEM_LIMIT = 56 * 1024 * 1024


def _dot(a, b, precision=None):
    return jnp.dot(a, b, preferred_element_type=F32, precision=precision)


def _dot_nt(a, b):
    return lax.dot_general(a, b, (((1,), (1,)), ((), ())), preferred_element_type=F32)


def _dot_tn(a, b):
    return lax.dot_general(a, b, (((0,), (0,)), ((), ())), preferred_element_type=F32)


def _rms(t, gain):
    return t * lax.rsqrt(jnp.mean(t * t, axis=-1, keepdims=True) + EPS) * gain


def _sigmoid(t):
    return 1.0 / (1.0 + jnp.exp(-t))


def _silu(t):
    half = 0.5 * t
    return half + half * jnp.tanh(half)


def _memkv_kernel(mem_ref, gain_ref, w_ref, mk_out, mv_out):
    hm = _rms(mem_ref[0], gain_ref[...]).astype(BF16)
    kv = _dot(hm, w_ref[...])
    half = HEADS * HEAD_DIM
    mk_out[0] = kv[:, :half].astype(BF16)
    mv_out[0] = kv[:, half:].astype(BF16)


def _rope_kernel(pos_ref, invf_ref, cos_out, sin_out):
    ang = pos_ref[...].astype(F32) * invf_ref[...]
    cos_out[...] = jnp.cos(ang)
    sin_out[...] = jnp.sin(ang)


def _proj_kernel(x_ref, cos_ref, sin_ref, nin_ref, w_ref, qan_ref, wqb_ref, kvan_ref, wkk_ref, wkv_ref,
                 conv_ref, alog_ref, dtb_ref, mk_ref, mv_ref,
                 q_out, k_out, v_out, gq_out, gk_out, gv_out, gg_out, ggt_out, om_out, gate_out,
                 cbuf, *, ts):
    j = pl.program_id(1)
    hd = HEADS * HEAD_DIM
    hb = _rms(x_ref[0], nin_ref[...]).astype(BF16)

    def proj(lo, hi):
        return _dot(hb, w_ref[:, lo:hi])

    @pl.when(j == 0)
    def _():
        cbuf[0:SUBLANES, :] = jnp.zeros((SUBLANES, GDN_QKV), F32)

    @pl.when(j > 0)
    def _():
        cbuf[0:SUBLANES, :] = cbuf[ts:ts + SUBLANES, :]

    group = 2 * LANES
    outs = (gq_out, gk_out, gv_out)

    def gdn_dot(g):
        cbuf[SUBLANES:ts + SUBLANES, group * g:group * (g + 1)] = proj(C_GDN + group * g, C_GDN + group * (g + 1))

    def gdn_epilogue(g):
        for sub in range(group // LANES):
            gi = (group // LANES) * g + sub
            cols = slice(LANES * gi, LANES * (gi + 1))
            base = SUBLANES - (GDN_CONV - 1)
            acc = conv_ref[0:1, cols] * cbuf[base:base + ts, cols]
            for t in range(1, GDN_CONV):
                acc = acc + conv_ref[t:t + 1, cols] * cbuf[base + t:base + t + ts, cols]
            y = _silu(acc)
            if gi < 2 * HEADS:
                y = y * lax.rsqrt(jnp.sum(y * y, axis=-1, keepdims=True) + EPS)
            if gi < HEADS:
                y = y * GDN_QSCALE
            outs[gi // HEADS][0, :, LANES * (gi % HEADS):LANES * (gi % HEADS + 1)] = y.astype(BF16)

    cq = proj(C_CQ, C_CKV)
    ckv = proj(C_CKV, C_KR)
    kr_raw = proj(C_KR, C_GDN)
    cqn = _rms(cq, qan_ref[...]).astype(BF16)
    gdn_dot(0)
    ckvn = _rms(ckv, kvan_ref[...]).astype(BF16)
    qf = _dot(cqn, wqb_ref[...])

    c32, s32 = cos_ref[0], sin_ref[0]
    zpad = jnp.zeros((ts, LANES - MLA_ROPE), F32)
    cosm = jnp.concatenate([c32, c32, zpad], axis=1)
    sinm = jnp.concatenate([-s32, s32, zpad], axis=1)

    def rope(r):
        return r * cosm + pltpu.roll(r, MLA_ROPE, 1) * sinm

    gdn_dot(1)
    gdn_epilogue(0)
    kn = _dot(ckvn, wkk_ref[...])
    vv = _dot(ckvn, wkv_ref[...])
    qscale = MLA_SCALE * LOG2E
    for h in range(HEADS):
        lo = QK_WIDTH * h
        q_out[0, :, lo:lo + LANES] = (qf[:, lo:lo + LANES] * qscale).astype(BF16)
        q_out[0, :, lo + LANES:lo + QK_WIDTH] = (rope(qf[:, lo + LANES:lo + QK_WIDTH]) * qscale).astype(BF16)
    gdn_dot(2)
    gdn_epilogue(1)
    ab = proj(C_AB, C_MQ)
    mq = proj(C_MQ, C_GATE)
    v_out[0] = vv.astype(BF16)
    kr = rope(kr_raw).astype(BF16)
    for h in range(HEADS):
        lo = QK_WIDTH * h
        k_out[0, :, lo:lo + LANES] = kn[:, LANES * h:LANES * (h + 1)].astype(BF16)
        k_out[0, :, lo + LANES:lo + QK_WIDTH] = kr
    gdn_dot(3)
    gdn_epilogue(2)

    mscale = MEM_SCALE * LOG2E
    sc, pp, ll, oo = {}, {}, {}, {}

    def mem_qk(h):
        cols = slice(HEAD_DIM * h, HEAD_DIM * (h + 1))
        sc[h] = _dot_nt((mq[:, cols] * mscale).astype(BF16), mk_ref[0, :, cols])

    def mem_softmax(h):
        p = jnp.exp2(sc[h] - jnp.max(sc[h], axis=-1, keepdims=True))
        ll[h] = jnp.sum(p, axis=-1, keepdims=True)
        pp[h] = p.astype(BF16)

    def mem_pv(h):
        oo[h] = _dot(pp[h], mv_ref[0, :, HEAD_DIM * h:HEAD_DIM * (h + 1)])

    mem_qk(0)
    z = ab + dtb_ref[...]
    softplus = jnp.maximum(z, 0.0) + jnp.log1p(jnp.exp(-jnp.abs(z)))
    gcum = -jnp.exp(alog_ref[...]) * softplus
    lane = lax.broadcasted_iota(jnp.int32, ab.shape, 1)
    pos_in_chunk = lax.broadcasted_iota(jnp.int32, ab.shape, 0) % GDN_CHUNK
    shift = 1
    while shift < GDN_CHUNK:
        gcum = gcum + jnp.where(pos_in_chunk >= shift, pltpu.roll(gcum, shift, 0), 0.0)
        shift *= 2
    gg_out[0] = jnp.where(lane < HEADS, gcum, _sigmoid(ab))
    ggt_out[0] = gcum.T[0:SUBLANES, :]

    gdn_dot(4)
    gdn_epilogue(3)
    mem_qk(1)
    mem_softmax(0)
    gdn_dot(5)
    gdn_epilogue(4)
    mem_qk(2)
    mem_pv(0)
    mem_softmax(1)
    gate_out[0, :, 0:hd] = proj(C_GATE, C_GATE + hd).astype(BF16)
    gdn_epilogue(5)
    mem_qk(3)
    mem_pv(1)
    mem_softmax(2)
    gate_out[0, :, hd:2 * hd] = proj(C_GATE + hd, C_GATE + 2 * hd).astype(BF16)
    mem_pv(2)
    mem_softmax(3)
    gmem = _silu(proj(C_GATE + 2 * hd, C_END))
    mem_pv(3)
    for h in range(HEADS):
        cols = slice(HEAD_DIM * h, HEAD_DIM * (h + 1))
        om_out[0, :, cols] = (oo[h] / ll[h] * gmem[:, cols]).astype(BF16)


def _mla_kernel(q_ref, k_ref, v_ref, o_ref, *, tq):
    i = pl.program_id(1)
    heads = range(HEADS)

    def step(jk, carry, masked):
        start = pl.multiple_of(jk * tq, tq)
        ss, ps, alpha, m_new, ls, pvs = {}, {}, {}, {}, {}, {}

        def scores(h):
            s = _dot_nt(q_ref[0, :, QK_WIDTH * h:QK_WIDTH * (h + 1)],
                        k_ref[0, pl.ds(start, tq), QK_WIDTH * h:QK_WIDTH * (h + 1)])
            if masked:
                row = lax.broadcasted_iota(jnp.int32, (tq, tq), 0)
                col = lax.broadcasted_iota(jnp.int32, (tq, tq), 1)
                s = jnp.where(row >= col, s, -jnp.inf)
            ss[h] = s

        def softmax(h):
            m_new[h] = jnp.maximum(carry[h][0], jnp.max(ss[h], axis=-1, keepdims=True))
            p = jnp.exp2(ss[h] - m_new[h])
            alpha[h] = jnp.exp2(carry[h][0] - m_new[h])
            ls[h] = alpha[h] * carry[h][1] + jnp.sum(p, axis=-1, keepdims=True)
            ps[h] = p.astype(BF16)

        def values(h):
            pvs[h] = _dot(ps[h], v_ref[0, pl.ds(start, tq), HEAD_DIM * h:HEAD_DIM * (h + 1)])

        for t in range(HEADS + 2):
            if t < HEADS:
                scores(t)
            if 0 <= t - 1 < HEADS:
                softmax(t - 1)
            if 0 <= t - 2 < HEADS:
                values(t - 2)
        return tuple((m_new[h], ls[h], alpha[h] * carry[h][2] + pvs[h]) for h in heads)

    init = tuple((jnp.full((tq, 1), -jnp.inf, F32), jnp.zeros((tq, 1), F32), jnp.zeros((tq, HEAD_DIM), F32))
                 for _ in heads)
    carry = lax.fori_loop(0, i, lambda jk, c: step(jk, c, False), init)
    final = step(i, carry, True)
    for h in heads:
        _, l, acc = final[h]
        o_ref[0, :, HEAD_DIM * h:HEAD_DIM * (h + 1)] = (acc / l).astype(BF16)


def _pair_blockdiag(t, lo_half):
    return jnp.concatenate([jnp.where(lo_half, t, 0.0), jnp.where(lo_half, 0.0, t)], axis=0).astype(BF16)


def _gdn_kernel(q_ref, k_ref, v_ref, gg_ref, ggt_ref, gn_ref, o_ref, s_ref, *, ts):
    j = pl.program_id(1)

    @pl.when(j == 0)
    def _():
        s_ref[...] = jnp.zeros(s_ref.shape, F32)

    c = GDN_CHUNK
    c2 = 2 * c
    npairs = ts // c2
    row = lax.broadcasted_iota(jnp.int32, (c, c2), 0)
    lane = lax.broadcasted_iota(jnp.int32, (c, c2), 1)
    lo_half = lane < c
    col = jnp.where(lo_half, lane, lane - c)
    incl = row >= col
    strict = row > col
    eye = jnp.where(row == col, 1.0, 0.0)
    zeros_k = jnp.zeros((c, HEAD_DIM), BF16)
    units = [(p2, h) for p2 in range(npairs) for h in range(HEADS)]

    pre = {}
    for (p2, h) in units:
        r2 = slice(c2 * p2, c2 * (p2 + 1))
        cols = slice(HEAD_DIM * h, HEAD_DIM * (h + 1))
        gg = gg_ref[0, r2, :]
        gcol = gg[:, h:h + 1]
        beta = gg[:, HEADS + h:HEADS + h + 1]
        grow = ggt_ref[0, h:h + 1, r2]
        glast = (gg[c - 1:c, h:h + 1], gg[c2 - 1:c2, h:h + 1])
        glast_col = jnp.concatenate([jnp.broadcast_to(glast[0], (c, 1)), jnp.broadcast_to(glast[1], (c, 1))], axis=0)
        q2 = q_ref[0, r2, cols]
        k2 = k_ref[0, r2, cols]
        kf = k2.astype(F32)
        kbeta = kf * beta
        kbeta_b = kbeta.astype(BF16)
        vbeta_b = (v_ref[0, r2, cols].astype(F32) * beta).astype(BF16)
        eg = jnp.exp(gcol)
        lhs = jnp.concatenate([jnp.concatenate([kbeta_b[:c], kbeta_b[c:]], axis=1),
                               jnp.concatenate([q2[:c], q2[c:]], axis=1)], axis=0)
        rhs = jnp.concatenate([jnp.concatenate([k2[:c], zeros_k], axis=1),
                               jnp.concatenate([zeros_k, k2[c:]], axis=1)], axis=0)
        gcol_pair = jnp.where(lo_half, gcol[:c], gcol[c:])
        pre[(p2, h)] = dict(
            lhs=lhs, rhs=rhs, glast=glast,
            vk=jnp.concatenate([vbeta_b, (kbeta * eg).astype(BF16)], axis=1),
            qg=(q2.astype(F32) * eg).astype(BF16),
            kdec=(kf * jnp.exp(glast_col - gcol)).astype(BF16),
            decay=jnp.exp(jnp.where(incl, gcol_pair - grow, -jnp.inf)))

    kq = {u: _dot_nt(pre[u]["lhs"], pre[u]["rhs"]) for u in units}
    a_pair = {u: kq[u][c:] * pre[u]["decay"] for u in units}
    m = {u: -jnp.where(strict, kq[u][:c] * pre[u]["decay"], 0.0) for u in units}

    p = {u: eye + m[u] for u in units}
    m = {u: _dot(m[u].astype(BF16), _pair_blockdiag(m[u], lo_half)) for u in units}
    for _ in range(int(np.log2(c)) - 2):
        pm = {u: _dot(jnp.concatenate([p[u], m[u]], axis=0).astype(BF16), _pair_blockdiag(m[u], lo_half))
              for u in units}
        p = {u: p[u] + pm[u][:c] for u in units}
        m = {u: pm[u][c:] for u in units}
    pm = {u: _dot(p[u].astype(BF16), _pair_blockdiag(m[u], lo_half)) for u in units}
    t_pair = {u: p[u] + pm[u] for u in units}

    uw = {u: _dot(_pair_blockdiag(t_pair[u], lo_half), pre[u]["vk"]) for u in units}
    a_chunks = {u: (a_pair[u][:, :c].astype(BF16), pltpu.roll(a_pair[u], c, 1)[:, :c].astype(BF16)) for u in units}

    heads = range(HEADS)
    state = [s_ref[h] for h in heads]
    for p2 in range(npairs):
        for ci in range(2):
            rc = slice(c * ci, c * (ci + 1))
            ws_qs = [_dot(jnp.concatenate([uw[(p2, h)][rc, HEAD_DIM:].astype(BF16), pre[(p2, h)]["qg"][rc]], axis=0),
                          state[h].astype(BF16)) for h in heads]
            v_new = [(uw[(p2, h)][rc, :HEAD_DIM] - ws_qs[h][:c]).astype(BF16) for h in heads]
            o_intra = [_dot(a_chunks[(p2, h)][ci], v_new[h]) for h in heads]
            ds = [_dot_tn(pre[(p2, h)]["kdec"][rc], v_new[h]) for h in heads]
            state = [state[h] * jnp.exp(pre[(p2, h)]["glast"][ci]) + ds[h] for h in heads]
            for h in heads:
                o_ref[0, c2 * p2 + c * ci:c2 * p2 + c * (ci + 1), HEAD_DIM * h:HEAD_DIM * (h + 1)] = _rms(
                    ws_qs[h][c:] + o_intra[h], gn_ref[...]).astype(BF16)
    for h in heads:
        s_ref[h] = state[h]


def _merge_kernel(x_ref, omla_ref, ogdn_ref, om_ref, gate_ref, wout_ref, nf_ref, out_ref):
    half = HEADS * HEAD_DIM
    gate = _silu(gate_ref[0].astype(F32))
    acc = _dot((omla_ref[0].astype(F32) * gate[:, :half]).astype(BF16), wout_ref[0:half, :])
    acc = acc + _dot((ogdn_ref[0].astype(F32) * gate[:, half:]).astype(BF16), wout_ref[half:2 * half, :])
    acc = acc + _dot(om_ref[0], wout_ref[2 * half:, :])
    out_ref[0] = _rms(x_ref[0] + acc, nf_ref[...])


def _const_spec(shape):
    nd = len(shape)
    return pl.BlockSpec(shape, lambda *_: (0,) * nd)


def _pack_weights(w_in, w_q_b, w_kv_b):
    o = np.cumsum((0,) + IN_SPLITS)
    w_cq, w_ckv, w_kr, w_gdn, w_a, w_b, w_mq, w_gate = (w_in[:, o[i]:o[i + 1]] for i in range(8))
    half = MLA_ROPE // 2
    w_kr2 = jnp.concatenate([w_kr, w_kr[:, half:], w_kr[:, :half]], axis=1)
    w_ab = jnp.concatenate([w_a, w_b, jnp.zeros((D_MODEL, LANES - 2 * HEADS), w_in.dtype)], axis=1)
    w_all = jnp.concatenate([w_cq, w_ckv, w_kr2, w_gdn, w_ab, w_mq, w_gate], axis=1).astype(BF16)
    wq = w_q_b.reshape(Q_LORA, HEADS, HEAD_DIM + MLA_ROPE)
    nope, ropec = wq[..., :HEAD_DIM], wq[..., HEAD_DIM:]
    wqb = jnp.concatenate([nope, ropec, ropec[..., half:], ropec[..., :half]], axis=-1)
    wqb = wqb.reshape(Q_LORA, HEADS * QK_WIDTH).astype(BF16)
    wkv = w_kv_b.reshape(KV_LORA, HEADS, 2 * HEAD_DIM)
    wkk = wkv[..., :HEAD_DIM].reshape(KV_LORA, HEADS * HEAD_DIM).astype(BF16)
    wkvv = wkv[..., HEAD_DIM:].reshape(KV_LORA, HEADS * HEAD_DIM).astype(BF16)
    return w_all, wqb, wkk, wkvv


def _lane_row(vec):
    return jnp.zeros((1, LANES), F32).at[0, :vec.shape[0]].set(vec.astype(F32))


def _layer(x, mk, mv, cos, sin, norm_in, w_in, q_a_norm, w_q_b, kv_a_norm, w_kv_b, gdn_conv,
           gdn_a_log, gdn_dt_bias, gdn_norm, w_out, out_gain, *, ts_proj, ts_gdn, tq, ts_merge):
    B, S, D = x.shape
    hd = HEADS * HEAD_DIM
    w_all, wqb, wkk, wkvv = _pack_weights(w_in, w_q_b, w_kv_b)
    M = mk.shape[1]
    arb2 = pltpu.CompilerParams(dimension_semantics=("arbitrary", "arbitrary"), vmem_limit_bytes=VMEM_LIMIT)

    def row_spec(width, ts):
        return pl.BlockSpec((1, ts, width), lambda b, j: (b, j, 0))

    def bs_shape(width, dtype=BF16):
        return jax.ShapeDtypeStruct((B, S, width), dtype)

    q, k, v, gq, gk, gv, gg, ggt, om, gate = pl.pallas_call(
        functools.partial(_proj_kernel, ts=ts_proj),
        grid=(B, S // ts_proj),
        in_specs=[row_spec(D, ts_proj), row_spec(MLA_ROPE // 2, ts_proj), row_spec(MLA_ROPE // 2, ts_proj),
                  _const_spec((1, D)), _const_spec(w_all.shape),
                  _const_spec((1, Q_LORA)), _const_spec(wqb.shape), _const_spec((1, KV_LORA)),
                  _const_spec(wkk.shape), _const_spec(wkvv.shape), _const_spec((GDN_CONV, GDN_QKV)),
                  _const_spec((1, LANES)), _const_spec((1, LANES)),
                  pl.BlockSpec((1, M, hd), lambda b, j: (b, 0, 0)), pl.BlockSpec((1, M, hd), lambda b, j: (b, 0, 0))],
        out_specs=[row_spec(HEADS * QK_WIDTH, ts_proj), row_spec(HEADS * QK_WIDTH, ts_proj), row_spec(hd, ts_proj),
                   row_spec(hd, ts_proj), row_spec(hd, ts_proj), row_spec(hd, ts_proj), row_spec(LANES, ts_proj),
                   pl.BlockSpec((1, SUBLANES, ts_proj), lambda b, j: (b, 0, j)),
                   row_spec(hd, ts_proj), row_spec(2 * hd, ts_proj)],
        out_shape=[bs_shape(HEADS * QK_WIDTH), bs_shape(HEADS * QK_WIDTH), bs_shape(hd), bs_shape(hd), bs_shape(hd),
                   bs_shape(hd), bs_shape(LANES, F32), jax.ShapeDtypeStruct((B, SUBLANES, S), F32),
                   bs_shape(hd), bs_shape(2 * hd)],
        scratch_shapes=[pltpu.VMEM((ts_proj + SUBLANES, GDN_QKV), F32)],
        compiler_params=arb2,
        name="proj",
    )(x, cos, sin, norm_in.reshape(1, D), w_all, q_a_norm.reshape(1, Q_LORA), wqb, kv_a_norm.reshape(1, KV_LORA),
      wkk, wkvv, gdn_conv, _lane_row(gdn_a_log), _lane_row(gdn_dt_bias), mk, mv)

    o_mla = pl.pallas_call(
        functools.partial(_mla_kernel, tq=tq),
        grid=(B, S // tq),
        in_specs=[pl.BlockSpec((1, tq, HEADS * QK_WIDTH), lambda b, i: (b, i, 0)),
                  pl.BlockSpec((1, S, HEADS * QK_WIDTH), lambda b, i: (b, 0, 0)),
                  pl.BlockSpec((1, S, hd), lambda b, i: (b, 0, 0))],
        out_specs=pl.BlockSpec((1, tq, hd), lambda b, i: (b, i, 0)),
        out_shape=bs_shape(hd),
        compiler_params=arb2,
        name="mla",
    )(q, k, v)

    o_gdn = pl.pallas_call(
        functools.partial(_gdn_kernel, ts=ts_gdn),
        grid=(B, S // ts_gdn),
        in_specs=[row_spec(hd, ts_gdn), row_spec(hd, ts_gdn), row_spec(hd, ts_gdn), row_spec(LANES, ts_gdn),
                  pl.BlockSpec((1, SUBLANES, ts_gdn), lambda b, j: (b, 0, j)), _const_spec((1, HEAD_DIM))],
        out_specs=row_spec(hd, ts_gdn),
        out_shape=bs_shape(hd),
        scratch_shapes=[pltpu.VMEM((HEADS, HEAD_DIM, HEAD_DIM), F32)],
        compiler_params=arb2,
        name="gdn",
    )(gq, gk, gv, gg, ggt, gdn_norm.reshape(1, HEAD_DIM))

    return pl.pallas_call(
        _merge_kernel,
        grid=(B, S // ts_merge),
        in_specs=[row_spec(D, ts_merge), row_spec(hd, ts_merge), row_spec(hd, ts_merge), row_spec(hd, ts_merge),
                  row_spec(2 * hd, ts_merge), _const_spec((D_MIX, D)), _const_spec((1, D))],
        out_specs=row_spec(D, ts_merge),
        out_shape=bs_shape(D, F32),
        compiler_params=arb2,
        name="merge",
    )(x, o_mla, o_gdn, om, gate, w_out.astype(BF16), out_gain.reshape(1, D))


def _tile(n, pref):
    return pref if n % pref == 0 else n


def kernel(x, mem, positions, norm_in, w_in, q_a_norm, w_q_b, kv_a_norm, w_kv_b, gdn_conv, gdn_a_log,
           gdn_dt_bias, gdn_norm, mem_norm, w_mem_kv, w_out, norm_final):
    B, S, D = x.shape
    M = mem.shape[1]
    depth = norm_in.shape[0]
    assert depth == 1, "the final norm is fused into the single layer's merge kernel"
    hd = HEADS * HEAD_DIM
    half = MLA_ROPE // 2
    inv_freq = 1.0 / (ROPE_THETA ** (jnp.arange(half, dtype=F32) / half))
    per_row = LANES // half
    invf = jnp.tile(inv_freq, per_row).reshape(1, LANES)
    pos_rows = jnp.repeat(positions.reshape(B * S // per_row, per_row), half, axis=1)
    n_rows = B * S // per_row
    tr = _tile(n_rows, 1024)
    cos, sin = pl.pallas_call(
        _rope_kernel,
        grid=(n_rows // tr,),
        in_specs=[pl.BlockSpec((tr, LANES), lambda r: (r, 0)), _const_spec((1, LANES))],
        out_specs=[pl.BlockSpec((tr, LANES), lambda r: (r, 0))] * 2,
        out_shape=[jax.ShapeDtypeStruct((n_rows, LANES), F32)] * 2,
        compiler_params=pltpu.CompilerParams(dimension_semantics=("arbitrary",), vmem_limit_bytes=VMEM_LIMIT),
        name="rope",
    )(pos_rows, invf)
    cos = cos.reshape(B, S, half)
    sin = sin.reshape(B, S, half)
    l = 0
    mk, mv = pl.pallas_call(
        _memkv_kernel,
        grid=(B,),
        in_specs=[pl.BlockSpec((1, M, D), lambda b: (b, 0, 0)), _const_spec((1, D)), _const_spec((D, 2 * hd))],
        out_specs=[pl.BlockSpec((1, M, hd), lambda b: (b, 0, 0))] * 2,
        out_shape=[jax.ShapeDtypeStruct((B, M, hd), BF16)] * 2,
        compiler_params=pltpu.CompilerParams(dimension_semantics=("arbitrary",), vmem_limit_bytes=VMEM_LIMIT),
        name="memkv",
    )(mem, mem_norm[l].reshape(1, D), w_mem_kv[l].astype(BF16))
    return _layer(x, mk, mv, cos, sin, norm_in[l], w_in[l], q_a_norm[l], w_q_b[l], kv_a_norm[l], w_kv_b[l],
                  gdn_conv[l], gdn_a_log[l], gdn_dt_bias[l], gdn_norm[l], w_out[l], norm_final,
                  ts_proj=_tile(S, 512), ts_gdn=_tile(S, 256), tq=_tile(S, 512), ts_merge=_tile(S, 512))
```

```python
import functools

import jax
import jax.numpy as jnp
import numpy as np
from jax import lax
from jax.experimental import pallas as pl
from jax.experimental.pallas import tpu as pltpu

F32 = jnp.float32
BF16 = jnp.bfloat16

D_MODEL = 1024
HEADS = 4
HEAD_DIM = 128
MLA_ROPE = 64
Q_LORA = 384
KV_LORA = 256
ROPE_THETA = 10000.0
GDN_CONV = 4
GDN_CHUNK = 64
GDN_QKV = 3 * HEADS * HEAD_DIM
D_MIX = 3 * HEADS * HEAD_DIM
IN_SPLITS = (Q_LORA, KV_LORA, MLA_ROPE, GDN_QKV, HEADS, HEADS, HEADS * HEAD_DIM, D_MIX)
EPS = 1e-6
MLA_SCALE = (HEAD_DIM + MLA_ROPE) ** -0.5
LOG2E = 1.4426950408889634
MEM_SCALE = HEAD_DIM ** -0.5
GDN_QSCALE = HEAD_DIM ** -0.5

LANES = 128
SUBLANES = 8
QK_WIDTH = 2 * LANES
MLA_STRIP = 64

C_CQ = 0
C_CKV = C_CQ + Q_LORA
C_KR = C_CKV + KV_LORA
C_AB = C_KR + LANES
C_GDN = C_AB + LANES
C_MQ = C_GDN + GDN_QKV
C_GATE = C_MQ + HEADS * HEAD_DIM
C_END = C_GATE + D_MIX

VMEM_LIMIT = 56 * 1024 * 1024


def _dot(a, b, precision=None):
    return jnp.dot(a, b, preferred_element_type=F32, precision=precision)


def _dot_nt(a, b):
    return lax.dot_general(a, b, (((1,), (1,)), ((), ())), preferred_element_type=F32)


def _dot_tn(a, b):
    return lax.dot_general(a, b, (((0,), (0,)), ((), ())), preferred_element_type=F32)


def _rms(t, gain):
    return t * lax.rsqrt(jnp.mean(t * t, axis=-1, keepdims=True) + EPS) * gain


def _sigmoid(t):
    return 1.0 / (1.0 + jnp.exp(-t))


def _silu(t):
    half = 0.5 * t
    return half + half * jnp.tanh(half)


def _memkv_kernel(mem_ref, gain_ref, w_ref, mk_out, mv_out):
    hm = _rms(mem_ref[0], gain_ref[...]).astype(BF16)
    kv = _dot(hm, w_ref[...])
    half = HEADS * HEAD_DIM
    mk_out[0] = kv[:, :half].astype(BF16)
    mv_out[0] = kv[:, half:].astype(BF16)


def _rope_kernel(pos_ref, invf_ref, cos_out, sin_out):
    ang = pos_ref[...].astype(F32) * invf_ref[...]
    cos_out[...] = jnp.cos(ang)
    sin_out[...] = jnp.sin(ang)


def _proj_kernel(x_ref, cos_ref, sin_ref, nin_ref, w_ref, qan_ref, wqb_ref, kvan_ref, wkk_ref, wkv_ref,
                 conv_ref, alog_ref, dtb_ref, mk_ref, mv_ref,
                 q_out, k_out, v_out, gq_out, gk_out, gv_out, gg_out, ggt_out, om_out, gate_out,
                 cbuf, *, ts):
    j = pl.program_id(1)
    hd = HEADS * HEAD_DIM
    hb = _rms(x_ref[0], nin_ref[...]).astype(BF16)

    def proj(lo, hi):
        return _dot(hb, w_ref[:, lo:hi])

    @pl.when(j == 0)
    def _():
        cbuf[0:SUBLANES, :] = jnp.zeros((SUBLANES, GDN_QKV), F32)

    @pl.when(j > 0)
    def _():
        cbuf[0:SUBLANES, :] = cbuf[ts:ts + SUBLANES, :]

    group = 2 * LANES
    outs = (gq_out, gk_out, gv_out)

    def gdn_dot(g):
        cbuf[SUBLANES:ts + SUBLANES, group * g:group * (g + 1)] = proj(C_GDN + group * g, C_GDN + group * (g + 1))

    def gdn_epilogue(g):
        for sub in range(group // LANES):
            gi = (group // LANES) * g + sub
            cols = slice(LANES * gi, LANES * (gi + 1))
            base = SUBLANES - (GDN_CONV - 1)
            acc = conv_ref[0:1, cols] * cbuf[base:base + ts, cols]
            for t in range(1, GDN_CONV):
                acc = acc + conv_ref[t:t + 1, cols] * cbuf[base + t:base + t + ts, cols]
            y = _silu(acc)
            if gi < 2 * HEADS:
                y = y * lax.rsqrt(jnp.sum(y * y, axis=-1, keepdims=True) + EPS)
            if gi < HEADS:
                y = y * GDN_QSCALE
            outs[gi // HEADS][0, :, LANES * (gi % HEADS):LANES * (gi % HEADS + 1)] = y.astype(BF16)

    narrow = proj(C_CQ, C_GDN)
    cq = narrow[:, C_CQ:C_CKV]
    ckv = narrow[:, C_CKV:C_KR]
    kr_raw = narrow[:, C_KR:C_AB]
    ab = narrow[:, C_AB:C_GDN]
    cqn = _rms(cq, qan_ref[...]).astype(BF16)
    gdn_dot(0)
    ckvn = _rms(ckv, kvan_ref[...]).astype(BF16)
    qf = _dot(cqn, wqb_ref[...])

    c32, s32 = cos_ref[0], sin_ref[0]
    zpad = jnp.zeros((ts, LANES - MLA_ROPE), F32)
    cosm = jnp.concatenate([c32, c32, zpad], axis=1)
    sinm = jnp.concatenate([-s32, s32, zpad], axis=1)

    def rope(r):
        return r * cosm + pltpu.roll(r, MLA_ROPE, 1) * sinm

    gdn_dot(1)
    gdn_epilogue(0)
    kn = _dot(ckvn, wkk_ref[...])
    vv = _dot(ckvn, wkv_ref[...])
    qscale = MLA_SCALE * LOG2E
    for h in range(HEADS):
        lo = QK_WIDTH * h
        q_out[0, :, lo:lo + LANES] = (qf[:, lo:lo + LANES] * qscale).astype(BF16)
        q_out[0, :, lo + LANES:lo + QK_WIDTH] = (rope(qf[:, lo + LANES:lo + QK_WIDTH]) * qscale).astype(BF16)
    gdn_dot(2)
    gdn_epilogue(1)
    mq = proj(C_MQ, C_GATE)
    v_out[0] = vv.astype(BF16)
    kr = rope(kr_raw).astype(BF16)
    for h in range(HEADS):
        lo = QK_WIDTH * h
        k_out[0, :, lo:lo + LANES] = kn[:, LANES * h:LANES * (h + 1)].astype(BF16)
        k_out[0, :, lo + LANES:lo + QK_WIDTH] = kr
    gdn_dot(3)
    gdn_epilogue(2)

    mscale = MEM_SCALE * LOG2E
    sc, pp, ll, oo = {}, {}, {}, {}

    def mem_qk(h):
        cols = slice(HEAD_DIM * h, HEAD_DIM * (h + 1))
        sc[h] = _dot_nt((mq[:, cols] * mscale).astype(BF16), mk_ref[0, :, cols])

    def mem_softmax(h):
        p = jnp.exp2(sc[h] - jnp.max(sc[h], axis=-1, keepdims=True))
        ll[h] = jnp.sum(p, axis=-1, keepdims=True)
        pp[h] = p.astype(BF16)

    def mem_pv(h):
        oo[h] = _dot(pp[h], mv_ref[0, :, HEAD_DIM * h:HEAD_DIM * (h + 1)])

    mem_qk(0)
    z = ab + dtb_ref[...]
    softplus = jnp.maximum(z, 0.0) + jnp.log1p(jnp.exp(-jnp.abs(z)))
    gcum = -jnp.exp(alog_ref[...]) * softplus
    lane = lax.broadcasted_iota(jnp.int32, ab.shape, 1)
    pos_in_chunk = lax.broadcasted_iota(jnp.int32, ab.shape, 0) % GDN_CHUNK
    shift = 1
    while shift < GDN_CHUNK:
        gcum = gcum + jnp.where(pos_in_chunk >= shift, pltpu.roll(gcum, shift, 0), 0.0)
        shift *= 2
    gg_out[0] = jnp.where(lane < HEADS, gcum, _sigmoid(ab))
    ggt_out[0] = gcum.T[0:SUBLANES, :]

    gdn_dot(4)
    gdn_epilogue(3)
    mem_qk(1)
    mem_softmax(0)
    gdn_dot(5)
    gdn_epilogue(4)
    mem_qk(2)
    mem_pv(0)
    mem_softmax(1)
    gate_out[0, :, 0:hd] = proj(C_GATE, C_GATE + hd).astype(BF16)
    gdn_epilogue(5)
    mem_qk(3)
    mem_pv(1)
    mem_softmax(2)
    gate_out[0, :, hd:2 * hd] = proj(C_GATE + hd, C_GATE + 2 * hd).astype(BF16)
    mem_pv(2)
    mem_softmax(3)
    gmem = _silu(proj(C_GATE + 2 * hd, C_END))
    mem_pv(3)
    for h in range(HEADS):
        cols = slice(HEAD_DIM * h, HEAD_DIM * (h + 1))
        om_out[0, :, cols] = (oo[h] / ll[h] * gmem[:, cols]).astype(BF16)


def _mla_kernel(q_ref, k_ref, v_ref, o_ref, s_scr, p_scr, m_scr, l_scr, a_scr, acc_scr, *, tq):
    i = pl.program_id(1)
    heads = range(HEADS)
    strip = MLA_STRIP

    m_scr[...] = jnp.full(m_scr.shape, -jnp.inf, F32)
    l_scr[...] = jnp.zeros(l_scr.shape, F32)
    acc_scr[...] = jnp.zeros(acc_scr.shape, F32)

    def step(jk, masked):
        start = pl.multiple_of(jk * tq, tq)

        def scores(h):
            s_scr[h] = _dot_nt(q_ref[0, :, QK_WIDTH * h:QK_WIDTH * (h + 1)],
                               k_ref[0, pl.ds(start, tq), QK_WIDTH * h:QK_WIDTH * (h + 1)])

        def load_strip(h, r):
            s = s_scr[h, strip * r:strip * (r + 1), :]
            if masked:
                row = lax.broadcasted_iota(jnp.int32, s.shape, 0) + strip * r
                col = lax.broadcasted_iota(jnp.int32, s.shape, 1)
                s = jnp.where(row >= col, s, -jnp.inf)
            return s

        def lane_tiles(t):
            return [t[:, LANES * g:LANES * (g + 1)] for g in range(t.shape[1] // LANES)]

        def softmax(h):
            strips = range(tq // strip)
            rows = [slice(strip * r, strip * (r + 1)) for r in strips]
            part = [functools.reduce(jnp.maximum, lane_tiles(load_strip(h, r))) for r in strips]
            peak = [jnp.broadcast_to(jnp.max(t, axis=-1, keepdims=True), (strip, LANES)) for t in part]
            m_old = [m_scr[h, rows[r], :] for r in strips]
            m_new = [jnp.maximum(m_old[r], peak[r]) for r in strips]
            alpha = [jnp.exp2(m_old[r] - m_new[r]) for r in strips]
            for r in strips:
                m_scr[h, rows[r], :] = m_new[r]
                a_scr[h, rows[r], :] = alpha[r]
            part = []
            for r in strips:
                p = [jnp.exp2(t - m_new[r]) for t in lane_tiles(load_strip(h, r))]
                for g, t in enumerate(p):
                    p_scr[h, rows[r], LANES * g:LANES * (g + 1)] = t.astype(BF16)
                part.append(functools.reduce(jnp.add, p))
            total = [jnp.broadcast_to(jnp.sum(t, axis=-1, keepdims=True), (strip, LANES)) for t in part]
            for r in strips:
                l_scr[h, rows[r], :] = alpha[r] * l_scr[h, rows[r], :] + total[r]

        def values(h):
            pv = _dot(p_scr[h], v_ref[0, pl.ds(start, tq), HEAD_DIM * h:HEAD_DIM * (h + 1)])
            acc_scr[h] = a_scr[h] * acc_scr[h] + pv

        for t in range(HEADS + 2):
            if t < HEADS:
                scores(t)
            if 0 <= t - 1 < HEADS:
                softmax(t - 1)
            if 0 <= t - 2 < HEADS:
                values(t - 2)

    def body(jk, carry):
        step(jk, False)
        return carry

    lax.fori_loop(0, i, body, 0)
    step(i, True)
    for h in heads:
        o_ref[0, :, HEAD_DIM * h:HEAD_DIM * (h + 1)] = (acc_scr[h] / l_scr[h]).astype(BF16)


def _pair_blockdiag(t, lo_half):
    return jnp.concatenate([jnp.where(lo_half, t, 0.0), jnp.where(lo_half, 0.0, t)], axis=0).astype(BF16)


def _gdn_kernel(q_ref, k_ref, v_ref, gg_ref, ggt_ref, gn_ref, o_ref, s_ref, *, ts):
    j = pl.program_id(1)

    @pl.when(j == 0)
    def _():
        s_ref[...] = jnp.zeros(s_ref.shape, F32)

    c = GDN_CHUNK
    c2 = 2 * c
    npairs = ts // c2
    row = lax.broadcasted_iota(jnp.int32, (c, c2), 0)
    lane = lax.broadcasted_iota(jnp.int32, (c, c2), 1)
    lo_half = lane < c
    col = jnp.where(lo_half, lane, lane - c)
    incl = row >= col
    strict = row > col
    eye = jnp.where(row == col, 1.0, 0.0)
    zeros_k = jnp.zeros((c, HEAD_DIM), BF16)
    units = [(p2, h) for p2 in range(npairs) for h in range(HEADS)]

    pre = {}
    for (p2, h) in units:
        r2 = slice(c2 * p2, c2 * (p2 + 1))
        cols = slice(HEAD_DIM * h, HEAD_DIM * (h + 1))
        gg = gg_ref[0, r2, :]
        gcol = gg[:, h:h + 1]
        beta = gg[:, HEADS + h:HEADS + h + 1]
        grow = ggt_ref[0, h:h + 1, r2]
        glast = (gg[c - 1:c, h:h + 1], gg[c2 - 1:c2, h:h + 1])
        glast_col = jnp.concatenate([jnp.broadcast_to(glast[0], (c, 1)), jnp.broadcast_to(glast[1], (c, 1))], axis=0)
        q2 = q_ref[0, r2, cols]
        k2 = k_ref[0, r2, cols]
        kf = k2.astype(F32)
        kbeta = kf * beta
        kbeta_b = kbeta.astype(BF16)
        vbeta_b = (v_ref[0, r2, cols].astype(F32) * beta).astype(BF16)
        eg = jnp.exp(gcol)
        lhs = jnp.concatenate([jnp.concatenate([kbeta_b[:c], kbeta_b[c:]], axis=1),
                               jnp.concatenate([q2[:c], q2[c:]], axis=1)], axis=0)
        rhs = jnp.concatenate([jnp.concatenate([k2[:c], zeros_k], axis=1),
                               jnp.concatenate([zeros_k, k2[c:]], axis=1)], axis=0)
        gcol_pair = jnp.where(lo_half, gcol[:c], gcol[c:])
        pre[(p2, h)] = dict(
            lhs=lhs, rhs=rhs, glast=glast,
            vk=jnp.concatenate([vbeta_b, (kbeta * eg).astype(BF16)], axis=1),
            qg=(q2.astype(F32) * eg).astype(BF16),
            kdec=(kf * jnp.exp(glast_col - gcol)).astype(BF16),
            decay=jnp.exp(jnp.where(incl, gcol_pair - grow, -jnp.inf)))

    kq = {u: _dot_nt(pre[u]["lhs"], pre[u]["rhs"]) for u in units}
    a_pair = {u: kq[u][c:] * pre[u]["decay"] for u in units}
    m = {u: -jnp.where(strict, kq[u][:c] * pre[u]["decay"], 0.0) for u in units}

    p = {u: eye + m[u] for u in units}
    m = {u: _dot(m[u].astype(BF16), _pair_blockdiag(m[u], lo_half)) for u in units}
    for _ in range(int(np.log2(c)) - 2):
        pm = {u: _dot(jnp.concatenate([p[u], m[u]], axis=0).astype(BF16), _pair_blockdiag(m[u], lo_half))
              for u in units}
        p = {u: p[u] + pm[u][:c] for u in units}
        m = {u: pm[u][c:] for u in units}
    pm = {u: _dot(p[u].astype(BF16), _pair_blockdiag(m[u], lo_half)) for u in units}
    t_pair = {u: p[u] + pm[u] for u in units}

    uw = {u: _dot(_pair_blockdiag(t_pair[u], lo_half), pre[u]["vk"]) for u in units}
    a_chunks = {u: (a_pair[u][:, :c].astype(BF16), pltpu.roll(a_pair[u], c, 1)[:, :c].astype(BF16)) for u in units}

    heads = range(HEADS)
    state = [s_ref[h] for h in heads]
    for p2 in range(npairs):
        for ci in range(2):
            rc = slice(c * ci, c * (ci + 1))
            ws_qs = [_dot(jnp.concatenate([uw[(p2, h)][rc, HEAD_DIM:].astype(BF16), pre[(p2, h)]["qg"][rc]], axis=0),
                          state[h].astype(BF16)) for h in heads]
            v_new = [(uw[(p2, h)][rc, :HEAD_DIM] - ws_qs[h][:c]).astype(BF16) for h in heads]
            o_intra = [_dot(a_chunks[(p2, h)][ci], v_new[h]) for h in heads]
            ds = [_dot_tn(pre[(p2, h)]["kdec"][rc], v_new[h]) for h in heads]
            state = [state[h] * jnp.exp(pre[(p2, h)]["glast"][ci]) + ds[h] for h in heads]
            for h in heads:
                o_ref[0, c2 * p2 + c * ci:c2 * p2 + c * (ci + 1), HEAD_DIM * h:HEAD_DIM * (h + 1)] = _rms(
                    ws_qs[h][c:] + o_intra[h], gn_ref[...]).astype(BF16)
    for h in heads:
        s_ref[h] = state[h]


def _merge_kernel(x_ref, omla_ref, ogdn_ref, om_ref, gate_ref, wout_ref, nf_ref, out_ref):
    half = HEADS * HEAD_DIM
    gate = _silu(gate_ref[0].astype(F32))
    acc = _dot((omla_ref[0].astype(F32) * gate[:, :half]).astype(BF16), wout_ref[0:half, :])
    acc = acc + _dot((ogdn_ref[0].astype(F32) * gate[:, half:]).astype(BF16), wout_ref[half:2 * half, :])
    acc = acc + _dot(om_ref[0], wout_ref[2 * half:, :])
    out_ref[0] = _rms(x_ref[0] + acc, nf_ref[...])


def _const_spec(shape):
    nd = len(shape)
    return pl.BlockSpec(shape, lambda *_: (0,) * nd)


def _pack_weights(w_in, w_q_b, w_kv_b):
    o = np.cumsum((0,) + IN_SPLITS)
    w_cq, w_ckv, w_kr, w_gdn, w_a, w_b, w_mq, w_gate = (w_in[:, o[i]:o[i + 1]] for i in range(8))
    half = MLA_ROPE // 2
    w_kr2 = jnp.concatenate([w_kr, w_kr[:, half:], w_kr[:, :half]], axis=1)
    w_ab = jnp.concatenate([w_a, w_b, jnp.zeros((D_MODEL, LANES - 2 * HEADS), w_in.dtype)], axis=1)
    w_all = jnp.concatenate([w_cq, w_ckv, w_kr2, w_ab, w_gdn, w_mq, w_gate], axis=1).astype(BF16)
    wq = w_q_b.reshape(Q_LORA, HEADS, HEAD_DIM + MLA_ROPE)
    nope, ropec = wq[..., :HEAD_DIM], wq[..., HEAD_DIM:]
    wqb = jnp.concatenate([nope, ropec, ropec[..., half:], ropec[..., :half]], axis=-1)
    wqb = wqb.reshape(Q_LORA, HEADS * QK_WIDTH).astype(BF16)
    wkv = w_kv_b.reshape(KV_LORA, HEADS, 2 * HEAD_DIM)
    wkk = wkv[..., :HEAD_DIM].reshape(KV_LORA, HEADS * HEAD_DIM).astype(BF16)
    wkvv = wkv[..., HEAD_DIM:].reshape(KV_LORA, HEADS * HEAD_DIM).astype(BF16)
    return w_all, wqb, wkk, wkvv


def _lane_row(vec):
    return jnp.zeros((1, LANES), F32).at[0, :vec.shape[0]].set(vec.astype(F32))


def _layer(x, mk, mv, cos, sin, norm_in, w_in, q_a_norm, w_q_b, kv_a_norm, w_kv_b, gdn_conv,
           gdn_a_log, gdn_dt_bias, gdn_norm, w_out, out_gain, *, ts_proj, ts_gdn, tq, ts_merge):
    B, S, D = x.shape
    hd = HEADS * HEAD_DIM
    w_all, wqb, wkk, wkvv = _pack_weights(w_in, w_q_b, w_kv_b)
    M = mk.shape[1]
    arb2 = pltpu.CompilerParams(dimension_semantics=("arbitrary", "arbitrary"), vmem_limit_bytes=VMEM_LIMIT)

    def row_spec(width, ts):
        return pl.BlockSpec((1, ts, width), lambda b, j: (b, j, 0))

    def bs_shape(width, dtype=BF16):
        return jax.ShapeDtypeStruct((B, S, width), dtype)

    q, k, v, gq, gk, gv, gg, ggt, om, gate = pl.pallas_call(
        functools.partial(_proj_kernel, ts=ts_proj),
        grid=(B, S // ts_proj),
        in_specs=[row_spec(D, ts_proj), row_spec(MLA_ROPE // 2, ts_proj), row_spec(MLA_ROPE // 2, ts_proj),
                  _const_spec((1, D)), _const_spec(w_all.shape),
                  _const_spec((1, Q_LORA)), _const_spec(wqb.shape), _const_spec((1, KV_LORA)),
                  _const_spec(wkk.shape), _const_spec(wkvv.shape), _const_spec((GDN_CONV, GDN_QKV)),
                  _const_spec((1, LANES)), _const_spec((1, LANES)),
                  pl.BlockSpec((1, M, hd), lambda b, j: (b, 0, 0)), pl.BlockSpec((1, M, hd), lambda b, j: (b, 0, 0))],
        out_specs=[row_spec(HEADS * QK_WIDTH, ts_proj), row_spec(HEADS * QK_WIDTH, ts_proj), row_spec(hd, ts_proj),
                   row_spec(hd, ts_proj), row_spec(hd, ts_proj), row_spec(hd, ts_proj), row_spec(LANES, ts_proj),
                   pl.BlockSpec((1, SUBLANES, ts_proj), lambda b, j: (b, 0, j)),
                   row_spec(hd, ts_proj), row_spec(2 * hd, ts_proj)],
        out_shape=[bs_shape(HEADS * QK_WIDTH), bs_shape(HEADS * QK_WIDTH), bs_shape(hd), bs_shape(hd), bs_shape(hd),
                   bs_shape(hd), bs_shape(LANES, F32), jax.ShapeDtypeStruct((B, SUBLANES, S), F32),
                   bs_shape(hd), bs_shape(2 * hd)],
        scratch_shapes=[pltpu.VMEM((ts_proj + SUBLANES, GDN_QKV), F32)],
        compiler_params=arb2,
        name="proj",
    )(x, cos, sin, norm_in.reshape(1, D), w_all, q_a_norm.reshape(1, Q_LORA), wqb, kv_a_norm.reshape(1, KV_LORA),
      wkk, wkvv, gdn_conv, _lane_row(gdn_a_log), _lane_row(gdn_dt_bias), mk, mv)

    o_mla = pl.pallas_call(
        functools.partial(_mla_kernel, tq=tq),
        grid=(B, S // tq),
        in_specs=[pl.BlockSpec((1, tq, HEADS * QK_WIDTH), lambda b, i: (b, i, 0)),
                  pl.BlockSpec((1, S, HEADS * QK_WIDTH), lambda b, i: (b, 0, 0)),
                  pl.BlockSpec((1, S, hd), lambda b, i: (b, 0, 0))],
        out_specs=pl.BlockSpec((1, tq, hd), lambda b, i: (b, i, 0)),
        out_shape=bs_shape(hd),
        scratch_shapes=[pltpu.VMEM((HEADS, tq, tq), F32), pltpu.VMEM((HEADS, tq, tq), BF16),
                        pltpu.VMEM((HEADS, tq, LANES), F32), pltpu.VMEM((HEADS, tq, LANES), F32),
                        pltpu.VMEM((HEADS, tq, LANES), F32), pltpu.VMEM((HEADS, tq, HEAD_DIM), F32)],
        compiler_params=arb2,
        name="mla",
    )(q, k, v)

    o_gdn = pl.pallas_call(
        functools.partial(_gdn_kernel, ts=ts_gdn),
        grid=(B, S // ts_gdn),
        in_specs=[row_spec(hd, ts_gdn), row_spec(hd, ts_gdn), row_spec(hd, ts_gdn), row_spec(LANES, ts_gdn),
                  pl.BlockSpec((1, SUBLANES, ts_gdn), lambda b, j: (b, 0, j)), _const_spec((1, HEAD_DIM))],
        out_specs=row_spec(hd, ts_gdn),
        out_shape=bs_shape(hd),
        scratch_shapes=[pltpu.VMEM((HEADS, HEAD_DIM, HEAD_DIM), F32)],
        compiler_params=arb2,
        name="gdn",
    )(gq, gk, gv, gg, ggt, gdn_norm.reshape(1, HEAD_DIM))

    return pl.pallas_call(
        _merge_kernel,
        grid=(B, S // ts_merge),
        in_specs=[row_spec(D, ts_merge), row_spec(hd, ts_merge), row_spec(hd, ts_merge), row_spec(hd, ts_merge),
                  row_spec(2 * hd, ts_merge), _const_spec((D_MIX, D)), _const_spec((1, D))],
        out_specs=row_spec(D, ts_merge),
        out_shape=bs_shape(D, F32),
        compiler_params=arb2,
        name="merge",
    )(x, o_mla, o_gdn, om, gate, w_out.astype(BF16), out_gain.reshape(1, D))


def _tile(n, pref):
    return pref if n % pref == 0 else n


def kernel(x, mem, positions, norm_in, w_in, q_a_norm, w_q_b, kv_a_norm, w_kv_b, gdn_conv, gdn_a_log,
           gdn_dt_bias, gdn_norm, mem_norm, w_mem_kv, w_out, norm_final):
    B, S, D = x.shape
    M = mem.shape[1]
    depth = norm_in.shape[0]
    assert depth == 1, "the final norm is fused into the single layer's merge kernel"
    hd = HEADS * HEAD_DIM
    half = MLA_ROPE // 2
    inv_freq = 1.0 / (ROPE_THETA ** (jnp.arange(half, dtype=F32) / half))
    per_row = LANES // half
    invf = jnp.tile(inv_freq, per_row).reshape(1, LANES)
    pos_rows = jnp.repeat(positions.reshape(B * S // per_row, per_row), half, axis=1)
    n_rows = B * S // per_row
    tr = _tile(n_rows, 1024)
    cos, sin = pl.pallas_call(
        _rope_kernel,
        grid=(n_rows // tr,),
        in_specs=[pl.BlockSpec((tr, LANES), lambda r: (r, 0)), _const_spec((1, LANES))],
        out_specs=[pl.BlockSpec((tr, LANES), lambda r: (r, 0))] * 2,
        out_shape=[jax.ShapeDtypeStruct((n_rows, LANES), F32)] * 2,
        compiler_params=pltpu.CompilerParams(dimension_semantics=("arbitrary",), vmem_limit_bytes=VMEM_LIMIT),
        name="rope",
    )(pos_rows, invf)
    cos = cos.reshape(B, S, half)
    sin = sin.reshape(B, S, half)
    l = 0
    mk, mv = pl.pallas_call(
        _memkv_kernel,
        grid=(B,),
        in_specs=[pl.BlockSpec((1, M, D), lambda b: (b, 0, 0)), _const_spec((1, D)), _const_spec((D, 2 * hd))],
        out_specs=[pl.BlockSpec((1, M, hd), lambda b: (b, 0, 0))] * 2,
        out_shape=[jax.ShapeDtypeStruct((B, M, hd), BF16)] * 2,
        compiler_params=pltpu.CompilerParams(dimension_semantics=("arbitrary",), vmem_limit_bytes=VMEM_LIMIT),
        name="memkv",
    )(mem, mem_norm[l].reshape(1, D), w_mem_kv[l].astype(BF16))
    return _layer(x, mk, mv, cos, sin, norm_in[l], w_in[l], q_a_norm[l], w_q_b[l], kv_a_norm[l], w_kv_b[l],
                  gdn_conv[l], gdn_a_log[l], gdn_dt_bias[l], gdn_norm[l], w_out[l], norm_final,
                  ts_proj=_tile(S, 512), ts_gdn=_tile(S, 512), tq=_tile(S, 512), ts_merge=_tile(S, 512))
```

```python
import functools

import jax
import jax.numpy as jnp
import numpy as np
from jax import lax
from jax.experimental import pallas as pl
from jax.experimental.pallas import tpu as pltpu

F32 = jnp.float32
BF16 = jnp.bfloat16

D_MODEL = 1024
HEADS = 4
HEAD_DIM = 128
MLA_ROPE = 64
Q_LORA = 384
KV_LORA = 256
ROPE_THETA = 10000.0
GDN_CONV = 4
GDN_CHUNK = 64
GDN_QKV = 3 * HEADS * HEAD_DIM
D_MIX = 3 * HEADS * HEAD_DIM
IN_SPLITS = (Q_LORA, KV_LORA, MLA_ROPE, GDN_QKV, HEADS, HEADS, HEADS * HEAD_DIM, D_MIX)
EPS = 1e-6
MLA_SCALE = (HEAD_DIM + MLA_ROPE) ** -0.5
LOG2E = 1.4426950408889634
MEM_SCALE = HEAD_DIM ** -0.5
GDN_QSCALE = HEAD_DIM ** -0.5

LANES = 128
SUBLANES = 8
QK_WIDTH = 2 * LANES
MLA_STRIP = 64

C_CQ = 0
C_CKV = C_CQ + Q_LORA
C_KR = C_CKV + KV_LORA
C_AB = C_KR + LANES
C_GDN = C_AB + LANES
C_MQ = C_GDN + GDN_QKV
C_GATE = C_MQ + HEADS * HEAD_DIM
C_END = C_GATE + D_MIX

VMEM_LIMIT = 56 * 1024 * 1024


def _dot(a, b, precision=None):
    return jnp.dot(a, b, preferred_element_type=F32, precision=precision)


def _dot_nt(a, b):
    return lax.dot_general(a, b, (((1,), (1,)), ((), ())), preferred_element_type=F32)


def _dot_tn(a, b):
    return lax.dot_general(a, b, (((0,), (0,)), ((), ())), preferred_element_type=F32)


def _rms(t, gain):
    return t * lax.rsqrt(jnp.mean(t * t, axis=-1, keepdims=True) + EPS) * gain


def _sigmoid(t):
    return 1.0 / (1.0 + jnp.exp(-t))


def _silu(t):
    half = 0.5 * t
    return half + half * jnp.tanh(half)


def _memkv_kernel(mem_ref, gain_ref, w_ref, mk_out, mv_out):
    hm = _rms(mem_ref[0], gain_ref[...]).astype(BF16)
    kv = _dot(hm, w_ref[...])
    half = HEADS * HEAD_DIM
    mk_out[0] = kv[:, :half].astype(BF16)
    mv_out[0] = kv[:, half:].astype(BF16)


def _rope_kernel(pos_ref, invf_ref, cos_out, sin_out):
    ang = pos_ref[...].astype(F32) * invf_ref[...]
    cos_out[...] = jnp.cos(ang)
    sin_out[...] = jnp.sin(ang)


def _proj_kernel(x_ref, cos_ref, sin_ref, nin_ref, w_ref, qan_ref, wqb_ref, kvan_ref, wkk_ref, wkv_ref,
                 conv_ref, alog_ref, dtb_ref, mk_ref, mv_ref,
                 q_out, k_out, v_out, gq_out, gk_out, gv_out, gg_out, ggt_out, om_out, gate_out,
                 cbuf, *, ts):
    j = pl.program_id(1)
    hd = HEADS * HEAD_DIM
    hb = _rms(x_ref[0], nin_ref[...]).astype(BF16)

    def proj(lo, hi):
        return _dot(hb, w_ref[:, lo:hi])

    @pl.when(j == 0)
    def _():
        cbuf[0:SUBLANES, :] = jnp.zeros((SUBLANES, GDN_QKV), F32)

    @pl.when(j > 0)
    def _():
        cbuf[0:SUBLANES, :] = cbuf[ts:ts + SUBLANES, :]

    group = 2 * LANES
    outs = (gq_out, gk_out, gv_out)

    def gdn_dot(g):
        cbuf[SUBLANES:ts + SUBLANES, group * g:group * (g + 1)] = proj(C_GDN + group * g, C_GDN + group * (g + 1))

    def gdn_epilogue(g):
        for sub in range(group // LANES):
            gi = (group // LANES) * g + sub
            cols = slice(LANES * gi, LANES * (gi + 1))
            blk = cbuf[:, cols]
            acc = conv_ref[GDN_CONV - 1:GDN_CONV, cols] * blk[SUBLANES:]
            for back in range(1, GDN_CONV):
                tap = conv_ref[GDN_CONV - 1 - back:GDN_CONV - back, cols]
                acc = acc + tap * pltpu.roll(blk, back, 0)[SUBLANES:]
            y = _silu(acc)
            if gi < 2 * HEADS:
                y = y * lax.rsqrt(jnp.sum(y * y, axis=-1, keepdims=True) + EPS)
            if gi < HEADS:
                y = y * GDN_QSCALE
            outs[gi // HEADS][0, :, LANES * (gi % HEADS):LANES * (gi % HEADS + 1)] = y.astype(BF16)

    narrow = proj(C_CQ, C_GDN)
    cq = narrow[:, C_CQ:C_CKV]
    ckv = narrow[:, C_CKV:C_KR]
    kr_raw = narrow[:, C_KR:C_AB]
    ab = narrow[:, C_AB:C_GDN]
    cqn = _rms(cq, qan_ref[...]).astype(BF16)
    gdn_dot(0)
    ckvn = _rms(ckv, kvan_ref[...]).astype(BF16)
    qf = _dot(cqn, wqb_ref[...])

    c32, s32 = cos_ref[0], sin_ref[0]
    zpad = jnp.zeros((ts, LANES - MLA_ROPE), F32)
    cosm = jnp.concatenate([c32, c32, zpad], axis=1)
    sinm = jnp.concatenate([-s32, s32, zpad], axis=1)

    def rope(r):
        return r * cosm + pltpu.roll(r, MLA_ROPE, 1) * sinm

    gdn_dot(1)
    gdn_epilogue(0)
    kn = _dot(ckvn, wkk_ref[...])
    vv = _dot(ckvn, wkv_ref[...])
    qscale = MLA_SCALE * LOG2E
    for h in range(HEADS):
        lo = QK_WIDTH * h
        q_out[0, :, lo:lo + LANES] = (qf[:, lo:lo + LANES] * qscale).astype(BF16)
        q_out[0, :, lo + LANES:lo + QK_WIDTH] = (rope(qf[:, lo + LANES:lo + QK_WIDTH]) * qscale).astype(BF16)
    gdn_dot(2)
    gdn_epilogue(1)
    mq = proj(C_MQ, C_GATE)
    v_out[0] = vv.astype(BF16)
    kr = rope(kr_raw).astype(BF16)
    for h in range(HEADS):
        lo = QK_WIDTH * h
        k_out[0, :, lo:lo + LANES] = kn[:, LANES * h:LANES * (h + 1)].astype(BF16)
        k_out[0, :, lo + LANES:lo + QK_WIDTH] = kr
    gdn_dot(3)
    gdn_epilogue(2)

    mscale = MEM_SCALE * LOG2E
    sc, pp, ll, oo = {}, {}, {}, {}

    def mem_qk(h):
        cols = slice(HEAD_DIM * h, HEAD_DIM * (h + 1))
        sc[h] = _dot_nt((mq[:, cols] * mscale).astype(BF16), mk_ref[0, :, cols])

    def mem_softmax(h):
        p = jnp.exp2(sc[h] - jnp.max(sc[h], axis=-1, keepdims=True))
        ll[h] = jnp.sum(p, axis=-1, keepdims=True)
        pp[h] = p.astype(BF16)

    def mem_pv(h):
        oo[h] = _dot(pp[h], mv_ref[0, :, HEAD_DIM * h:HEAD_DIM * (h + 1)])

    mem_qk(0)
    z = ab + dtb_ref[...]
    softplus = jnp.maximum(z, 0.0) + jnp.log1p(jnp.exp(-jnp.abs(z)))
    gcum = -jnp.exp(alog_ref[...]) * softplus
    lane = lax.broadcasted_iota(jnp.int32, ab.shape, 1)
    pos_in_chunk = lax.broadcasted_iota(jnp.int32, ab.shape, 0) % GDN_CHUNK
    shift = 1
    while shift < GDN_CHUNK:
        gcum = gcum + jnp.where(pos_in_chunk >= shift, pltpu.roll(gcum, shift, 0), 0.0)
        shift *= 2
    gg_out[0] = jnp.where(lane < HEADS, gcum, _sigmoid(ab))
    ggt_out[0] = gcum.T[0:SUBLANES, :]

    gdn_dot(4)
    gdn_epilogue(3)
    mem_qk(1)
    mem_softmax(0)
    gdn_dot(5)
    gdn_epilogue(4)
    mem_qk(2)
    mem_pv(0)
    mem_softmax(1)
    gate_out[0, :, 0:hd] = proj(C_GATE, C_GATE + hd).astype(BF16)
    gdn_epilogue(5)
    mem_qk(3)
    mem_pv(1)
    mem_softmax(2)
    gate_out[0, :, hd:2 * hd] = proj(C_GATE + hd, C_GATE + 2 * hd).astype(BF16)
    mem_pv(2)
    mem_softmax(3)
    gmem = _silu(proj(C_GATE + 2 * hd, C_END))
    mem_pv(3)
    for h in range(HEADS):
        cols = slice(HEAD_DIM * h, HEAD_DIM * (h + 1))
        om_out[0, :, cols] = (oo[h] / ll[h] * gmem[:, cols]).astype(BF16)


def _mla_kernel(q_ref, k_ref, v_ref, o_ref, s_scr, p_scr, m_scr, l_scr, a_scr, acc_scr, *, tq):
    i = pl.program_id(1)
    heads = range(HEADS)
    strip = MLA_STRIP

    m_scr[...] = jnp.full(m_scr.shape, -jnp.inf, F32)
    l_scr[...] = jnp.zeros(l_scr.shape, F32)
    acc_scr[...] = jnp.zeros(acc_scr.shape, F32)

    def step(jk, masked):
        start = pl.multiple_of(jk * tq, tq)

        def scores(h):
            s_scr[h] = _dot_nt(q_ref[0, :, QK_WIDTH * h:QK_WIDTH * (h + 1)],
                               k_ref[0, pl.ds(start, tq), QK_WIDTH * h:QK_WIDTH * (h + 1)])

        def load_strip(h, r):
            s = s_scr[h, strip * r:strip * (r + 1), :]
            if masked:
                row = lax.broadcasted_iota(jnp.int32, s.shape, 0) + strip * r
                col = lax.broadcasted_iota(jnp.int32, s.shape, 1)
                s = jnp.where(row >= col, s, -jnp.inf)
            return s

        def lane_tiles(t):
            return [t[:, LANES * g:LANES * (g + 1)] for g in range(t.shape[1] // LANES)]

        def softmax(h):
            strips = range(tq // strip)
            rows = [slice(strip * r, strip * (r + 1)) for r in strips]
            part = [functools.reduce(jnp.maximum, lane_tiles(load_strip(h, r))) for r in strips]
            peak = [jnp.broadcast_to(jnp.max(t, axis=-1, keepdims=True), (strip, LANES)) for t in part]
            m_old = [m_scr[h, rows[r], :] for r in strips]
            m_new = [jnp.maximum(m_old[r], peak[r]) for r in strips]
            alpha = [jnp.exp2(m_old[r] - m_new[r]) for r in strips]
            for r in strips:
                m_scr[h, rows[r], :] = m_new[r]
                a_scr[h, rows[r], :] = alpha[r]
            part = []
            for r in strips:
                p = [jnp.exp2(t - m_new[r]) for t in lane_tiles(load_strip(h, r))]
                for g, t in enumerate(p):
                    p_scr[h, rows[r], LANES * g:LANES * (g + 1)] = t.astype(BF16)
                part.append(functools.reduce(jnp.add, p))
            total = [jnp.broadcast_to(jnp.sum(t, axis=-1, keepdims=True), (strip, LANES)) for t in part]
            for r in strips:
                l_scr[h, rows[r], :] = alpha[r] * l_scr[h, rows[r], :] + total[r]

        def values(h):
            pv = _dot(p_scr[h], v_ref[0, pl.ds(start, tq), HEAD_DIM * h:HEAD_DIM * (h + 1)])
            acc_scr[h] = a_scr[h] * acc_scr[h] + pv

        for t in range(HEADS + 2):
            if t < HEADS:
                scores(t)
            if 0 <= t - 1 < HEADS:
                softmax(t - 1)
            if 0 <= t - 2 < HEADS:
                values(t - 2)

    def body(jk, carry):
        step(jk, False)
        return carry

    lax.fori_loop(0, i, body, 0)
    step(i, True)
    for h in heads:
        o_ref[0, :, HEAD_DIM * h:HEAD_DIM * (h + 1)] = (acc_scr[h] / l_scr[h]).astype(BF16)


def _pair_blockdiag(t, lo_half):
    return jnp.concatenate([jnp.where(lo_half, t, 0.0), jnp.where(lo_half, 0.0, t)], axis=0).astype(BF16)


def _gdn_merge_kernel(q_ref, k_ref, v_ref, gg_ref, ggt_ref, gn_ref, x_ref, omla_ref, om_ref, gate_ref, wout_ref,
                      nf_ref, out_ref, s_ref, og_scr, *, ts):
    j = pl.program_id(1)
    hd = HEADS * HEAD_DIM
    ntile = 2 * LANES

    @pl.when(j == 0)
    def _():
        s_ref[...] = jnp.zeros(s_ref.shape, F32)

    c = GDN_CHUNK
    c2 = 2 * c
    npairs = ts // c2
    row = lax.broadcasted_iota(jnp.int32, (c, c2), 0)
    lane = lax.broadcasted_iota(jnp.int32, (c, c2), 1)
    lo_half = lane < c
    col = jnp.where(lo_half, lane, lane - c)
    incl = row >= col
    strict = row > col
    eye = jnp.where(row == col, 1.0, 0.0)
    zeros_k = jnp.zeros((c, HEAD_DIM), BF16)
    units = [(p2, h) for p2 in range(npairs) for h in range(HEADS)]

    pre = {}
    for (p2, h) in units:
        r2 = slice(c2 * p2, c2 * (p2 + 1))
        cols = slice(HEAD_DIM * h, HEAD_DIM * (h + 1))
        gg = gg_ref[0, r2, :]
        gcol = gg[:, h:h + 1]
        beta = gg[:, HEADS + h:HEADS + h + 1]
        grow = ggt_ref[0, h:h + 1, r2]
        glast = (gg[c - 1:c, h:h + 1], gg[c2 - 1:c2, h:h + 1])
        glast_col = jnp.concatenate([jnp.broadcast_to(glast[0], (c, 1)), jnp.broadcast_to(glast[1], (c, 1))], axis=0)
        q2 = q_ref[0, r2, cols]
        k2 = k_ref[0, r2, cols]
        kf = k2.astype(F32)
        kbeta = kf * beta
        kbeta_b = kbeta.astype(BF16)
        vbeta_b = (v_ref[0, r2, cols].astype(F32) * beta).astype(BF16)
        eg = jnp.exp(gcol)
        lhs = jnp.concatenate([jnp.concatenate([kbeta_b[:c], kbeta_b[c:]], axis=1),
                               jnp.concatenate([q2[:c], q2[c:]], axis=1)], axis=0)
        rhs = jnp.concatenate([jnp.concatenate([k2[:c], zeros_k], axis=1),
                               jnp.concatenate([zeros_k, k2[c:]], axis=1)], axis=0)
        gcol_pair = jnp.where(lo_half, gcol[:c], gcol[c:])
        pre[(p2, h)] = dict(
            lhs=lhs, rhs=rhs, glast=glast,
            vk=jnp.concatenate([vbeta_b, (kbeta * eg).astype(BF16)], axis=1),
            qg=(q2.astype(F32) * eg).astype(BF16),
            kdec=(kf * jnp.exp(glast_col - gcol)).astype(BF16),
            decay=jnp.exp(jnp.where(incl, gcol_pair - grow, -jnp.inf)))

    kq = {u: _dot_nt(pre[u]["lhs"], pre[u]["rhs"]) for u in units}
    a_pair = {u: kq[u][c:] * pre[u]["decay"] for u in units}
    m = {u: -jnp.where(strict, kq[u][:c] * pre[u]["decay"], 0.0) for u in units}

    p = {u: eye + m[u] for u in units}
    m = {u: _dot(m[u].astype(BF16), _pair_blockdiag(m[u], lo_half)) for u in units}
    for _ in range(int(np.log2(c)) - 2):
        pm = {u: _dot(jnp.concatenate([p[u], m[u]], axis=0).astype(BF16), _pair_blockdiag(m[u], lo_half))
              for u in units}
        p = {u: p[u] + pm[u][:c] for u in units}
        m = {u: pm[u][c:] for u in units}
    pm = {u: _dot(p[u].astype(BF16), _pair_blockdiag(m[u], lo_half)) for u in units}
    t_pair = {u: p[u] + pm[u] for u in units}

    uw = {u: _dot(_pair_blockdiag(t_pair[u], lo_half), pre[u]["vk"]) for u in units}
    a_chunks = {u: (a_pair[u][:, :c].astype(BF16), pltpu.roll(a_pair[u], c, 1)[:, :c].astype(BF16)) for u in units}

    a_mla = (omla_ref[0].astype(F32) * _silu(gate_ref[0, :, 0:hd].astype(F32))).astype(BF16)
    free_parts = [(a_mla, 0), (om_ref[0], 2 * hd)]
    pieces = [(part, n) for part in range(len(free_parts)) for n in range(D_MODEL // ntile)]
    partial = {}

    heads = range(HEADS)
    state = [s_ref[h] for h in heads]
    nsteps = 2 * npairs
    for step in range(nsteps):
        p2, ci = divmod(step, 2)
        rc = slice(c * ci, c * (ci + 1))
        rows = slice(c * step, c * (step + 1))
        ws_qs = [_dot(jnp.concatenate([uw[(p2, h)][rc, HEAD_DIM:].astype(BF16), pre[(p2, h)]["qg"][rc]], axis=0),
                      state[h].astype(BF16)) for h in heads]
        for part, n in pieces[len(pieces) * step // nsteps:len(pieces) * (step + 1) // nsteps]:
            operand, w_lo = free_parts[part]
            partial[(part, n)] = _dot(operand, wout_ref[w_lo:w_lo + hd, ntile * n:ntile * (n + 1)])
        v_new = [(uw[(p2, h)][rc, :HEAD_DIM] - ws_qs[h][:c]).astype(BF16) for h in heads]
        o_intra = [_dot(a_chunks[(p2, h)][ci], v_new[h]) for h in heads]
        ds = [_dot_tn(pre[(p2, h)]["kdec"][rc], v_new[h]) for h in heads]
        state = [state[h] * jnp.exp(pre[(p2, h)]["glast"][ci]) + ds[h] for h in heads]
        for h in heads:
            cols = slice(HEAD_DIM * h, HEAD_DIM * (h + 1))
            gate = _silu(gate_ref[0, rows, hd + HEAD_DIM * h:hd + HEAD_DIM * (h + 1)].astype(F32))
            og_scr[rows, cols] = (_rms(ws_qs[h][c:] + o_intra[h], gn_ref[...]) * gate).astype(BF16)
    for h in heads:
        s_ref[h] = state[h]

    og = og_scr[...]
    ys = []
    for n in range(D_MODEL // ntile):
        ncols = slice(ntile * n, ntile * (n + 1))
        acc = _dot(og, wout_ref[hd:2 * hd, ncols])
        for part in range(len(free_parts)):
            acc = acc + partial[(part, n)]
        ys.append(x_ref[0, :, ncols] + acc)
    ssq = functools.reduce(jnp.add, [jnp.sum(y * y, axis=-1, keepdims=True) for y in ys])
    inv = lax.rsqrt(ssq * (1.0 / D_MODEL) + EPS)
    for n, y in enumerate(ys):
        ncols = slice(ntile * n, ntile * (n + 1))
        out_ref[0, :, ncols] = y * inv * nf_ref[:, ncols]


def _const_spec(shape):
    nd = len(shape)
    return pl.BlockSpec(shape, lambda *_: (0,) * nd)


def _pack_weights(w_in, w_q_b, w_kv_b):
    o = np.cumsum((0,) + IN_SPLITS)
    w_cq, w_ckv, w_kr, w_gdn, w_a, w_b, w_mq, w_gate = (w_in[:, o[i]:o[i + 1]] for i in range(8))
    half = MLA_ROPE // 2
    w_kr2 = jnp.concatenate([w_kr, w_kr[:, half:], w_kr[:, :half]], axis=1)
    w_ab = jnp.concatenate([w_a, w_b, jnp.zeros((D_MODEL, LANES - 2 * HEADS), w_in.dtype)], axis=1)
    w_all = jnp.concatenate([w_cq, w_ckv, w_kr2, w_ab, w_gdn, w_mq, w_gate], axis=1).astype(BF16)
    wq = w_q_b.reshape(Q_LORA, HEADS, HEAD_DIM + MLA_ROPE)
    nope, ropec = wq[..., :HEAD_DIM], wq[..., HEAD_DIM:]
    wqb = jnp.concatenate([nope, ropec, ropec[..., half:], ropec[..., :half]], axis=-1)
    wqb = wqb.reshape(Q_LORA, HEADS * QK_WIDTH).astype(BF16)
    wkv = w_kv_b.reshape(KV_LORA, HEADS, 2 * HEAD_DIM)
    wkk = wkv[..., :HEAD_DIM].reshape(KV_LORA, HEADS * HEAD_DIM).astype(BF16)
    wkvv = wkv[..., HEAD_DIM:].reshape(KV_LORA, HEADS * HEAD_DIM).astype(BF16)
    return w_all, wqb, wkk, wkvv


def _lane_row(vec):
    return jnp.zeros((1, LANES), F32).at[0, :vec.shape[0]].set(vec.astype(F32))


def _layer(x, mk, mv, cos, sin, norm_in, w_in, q_a_norm, w_q_b, kv_a_norm, w_kv_b, gdn_conv,
           gdn_a_log, gdn_dt_bias, gdn_norm, w_out, out_gain, *, ts_proj, ts_gdn, tq):
    B, S, D = x.shape
    hd = HEADS * HEAD_DIM
    w_all, wqb, wkk, wkvv = _pack_weights(w_in, w_q_b, w_kv_b)
    M = mk.shape[1]
    arb2 = pltpu.CompilerParams(dimension_semantics=("arbitrary", "arbitrary"), vmem_limit_bytes=VMEM_LIMIT)

    def row_spec(width, ts):
        return pl.BlockSpec((1, ts, width), lambda b, j: (b, j, 0))

    def bs_shape(width, dtype=BF16):
        return jax.ShapeDtypeStruct((B, S, width), dtype)

    q, k, v, gq, gk, gv, gg, ggt, om, gate = pl.pallas_call(
        functools.partial(_proj_kernel, ts=ts_proj),
        grid=(B, S // ts_proj),
        in_specs=[row_spec(D, ts_proj), row_spec(MLA_ROPE // 2, ts_proj), row_spec(MLA_ROPE // 2, ts_proj),
                  _const_spec((1, D)), _const_spec(w_all.shape),
                  _const_spec((1, Q_LORA)), _const_spec(wqb.shape), _const_spec((1, KV_LORA)),
                  _const_spec(wkk.shape), _const_spec(wkvv.shape), _const_spec((GDN_CONV, GDN_QKV)),
                  _const_spec((1, LANES)), _const_spec((1, LANES)),
                  pl.BlockSpec((1, M, hd), lambda b, j: (b, 0, 0)), pl.BlockSpec((1, M, hd), lambda b, j: (b, 0, 0))],
        out_specs=[row_spec(HEADS * QK_WIDTH, ts_proj), row_spec(HEADS * QK_WIDTH, ts_proj), row_spec(hd, ts_proj),
                   row_spec(hd, ts_proj), row_spec(hd, ts_proj), row_spec(hd, ts_proj), row_spec(LANES, ts_proj),
                   pl.BlockSpec((1, SUBLANES, ts_proj), lambda b, j: (b, 0, j)),
                   row_spec(hd, ts_proj), row_spec(2 * hd, ts_proj)],
        out_shape=[bs_shape(HEADS * QK_WIDTH), bs_shape(HEADS * QK_WIDTH), bs_shape(hd), bs_shape(hd), bs_shape(hd),
                   bs_shape(hd), bs_shape(LANES, F32), jax.ShapeDtypeStruct((B, SUBLANES, S), F32),
                   bs_shape(hd), bs_shape(2 * hd)],
        scratch_shapes=[pltpu.VMEM((ts_proj + SUBLANES, GDN_QKV), F32)],
        compiler_params=arb2,
        name="proj",
    )(x, cos, sin, norm_in.reshape(1, D), w_all, q_a_norm.reshape(1, Q_LORA), wqb, kv_a_norm.reshape(1, KV_LORA),
      wkk, wkvv, gdn_conv, _lane_row(gdn_a_log), _lane_row(gdn_dt_bias), mk, mv)

    o_mla = pl.pallas_call(
        functools.partial(_mla_kernel, tq=tq),
        grid=(B, S // tq),
        in_specs=[pl.BlockSpec((1, tq, HEADS * QK_WIDTH), lambda b, i: (b, i, 0)),
                  pl.BlockSpec((1, S, HEADS * QK_WIDTH), lambda b, i: (b, 0, 0)),
                  pl.BlockSpec((1, S, hd), lambda b, i: (b, 0, 0))],
        out_specs=pl.BlockSpec((1, tq, hd), lambda b, i: (b, i, 0)),
        out_shape=bs_shape(hd),
        scratch_shapes=[pltpu.VMEM((HEADS, tq, tq), F32), pltpu.VMEM((HEADS, tq, tq), BF16),
                        pltpu.VMEM((HEADS, tq, LANES), F32), pltpu.VMEM((HEADS, tq, LANES), F32),
                        pltpu.VMEM((HEADS, tq, LANES), F32), pltpu.VMEM((HEADS, tq, HEAD_DIM), F32)],
        compiler_params=arb2,
        name="mla",
    )(q, k, v)

    return pl.pallas_call(
        functools.partial(_gdn_merge_kernel, ts=ts_gdn),
        grid=(B, S // ts_gdn),
        in_specs=[row_spec(hd, ts_gdn), row_spec(hd, ts_gdn), row_spec(hd, ts_gdn), row_spec(LANES, ts_gdn),
                  pl.BlockSpec((1, SUBLANES, ts_gdn), lambda b, j: (b, 0, j)), _const_spec((1, HEAD_DIM)),
                  row_spec(D, ts_gdn), row_spec(hd, ts_gdn), row_spec(hd, ts_gdn), row_spec(2 * hd, ts_gdn),
                  _const_spec((D_MIX, D)), _const_spec((1, D))],
        out_specs=row_spec(D, ts_gdn),
        out_shape=bs_shape(D, F32),
        scratch_shapes=[pltpu.VMEM((HEADS, HEAD_DIM, HEAD_DIM), F32), pltpu.VMEM((ts_gdn, hd), BF16)],
        compiler_params=arb2,
        name="gdn_merge",
    )(gq, gk, gv, gg, ggt, gdn_norm.reshape(1, HEAD_DIM), x, o_mla, om, gate, w_out.astype(BF16),
      out_gain.reshape(1, D))


def _tile(n, pref):
    return pref if n % pref == 0 else n


def kernel(x, mem, positions, norm_in, w_in, q_a_norm, w_q_b, kv_a_norm, w_kv_b, gdn_conv, gdn_a_log,
           gdn_dt_bias, gdn_norm, mem_norm, w_mem_kv, w_out, norm_final):
    B, S, D = x.shape
    M = mem.shape[1]
    depth = norm_in.shape[0]
    assert depth == 1, "the final norm is fused into the single layer's last kernel"
    hd = HEADS * HEAD_DIM
    half = MLA_ROPE // 2
    inv_freq = 1.0 / (ROPE_THETA ** (jnp.arange(half, dtype=F32) / half))
    per_row = LANES // half
    invf = jnp.tile(inv_freq, per_row).reshape(1, LANES)
    pos_rows = jnp.repeat(positions.reshape(B * S // per_row, per_row), half, axis=1)
    n_rows = B * S // per_row
    tr = _tile(n_rows, 1024)
    cos, sin = pl.pallas_call(
        _rope_kernel,
        grid=(n_rows // tr,),
        in_specs=[pl.BlockSpec((tr, LANES), lambda r: (r, 0)), _const_spec((1, LANES))],
        out_specs=[pl.BlockSpec((tr, LANES), lambda r: (r, 0))] * 2,
        out_shape=[jax.ShapeDtypeStruct((n_rows, LANES), F32)] * 2,
        compiler_params=pltpu.CompilerParams(dimension_semantics=("arbitrary",), vmem_limit_bytes=VMEM_LIMIT),
        name="rope",
    )(pos_rows, invf)
    cos = cos.reshape(B, S, half)
    sin = sin.reshape(B, S, half)
    l = 0
    mk, mv = pl.pallas_call(
        _memkv_kernel,
        grid=(B,),
        in_specs=[pl.BlockSpec((1, M, D), lambda b: (b, 0, 0)), _const_spec((1, D)), _const_spec((D, 2 * hd))],
        out_specs=[pl.BlockSpec((1, M, hd), lambda b: (b, 0, 0))] * 2,
        out_shape=[jax.ShapeDtypeStruct((B, M, hd), BF16)] * 2,
        compiler_params=pltpu.CompilerParams(dimension_semantics=("arbitrary",), vmem_limit_bytes=VMEM_LIMIT),
        name="memkv",
    )(mem, mem_norm[l].reshape(1, D), w_mem_kv[l].astype(BF16))
    return _layer(x, mk, mv, cos, sin, norm_in[l], w_in[l], q_a_norm[l], w_q_b[l], kv_a_norm[l], w_kv_b[l],
                  gdn_conv[l], gdn_a_log[l], gdn_dt_bias[l], gdn_norm[l], w_out[l], norm_final,
                  ts_proj=_tile(S, 512), ts_gdn=_tile(S, 512), tq=_tile(S, 512))
```

```python
import functools

import jax
import jax.numpy as jnp
import numpy as np
from jax import lax
from jax.experimental import pallas as pl
from jax.experimental.pallas import tpu as pltpu

F32 = jnp.float32
BF16 = jnp.bfloat16

D_MODEL = 1024
HEADS = 4
HEAD_DIM = 128
MLA_ROPE = 64
Q_LORA = 384
KV_LORA = 256
ROPE_THETA = 10000.0
GDN_CONV = 4
GDN_CHUNK = 64
GDN_QKV = 3 * HEADS * HEAD_DIM
D_MIX = 3 * HEADS * HEAD_DIM
IN_SPLITS = (Q_LORA, KV_LORA, MLA_ROPE, GDN_QKV, HEADS, HEADS, HEADS * HEAD_DIM, D_MIX)
EPS = 1e-6
MLA_SCALE = (HEAD_DIM + MLA_ROPE) ** -0.5
LOG2E = 1.4426950408889634
MEM_SCALE = HEAD_DIM ** -0.5
GDN_QSCALE = HEAD_DIM ** -0.5

LANES = 128
SUBLANES = 8
QK_WIDTH = 2 * LANES
MLA_STRIP = 64

C_CQ = 0
C_CKV = C_CQ + Q_LORA
C_KR = C_CKV + KV_LORA
C_AB = C_KR + LANES
C_MQ = C_AB + LANES
C_GATE = C_MQ + HEADS * HEAD_DIM
C_END = C_GATE + D_MIX

GDN_GROUP = 2 * LANES
GDN_GROUPS = GDN_QKV // GDN_GROUP

VMEM_LIMIT = 56 * 1024 * 1024


def _dot(a, b, precision=None):
    return jnp.dot(a, b, preferred_element_type=F32, precision=precision)


def _dot_nt(a, b):
    return lax.dot_general(a, b, (((1,), (1,)), ((), ())), preferred_element_type=F32)


def _dot_tn(a, b):
    return lax.dot_general(a, b, (((0,), (0,)), ((), ())), preferred_element_type=F32)


def _rms(t, gain):
    return t * lax.rsqrt(jnp.mean(t * t, axis=-1, keepdims=True) + EPS) * gain


def _sigmoid(t):
    return 1.0 / (1.0 + jnp.exp(-t))


def _silu(t):
    half = 0.5 * t
    return half + half * jnp.tanh(half)


def _memkv_kernel(mem_ref, gain_ref, w_ref, mk_out, mv_out):
    nb, m, d = mem_ref.shape
    hm = _rms(mem_ref[...].reshape(nb * m, d), gain_ref[...]).astype(BF16)
    kv = _dot(hm, w_ref[...])
    half = HEADS * HEAD_DIM
    mk_out[...] = kv[:, :half].astype(BF16).reshape(nb, m, half)
    mv_out[...] = kv[:, half:].astype(BF16).reshape(nb, m, half)


def _rope_kernel(pos_ref, invf_ref, cos_out, sin_out):
    ang = pos_ref[...].astype(F32) * invf_ref[...]
    cos_out[...] = jnp.cos(ang)
    sin_out[...] = jnp.sin(ang)


def _proj_kernel(x_ref, cos_ref, sin_ref, nin_ref, w_ref, wg_ref, qan_ref, wqb_ref, kvan_ref, wkk_ref, wkv_ref,
                 conv_ref, alog_ref, dtb_ref, mk_ref, mv_ref,
                 q_out, k_out, v_out, gqkv_out, gg_out, ggt_out, om_out, gate_out,
                 cbuf, *, ts):
    j = pl.program_id(1)
    hd = HEADS * HEAD_DIM
    hb = _rms(x_ref[0], nin_ref[...]).astype(BF16)

    def proj(lo, hi):
        return _dot(hb, w_ref[:, lo:hi])

    @pl.when(j == 0)
    def _():
        cbuf[:, 0:SUBLANES, :] = jnp.zeros((GDN_GROUPS, SUBLANES, GDN_GROUP), F32)

    @pl.when(j > 0)
    def _():
        cbuf[:, 0:SUBLANES, :] = cbuf[:, ts:ts + SUBLANES, :]

    def gdn_dot(g):
        cbuf[g, SUBLANES:ts + SUBLANES, :] = _dot(hb, wg_ref[g])

    def gdn_epilogue(g):
        taps = conv_ref[g]
        for sub in range(GDN_GROUP // LANES):
            cols = slice(LANES * sub, LANES * (sub + 1))
            blk = cbuf[g, :, cols]
            acc = taps[GDN_CONV - 1:GDN_CONV, cols] * blk[SUBLANES:]
            for back in range(1, GDN_CONV):
                acc = acc + taps[GDN_CONV - 1 - back:GDN_CONV - back, cols] * pltpu.roll(blk, back, 0)[SUBLANES:]
            y = _silu(acc)
            if g < 2 * (GDN_GROUPS // 3):
                inv = lax.rsqrt(jnp.sum(y * y, axis=-1, keepdims=True) + EPS)
                y = y * (inv * GDN_QSCALE if g < GDN_GROUPS // 3 else inv)
            gqkv_out[0, g, :, cols] = y.astype(BF16)

    narrow = proj(C_CQ, C_MQ)
    cq = narrow[:, C_CQ:C_CKV]
    ckv = narrow[:, C_CKV:C_KR]
    kr_raw = narrow[:, C_KR:C_AB]
    ab = narrow[:, C_AB:C_MQ]
    cqn = _rms(cq, qan_ref[...]).astype(BF16)
    gdn_dot(0)
    ckvn = _rms(ckv, kvan_ref[...]).astype(BF16)
    qf = _dot(cqn, wqb_ref[...])

    c32, s32 = cos_ref[0], sin_ref[0]
    zpad = jnp.zeros((ts, LANES - MLA_ROPE), F32)
    cosm = jnp.concatenate([c32, c32, zpad], axis=1)
    sinm = jnp.concatenate([-s32, s32, zpad], axis=1)

    def rope(r):
        return r * cosm + pltpu.roll(r, MLA_ROPE, 1) * sinm

    gdn_dot(1)
    gdn_epilogue(0)
    kn = _dot(ckvn, wkk_ref[...])
    vv = _dot(ckvn, wkv_ref[...])
    qscale = MLA_SCALE * LOG2E
    for h in range(HEADS):
        lo = QK_WIDTH * h
        q_out[0, :, lo:lo + LANES] = (qf[:, lo:lo + LANES] * qscale).astype(BF16)
        q_out[0, :, lo + LANES:lo + QK_WIDTH] = (rope(qf[:, lo + LANES:lo + QK_WIDTH]) * qscale).astype(BF16)
    gdn_dot(2)
    gdn_epilogue(1)
    mq = proj(C_MQ, C_GATE)
    v_out[0] = vv.astype(BF16)
    kr = rope(kr_raw).astype(BF16)
    for h in range(HEADS):
        lo = QK_WIDTH * h
        k_out[0, :, lo:lo + LANES] = kn[:, LANES * h:LANES * (h + 1)].astype(BF16)
        k_out[0, :, lo + LANES:lo + QK_WIDTH] = kr
    gdn_dot(3)
    gdn_epilogue(2)

    mscale = MEM_SCALE * LOG2E
    sc, pp, ll, oo = {}, {}, {}, {}

    def mem_qk(h):
        cols = slice(HEAD_DIM * h, HEAD_DIM * (h + 1))
        sc[h] = _dot_nt((mq[:, cols] * mscale).astype(BF16), mk_ref[0, :, cols])

    def mem_softmax(h):
        p = jnp.exp2(sc[h] - jnp.max(sc[h], axis=-1, keepdims=True))
        ll[h] = jnp.sum(p, axis=-1, keepdims=True)
        pp[h] = p.astype(BF16)

    def mem_pv(h):
        oo[h] = _dot(pp[h], mv_ref[0, :, HEAD_DIM * h:HEAD_DIM * (h + 1)])

    mem_qk(0)
    z = ab + dtb_ref[...]
    softplus = jnp.maximum(z, 0.0) + jnp.log1p(jnp.exp(-jnp.abs(z)))
    gcum = -jnp.exp(alog_ref[...]) * softplus
    lane = lax.broadcasted_iota(jnp.int32, ab.shape, 1)
    pos_in_chunk = lax.broadcasted_iota(jnp.int32, ab.shape, 0) % GDN_CHUNK
    shift = 1
    while shift < GDN_CHUNK:
        gcum = gcum + jnp.where(pos_in_chunk >= shift, pltpu.roll(gcum, shift, 0), 0.0)
        shift *= 2
    gg_out[0] = jnp.where(lane < HEADS, gcum, _sigmoid(ab))
    ggt_out[0] = gcum.T[0:SUBLANES, :]

    gdn_dot(4)
    gdn_epilogue(3)
    mem_qk(1)
    mem_softmax(0)
    gdn_dot(5)
    gdn_epilogue(4)
    mem_qk(2)
    mem_pv(0)
    mem_softmax(1)
    gate_out[0, :, 0:hd] = proj(C_GATE, C_GATE + hd).astype(BF16)
    gdn_epilogue(5)
    mem_qk(3)
    mem_pv(1)
    mem_softmax(2)
    gate_out[0, :, hd:2 * hd] = proj(C_GATE + hd, C_GATE + 2 * hd).astype(BF16)
    mem_pv(2)
    mem_softmax(3)
    gmem = _silu(proj(C_GATE + 2 * hd, C_END))
    mem_pv(3)
    for h in range(HEADS):
        cols = slice(HEAD_DIM * h, HEAD_DIM * (h + 1))
        om_out[0, :, cols] = (oo[h] / ll[h] * gmem[:, cols]).astype(BF16)


def _mla_kernel(q_ref, k_ref, v_ref, o_ref, s_scr, p_scr, m_scr, l_scr, a_scr, acc_scr, *, tq):
    i = pl.program_id(1)
    heads = range(HEADS)
    strip = MLA_STRIP

    def step(jk, diagonal):
        start = pl.multiple_of(jk * tq, tq)
        if diagonal:
            half = tq // 2
            items = [(h, r0, half, r0 + half) for h in heads for r0 in (0, half)]
        else:
            items = [(h, 0, tq, tq) for h in heads]

        def scores(item):
            h, r0, nr, nk = item
            s_scr[h, r0:r0 + nr, 0:nk] = _dot_nt(q_ref[0, r0:r0 + nr, QK_WIDTH * h:QK_WIDTH * (h + 1)],
                                                 k_ref[0, pl.ds(start, nk), QK_WIDTH * h:QK_WIDTH * (h + 1)])

        def load_strip(item, r):
            h, r0, nr, nk = item
            s = s_scr[h, r0 + strip * r:r0 + strip * (r + 1), 0:nk]
            if diagonal:
                row = lax.broadcasted_iota(jnp.int32, s.shape, 0) + (r0 + strip * r)
                col = lax.broadcasted_iota(jnp.int32, s.shape, 1)
                s = jnp.where(row >= col, s, -jnp.inf)
            return s

        def lane_tiles(t):
            return [t[:, LANES * g:LANES * (g + 1)] for g in range(t.shape[1] // LANES)]

        def softmax(item):
            h, r0, nr, nk = item
            strips = range(nr // strip)
            rows = [slice(r0 + strip * r, r0 + strip * (r + 1)) for r in strips]
            part = [functools.reduce(jnp.maximum, lane_tiles(load_strip(item, r))) for r in strips]
            peak = [jnp.broadcast_to(jnp.max(t, axis=-1, keepdims=True), (strip, LANES)) for t in part]
            if diagonal:
                m_new = peak
            else:
                m_old = [m_scr[h, rows[r], :] for r in strips]
                m_new = [jnp.maximum(m_old[r], peak[r]) for r in strips]
                alpha = [jnp.exp2(m_old[r] - m_new[r]) for r in strips]
                for r in strips:
                    a_scr[h, rows[r], :] = alpha[r]
            for r in strips:
                m_scr[h, rows[r], :] = m_new[r]
            part = []
            for r in strips:
                p = [jnp.exp2(t - m_new[r]) for t in lane_tiles(load_strip(item, r))]
                for g, t in enumerate(p):
                    p_scr[h, rows[r], LANES * g:LANES * (g + 1)] = t.astype(BF16)
                part.append(functools.reduce(jnp.add, p))
            total = [jnp.broadcast_to(jnp.sum(t, axis=-1, keepdims=True), (strip, LANES)) for t in part]
            for r in strips:
                l_scr[h, rows[r], :] = total[r] if diagonal else alpha[r] * l_scr[h, rows[r], :] + total[r]

        def values(item):
            h, r0, nr, nk = item
            pv = _dot(p_scr[h, r0:r0 + nr, 0:nk], v_ref[0, pl.ds(start, nk), HEAD_DIM * h:HEAD_DIM * (h + 1)])
            if diagonal:
                acc_scr[h, r0:r0 + nr, :] = pv
            else:
                acc_scr[h, r0:r0 + nr, :] = a_scr[h, r0:r0 + nr, :] * acc_scr[h, r0:r0 + nr, :] + pv

        for t in range(len(items) + 2):
            if t < len(items):
                scores(items[t])
            if 0 <= t - 1 < len(items):
                softmax(items[t - 1])
            if 0 <= t - 2 < len(items):
                values(items[t - 2])

    def body(jk, carry):
        step(jk, False)
        return carry

    step(i, True)
    lax.fori_loop(0, i, body, 0)
    for h in heads:
        o_ref[0, :, HEAD_DIM * h:HEAD_DIM * (h + 1)] = (acc_scr[h] / l_scr[h]).astype(BF16)


def _pair_blockdiag(t, lo_half):
    return jnp.concatenate([jnp.where(lo_half, t, 0.0), jnp.where(lo_half, 0.0, t)], axis=0).astype(BF16)


def _gdn_merge_kernel(qkv_ref, gg_ref, ggt_ref, gn_ref, x_ref, omla_ref, om_ref, gate_ref, wout_ref,
                      nf_ref, out_ref, s_ref, og_scr, *, ts):
    j = pl.program_id(1)
    hd = HEADS * HEAD_DIM
    ntile = 2 * LANES

    @pl.when(j == 0)
    def _():
        s_ref[...] = jnp.zeros(s_ref.shape, F32)

    c = GDN_CHUNK
    c2 = 2 * c
    npairs = ts // c2
    row = lax.broadcasted_iota(jnp.int32, (c, c2), 0)
    lane = lax.broadcasted_iota(jnp.int32, (c, c2), 1)
    lo_half = lane < c
    col = jnp.where(lo_half, lane, lane - c)
    incl = row >= col
    strict = row > col
    eye = jnp.where(row == col, 1.0, 0.0)
    zeros_k = jnp.zeros((c, HEAD_DIM), BF16)
    units = [(p2, h) for p2 in range(npairs) for h in range(HEADS)]

    pre = {}
    for (p2, h) in units:
        r2 = slice(c2 * p2, c2 * (p2 + 1))
        cols = slice(HEAD_DIM * h, HEAD_DIM * (h + 1))
        gg = gg_ref[0, r2, :]
        gcol = gg[:, h:h + 1]
        beta = gg[:, HEADS + h:HEADS + h + 1]
        grow = ggt_ref[0, h:h + 1, r2]
        glast = (gg[c - 1:c, h:h + 1], gg[c2 - 1:c2, h:h + 1])
        glast_col = jnp.concatenate([jnp.broadcast_to(glast[0], (c, 1)), jnp.broadcast_to(glast[1], (c, 1))], axis=0)
        per_group = GDN_GROUP // HEAD_DIM
        gcols = slice(HEAD_DIM * (h % per_group), HEAD_DIM * (h % per_group + 1))
        q2 = qkv_ref[0, h // per_group, r2, gcols]
        k2 = qkv_ref[0, GDN_GROUPS // 3 + h // per_group, r2, gcols]
        kf = k2.astype(F32)
        kbeta = kf * beta
        kbeta_b = kbeta.astype(BF16)
        vbeta_b = (qkv_ref[0, 2 * (GDN_GROUPS // 3) + h // per_group, r2, gcols].astype(F32) * beta).astype(BF16)
        eg = jnp.exp(gcol)
        lhs = jnp.concatenate([jnp.concatenate([kbeta_b[:c], kbeta_b[c:]], axis=1),
                               jnp.concatenate([q2[:c], q2[c:]], axis=1)], axis=0)
        rhs = jnp.concatenate([jnp.concatenate([k2[:c], zeros_k], axis=1),
                               jnp.concatenate([zeros_k, k2[c:]], axis=1)], axis=0)
        gcol_pair = jnp.where(lo_half, gcol[:c], gcol[c:])
        pre[(p2, h)] = dict(
            lhs=lhs, rhs=rhs, glast=glast,
            vk=jnp.concatenate([vbeta_b, (kbeta * eg).astype(BF16)], axis=1),
            qg=(q2.astype(F32) * eg).astype(BF16),
            kdec=(kf * jnp.exp(glast_col - gcol)).astype(BF16),
            decay=jnp.exp(jnp.where(incl, gcol_pair - grow, -jnp.inf)))

    kq = {u: _dot_nt(pre[u]["lhs"], pre[u]["rhs"]) for u in units}
    a_pair = {u: kq[u][c:] * pre[u]["decay"] for u in units}
    m = {u: -jnp.where(strict, kq[u][:c] * pre[u]["decay"], 0.0) for u in units}

    p = {u: eye + m[u] for u in units}
    m = {u: _dot(m[u].astype(BF16), _pair_blockdiag(m[u], lo_half)) for u in units}
    for _ in range(int(np.log2(c)) - 2):
        pm = {u: _dot(jnp.concatenate([p[u], m[u]], axis=0).astype(BF16), _pair_blockdiag(m[u], lo_half))
              for u in units}
        p = {u: p[u] + pm[u][:c] for u in units}
        m = {u: pm[u][c:] for u in units}
    pm = {u: _dot(p[u].astype(BF16), _pair_blockdiag(m[u], lo_half)) for u in units}
    t_pair = {u: p[u] + pm[u] for u in units}

    uw = {u: _dot(_pair_blockdiag(t_pair[u], lo_half), pre[u]["vk"]) for u in units}
    a_chunks = {u: (a_pair[u][:, :c].astype(BF16), pltpu.roll(a_pair[u], c, 1)[:, :c].astype(BF16)) for u in units}

    a_mla = (omla_ref[0].astype(F32) * _silu(gate_ref[0, :, 0:hd].astype(F32))).astype(BF16)
    free_parts = [(a_mla, 0), (om_ref[0], 2 * hd)]
    pieces = [(part, n) for part in range(len(free_parts)) for n in range(D_MODEL // ntile)]
    partial = {}

    heads = range(HEADS)
    state = [s_ref[h] for h in heads]
    nsteps = 2 * npairs
    for step in range(nsteps):
        p2, ci = divmod(step, 2)
        rc = slice(c * ci, c * (ci + 1))
        rows = slice(c * step, c * (step + 1))
        ws_qs = [_dot(jnp.concatenate([uw[(p2, h)][rc, HEAD_DIM:].astype(BF16), pre[(p2, h)]["qg"][rc]], axis=0),
                      state[h].astype(BF16)) for h in heads]
        for part, n in pieces[len(pieces) * step // nsteps:len(pieces) * (step + 1) // nsteps]:
            operand, w_lo = free_parts[part]
            partial[(part, n)] = _dot(operand, wout_ref[w_lo:w_lo + hd, ntile * n:ntile * (n + 1)])
        v_new = [(uw[(p2, h)][rc, :HEAD_DIM] - ws_qs[h][:c]).astype(BF16) for h in heads]
        o_intra = [_dot(a_chunks[(p2, h)][ci], v_new[h]) for h in heads]
        ds = [_dot_tn(pre[(p2, h)]["kdec"][rc], v_new[h]) for h in heads]
        state = [state[h] * jnp.exp(pre[(p2, h)]["glast"][ci]) + ds[h] for h in heads]
        for h in heads:
            cols = slice(HEAD_DIM * h, HEAD_DIM * (h + 1))
            gate = _silu(gate_ref[0, rows, hd + HEAD_DIM * h:hd + HEAD_DIM * (h + 1)].astype(F32))
            og_scr[rows, cols] = (_rms(ws_qs[h][c:] + o_intra[h], gn_ref[...]) * gate).astype(BF16)
    for h in heads:
        s_ref[h] = state[h]

    og = og_scr[...]
    ys = []
    for n in range(D_MODEL // ntile):
        ncols = slice(ntile * n, ntile * (n + 1))
        acc = _dot(og, wout_ref[hd:2 * hd, ncols])
        for part in range(len(free_parts)):
            acc = acc + partial[(part, n)]
        ys.append(x_ref[0, :, ncols] + acc)
    ssq = functools.reduce(jnp.add, [jnp.sum(y * y, axis=-1, keepdims=True) for y in ys])
    inv = lax.rsqrt(ssq * (1.0 / D_MODEL) + EPS)
    for n, y in enumerate(ys):
        ncols = slice(ntile * n, ntile * (n + 1))
        out_ref[0, :, ncols] = y * inv * nf_ref[:, ncols]


def _const_spec(shape):
    nd = len(shape)
    return pl.BlockSpec(shape, lambda *_: (0,) * nd)


def _pack_weights(w_in, w_q_b, w_kv_b):
    o = np.cumsum((0,) + IN_SPLITS)
    w_bf = w_in.astype(BF16)
    w_cq, w_ckv, w_kr, w_gdn, w_a, w_b, w_mq, w_gate = (w_bf[:, o[i]:o[i + 1]] for i in range(8))
    half = MLA_ROPE // 2
    w_kr2 = jnp.concatenate([w_kr, w_kr[:, half:], w_kr[:, :half]], axis=1)
    w_ab = jnp.concatenate([w_a, w_b, jnp.zeros((D_MODEL, LANES - 2 * HEADS), BF16)], axis=1)
    w_all = jnp.concatenate([w_cq, w_ckv, w_kr2, w_ab, w_mq, w_gate], axis=1)
    w_groups = w_gdn.reshape(D_MODEL, GDN_GROUPS, GDN_GROUP).transpose(1, 0, 2)
    wq = w_q_b.reshape(Q_LORA, HEADS, HEAD_DIM + MLA_ROPE)
    nope, ropec = wq[..., :HEAD_DIM], wq[..., HEAD_DIM:]
    wqb = jnp.concatenate([nope, ropec, ropec[..., half:], ropec[..., :half]], axis=-1)
    wqb = wqb.reshape(Q_LORA, HEADS * QK_WIDTH).astype(BF16)
    wkv = w_kv_b.reshape(KV_LORA, HEADS, 2 * HEAD_DIM)
    wkk = wkv[..., :HEAD_DIM].reshape(KV_LORA, HEADS * HEAD_DIM).astype(BF16)
    wkvv = wkv[..., HEAD_DIM:].reshape(KV_LORA, HEADS * HEAD_DIM).astype(BF16)
    return w_all, w_groups, wqb, wkk, wkvv


def _lane_row(vec):
    return jnp.zeros((1, LANES), F32).at[0, :vec.shape[0]].set(vec.astype(F32))


def _layer(x, mk, mv, cos, sin, norm_in, w_in, q_a_norm, w_q_b, kv_a_norm, w_kv_b, gdn_conv,
           gdn_a_log, gdn_dt_bias, gdn_norm, w_out, out_gain, *, ts_proj, ts_gdn, tq):
    B, S, D = x.shape
    hd = HEADS * HEAD_DIM
    w_all, w_groups, wqb, wkk, wkvv = _pack_weights(w_in, w_q_b, w_kv_b)
    conv_groups = gdn_conv.reshape(GDN_CONV, GDN_GROUPS, GDN_GROUP).transpose(1, 0, 2)
    M = mk.shape[1]
    arb2 = pltpu.CompilerParams(dimension_semantics=("arbitrary", "arbitrary"), vmem_limit_bytes=VMEM_LIMIT)

    def row_spec(width, ts):
        return pl.BlockSpec((1, ts, width), lambda b, j: (b, j, 0))

    def bs_shape(width, dtype=BF16):
        return jax.ShapeDtypeStruct((B, S, width), dtype)

    q, k, v, gqkv, gg, ggt, om, gate = pl.pallas_call(
        functools.partial(_proj_kernel, ts=ts_proj),
        grid=(B, S // ts_proj),
        in_specs=[row_spec(D, ts_proj), row_spec(MLA_ROPE // 2, ts_proj), row_spec(MLA_ROPE // 2, ts_proj),
                  _const_spec((1, D)), _const_spec(w_all.shape), _const_spec(w_groups.shape),
                  _const_spec((1, Q_LORA)), _const_spec(wqb.shape), _const_spec((1, KV_LORA)),
                  _const_spec(wkk.shape), _const_spec(wkvv.shape), _const_spec(conv_groups.shape),
                  _const_spec((1, LANES)), _const_spec((1, LANES)),
                  pl.BlockSpec((1, M, hd), lambda b, j: (b, 0, 0)), pl.BlockSpec((1, M, hd), lambda b, j: (b, 0, 0))],
        out_specs=[row_spec(HEADS * QK_WIDTH, ts_proj), row_spec(HEADS * QK_WIDTH, ts_proj), row_spec(hd, ts_proj),
                   pl.BlockSpec((1, GDN_GROUPS, ts_proj, GDN_GROUP), lambda b, j: (b, 0, j, 0)),
                   row_spec(LANES, ts_proj), pl.BlockSpec((1, SUBLANES, ts_proj), lambda b, j: (b, 0, j)),
                   row_spec(hd, ts_proj), row_spec(2 * hd, ts_proj)],
        out_shape=[bs_shape(HEADS * QK_WIDTH), bs_shape(HEADS * QK_WIDTH), bs_shape(hd),
                   jax.ShapeDtypeStruct((B, GDN_GROUPS, S, GDN_GROUP), BF16),
                   bs_shape(LANES, F32), jax.ShapeDtypeStruct((B, SUBLANES, S), F32),
                   bs_shape(hd), bs_shape(2 * hd)],
        scratch_shapes=[pltpu.VMEM((GDN_GROUPS, ts_proj + SUBLANES, GDN_GROUP), F32)],
        compiler_params=arb2,
        name="proj",
    )(x, cos, sin, norm_in.reshape(1, D), w_all, w_groups, q_a_norm.reshape(1, Q_LORA), wqb,
      kv_a_norm.reshape(1, KV_LORA), wkk, wkvv, conv_groups, _lane_row(gdn_a_log), _lane_row(gdn_dt_bias), mk, mv)

    o_mla = pl.pallas_call(
        functools.partial(_mla_kernel, tq=tq),
        grid=(B, S // tq),
        in_specs=[pl.BlockSpec((1, tq, HEADS * QK_WIDTH), lambda b, i: (b, i, 0)),
                  pl.BlockSpec((1, S, HEADS * QK_WIDTH), lambda b, i: (b, 0, 0)),
                  pl.BlockSpec((1, S, hd), lambda b, i: (b, 0, 0))],
        out_specs=pl.BlockSpec((1, tq, hd), lambda b, i: (b, i, 0)),
        out_shape=bs_shape(hd),
        scratch_shapes=[pltpu.VMEM((HEADS, tq, tq), F32), pltpu.VMEM((HEADS, tq, tq), BF16),
                        pltpu.VMEM((HEADS, tq, LANES), F32), pltpu.VMEM((HEADS, tq, LANES), F32),
                        pltpu.VMEM((HEADS, tq, LANES), F32), pltpu.VMEM((HEADS, tq, HEAD_DIM), F32)],
        compiler_params=arb2,
        name="mla",
    )(q, k, v)

    return pl.pallas_call(
        functools.partial(_gdn_merge_kernel, ts=ts_gdn),
        grid=(B, S // ts_gdn),
        in_specs=[pl.BlockSpec((1, GDN_GROUPS, ts_gdn, GDN_GROUP), lambda b, j: (b, 0, j, 0)), row_spec(LANES, ts_gdn),
                  pl.BlockSpec((1, SUBLANES, ts_gdn), lambda b, j: (b, 0, j)), _const_spec((1, HEAD_DIM)),
                  row_spec(D, ts_gdn), row_spec(hd, ts_gdn), row_spec(hd, ts_gdn), row_spec(2 * hd, ts_gdn),
                  _const_spec((D_MIX, D)), _const_spec((1, D))],
        out_specs=row_spec(D, ts_gdn),
        out_shape=bs_shape(D, F32),
        scratch_shapes=[pltpu.VMEM((HEADS, HEAD_DIM, HEAD_DIM), F32), pltpu.VMEM((ts_gdn, hd), BF16)],
        compiler_params=arb2,
        name="gdn_merge",
    )(gqkv, gg, ggt, gdn_norm.reshape(1, HEAD_DIM), x, o_mla, om, gate, w_out.astype(BF16),
      out_gain.reshape(1, D))


def _tile(n, pref):
    return pref if n % pref == 0 else n


def kernel(x, mem, positions, norm_in, w_in, q_a_norm, w_q_b, kv_a_norm, w_kv_b, gdn_conv, gdn_a_log,
           gdn_dt_bias, gdn_norm, mem_norm, w_mem_kv, w_out, norm_final):
    B, S, D = x.shape
    M = mem.shape[1]
    depth = norm_in.shape[0]
    assert depth == 1, "the final norm is fused into the single layer's last kernel"
    hd = HEADS * HEAD_DIM
    half = MLA_ROPE // 2
    inv_freq = 1.0 / (ROPE_THETA ** (jnp.arange(half, dtype=F32) / half))
    per_row = LANES // half
    invf = jnp.tile(inv_freq, per_row).reshape(1, LANES)
    pos_rows = jnp.repeat(positions.reshape(B * S // per_row, per_row), half, axis=1)
    n_rows = B * S // per_row
    tr = _tile(n_rows, 1024)
    cos, sin = pl.pallas_call(
        _rope_kernel,
        grid=(n_rows // tr,),
        in_specs=[pl.BlockSpec((tr, LANES), lambda r: (r, 0)), _const_spec((1, LANES))],
        out_specs=[pl.BlockSpec((tr, LANES), lambda r: (r, 0))] * 2,
        out_shape=[jax.ShapeDtypeStruct((n_rows, LANES), F32)] * 2,
        compiler_params=pltpu.CompilerParams(dimension_semantics=("arbitrary",), vmem_limit_bytes=VMEM_LIMIT),
        name="rope",
    )(pos_rows, invf)
    cos = cos.reshape(B, S, half)
    sin = sin.reshape(B, S, half)
    l = 0
    nb_mem = _tile(B, 4)
    mk, mv = pl.pallas_call(
        _memkv_kernel,
        grid=(B // nb_mem,),
        in_specs=[pl.BlockSpec((nb_mem, M, D), lambda b: (b, 0, 0)), _const_spec((1, D)), _const_spec((D, 2 * hd))],
        out_specs=[pl.BlockSpec((nb_mem, M, hd), lambda b: (b, 0, 0))] * 2,
        out_shape=[jax.ShapeDtypeStruct((B, M, hd), BF16)] * 2,
        compiler_params=pltpu.CompilerParams(dimension_semantics=("arbitrary",), vmem_limit_bytes=VMEM_LIMIT),
        name="memkv",
    )(mem, mem_norm[l].reshape(1, D), w_mem_kv[l].astype(BF16))
    return _layer(x, mk, mv, cos, sin, norm_in[l], w_in[l], q_a_norm[l], w_q_b[l], kv_a_norm[l], w_kv_b[l],
                  gdn_conv[l], gdn_a_log[l], gdn_dt_bias[l], gdn_norm[l], w_out[l], norm_final,
                  ts_proj=_tile(S, 512), ts_gdn=_tile(S, 512), tq=_tile(S, 512))
```

```python
import functools

import jax
import jax.numpy as jnp
import numpy as np
from jax import lax
from jax.experimental import pallas as pl
from jax.experimental.pallas import tpu as pltpu

F32 = jnp.float32
BF16 = jnp.bfloat16

D_MODEL = 1024
HEADS = 4
HEAD_DIM = 128
MLA_ROPE = 64
Q_LORA = 384
KV_LORA = 256
ROPE_THETA = 10000.0
GDN_CONV = 4
GDN_CHUNK = 64
GDN_QKV = 3 * HEADS * HEAD_DIM
D_MIX = 3 * HEADS * HEAD_DIM
IN_SPLITS = (Q_LORA, KV_LORA, MLA_ROPE, GDN_QKV, HEADS, HEADS, HEADS * HEAD_DIM, D_MIX)
EPS = 1e-6
MLA_SCALE = (HEAD_DIM + MLA_ROPE) ** -0.5
LOG2E = 1.4426950408889634
MEM_SCALE = HEAD_DIM ** -0.5
GDN_QSCALE = HEAD_DIM ** -0.5

LANES = 128
SUBLANES = 8
QK_WIDTH = 2 * LANES
MLA_STRIP = 64

C_CQ = 0
C_CKV = C_CQ + Q_LORA
C_KR = C_CKV + KV_LORA
C_AB = C_KR + LANES
C_MQ = C_AB + LANES
C_GATE = C_MQ + HEADS * HEAD_DIM
C_END = C_GATE + D_MIX

GDN_GROUP = 2 * LANES
GDN_GROUPS = GDN_QKV // GDN_GROUP

VMEM_LIMIT = 56 * 1024 * 1024


def _dot(a, b, precision=None):
    return jnp.dot(a, b, preferred_element_type=F32, precision=precision)


def _dot_nt(a, b):
    return lax.dot_general(a, b, (((1,), (1,)), ((), ())), preferred_element_type=F32)


def _dot_tn(a, b):
    return lax.dot_general(a, b, (((0,), (0,)), ((), ())), preferred_element_type=F32)


def _rms(t, gain):
    return t * lax.rsqrt(jnp.mean(t * t, axis=-1, keepdims=True) + EPS) * gain


def _sigmoid(t):
    return 1.0 / (1.0 + jnp.exp(-t))


def _silu(t):
    half = 0.5 * t
    return half + half * jnp.tanh(half)


def _memkv_kernel(mem_ref, gain_ref, w_ref, mk_out, mv_out):
    nb, m, d = mem_ref.shape
    hm = _rms(mem_ref[...].reshape(nb * m, d), gain_ref[...]).astype(BF16)
    kv = _dot(hm, w_ref[...])
    half = HEADS * HEAD_DIM
    mk_out[...] = kv[:, :half].astype(BF16).reshape(nb, m, half)
    mv_out[...] = kv[:, half:].astype(BF16).reshape(nb, m, half)


def _pack_kernel(w_ref, wall_out, wg_out):
    o = [int(v) for v in np.cumsum((0,) + IN_SPLITS)]
    rows = w_ref.shape[1]
    lane = lax.broadcasted_iota(jnp.int32, (rows, LANES), 1)
    half = MLA_ROPE // 2
    xk = w_ref[0, :, o[2]:o[2] + LANES]
    kr2 = jnp.where(lane < MLA_ROPE, xk,
                    jnp.where(lane < MLA_ROPE + half, pltpu.roll(xk, half, 1), pltpu.roll(xk, LANES - half, 1)))
    xab = w_ref[0, :, o[4]:o[4] + LANES]
    wall_out[:, C_CQ:C_KR] = w_ref[0, :, o[0]:o[2]].astype(BF16)
    wall_out[:, C_KR:C_AB] = kr2.astype(BF16)
    wall_out[:, C_AB:C_MQ] = jnp.where(lane < 2 * HEADS, xab, 0.0).astype(BF16)
    wall_out[:, C_MQ:C_GATE] = w_ref[0, :, o[6]:o[7]].astype(BF16)
    wall_out[:, C_GATE:C_END] = w_ref[0, :, o[7]:o[8]].astype(BF16)
    for g in range(GDN_GROUPS):
        wg_out[g] = w_ref[0, :, o[3] + GDN_GROUP * g:o[3] + GDN_GROUP * (g + 1)].astype(BF16)


def _rope_kernel(pos_ref, invf_ref, cos_out, sin_out):
    ang = pos_ref[...].astype(F32) * invf_ref[...]
    cos_out[...] = jnp.cos(ang)
    sin_out[...] = jnp.sin(ang)


def _proj_kernel(x_ref, cos_ref, sin_ref, nin_ref, w_ref, wg_ref, qan_ref, wqb_ref, kvan_ref, wkk_ref, wkv_ref,
                 conv_ref, alog_ref, dtb_ref, mk_ref, mv_ref,
                 q_out, k_out, v_out, gqkv_out, gg_out, ggt_out, om_out, gate_out,
                 cbuf, *, ts):
    j = pl.program_id(1)
    hd = HEADS * HEAD_DIM
    hb = _rms(x_ref[0], nin_ref[...]).astype(BF16)

    def proj(lo, hi):
        return _dot(hb, w_ref[:, lo:hi])

    @pl.when(j == 0)
    def _():
        cbuf[:, 0:SUBLANES, :] = jnp.zeros((GDN_GROUPS, SUBLANES, GDN_GROUP), F32)

    @pl.when(j > 0)
    def _():
        cbuf[:, 0:SUBLANES, :] = cbuf[:, ts:ts + SUBLANES, :]

    def gdn_dot(g):
        cbuf[g, SUBLANES:ts + SUBLANES, :] = _dot(hb, wg_ref[g])

    def gdn_epilogue(g):
        taps = conv_ref[g]
        for sub in range(GDN_GROUP // LANES):
            cols = slice(LANES * sub, LANES * (sub + 1))
            blk = cbuf[g, :, cols]
            acc = taps[GDN_CONV - 1:GDN_CONV, cols] * blk[SUBLANES:]
            for back in range(1, GDN_CONV):
                acc = acc + taps[GDN_CONV - 1 - back:GDN_CONV - back, cols] * pltpu.roll(blk, back, 0)[SUBLANES:]
            y = _silu(acc)
            if g < 2 * (GDN_GROUPS // 3):
                inv = lax.rsqrt(jnp.sum(y * y, axis=-1, keepdims=True) + EPS)
                y = y * (inv * GDN_QSCALE if g < GDN_GROUPS // 3 else inv)
            gqkv_out[0, g, :, cols] = y.astype(BF16)

    narrow = proj(C_CQ, C_MQ)
    cq = narrow[:, C_CQ:C_CKV]
    ckv = narrow[:, C_CKV:C_KR]
    kr_raw = narrow[:, C_KR:C_AB]
    ab = narrow[:, C_AB:C_MQ]
    cqn = _rms(cq, qan_ref[...]).astype(BF16)
    gdn_dot(0)
    ckvn = _rms(ckv, kvan_ref[...]).astype(BF16)
    qf = _dot(cqn, wqb_ref[...])

    c32, s32 = cos_ref[0], sin_ref[0]
    zpad = jnp.zeros((ts, LANES - MLA_ROPE), F32)
    cosm = jnp.concatenate([c32, c32, zpad], axis=1)
    sinm = jnp.concatenate([-s32, s32, zpad], axis=1)

    def rope(r):
        return r * cosm + pltpu.roll(r, MLA_ROPE, 1) * sinm

    gdn_dot(1)
    gdn_epilogue(0)
    kn = _dot(ckvn, wkk_ref[...])
    vv = _dot(ckvn, wkv_ref[...])
    qscale = MLA_SCALE * LOG2E
    for h in range(HEADS):
        lo = QK_WIDTH * h
        q_out[0, :, lo:lo + LANES] = (qf[:, lo:lo + LANES] * qscale).astype(BF16)
        q_out[0, :, lo + LANES:lo + QK_WIDTH] = (rope(qf[:, lo + LANES:lo + QK_WIDTH]) * qscale).astype(BF16)
    gdn_dot(2)
    gdn_epilogue(1)
    mq = proj(C_MQ, C_GATE)
    v_out[0] = vv.astype(BF16)
    kr = rope(kr_raw).astype(BF16)
    for h in range(HEADS):
        lo = QK_WIDTH * h
        k_out[0, :, lo:lo + LANES] = kn[:, LANES * h:LANES * (h + 1)].astype(BF16)
        k_out[0, :, lo + LANES:lo + QK_WIDTH] = kr
    gdn_dot(3)
    gdn_epilogue(2)

    mscale = MEM_SCALE * LOG2E
    sc, pp, ll, oo = {}, {}, {}, {}

    def mem_qk(h):
        cols = slice(HEAD_DIM * h, HEAD_DIM * (h + 1))
        sc[h] = _dot_nt((mq[:, cols] * mscale).astype(BF16), mk_ref[0, :, cols])

    def mem_softmax(h):
        p = jnp.exp2(sc[h] - jnp.max(sc[h], axis=-1, keepdims=True))
        ll[h] = jnp.sum(p, axis=-1, keepdims=True)
        pp[h] = p.astype(BF16)

    def mem_pv(h):
        oo[h] = _dot(pp[h], mv_ref[0, :, HEAD_DIM * h:HEAD_DIM * (h + 1)])

    mem_qk(0)
    z = ab + dtb_ref[...]
    softplus = jnp.maximum(z, 0.0) + jnp.log1p(jnp.exp(-jnp.abs(z)))
    gcum = -jnp.exp(alog_ref[...]) * softplus
    lane = lax.broadcasted_iota(jnp.int32, ab.shape, 1)
    pos_in_chunk = lax.broadcasted_iota(jnp.int32, ab.shape, 0) % GDN_CHUNK
    shift = 1
    while shift < GDN_CHUNK:
        gcum = gcum + jnp.where(pos_in_chunk >= shift, pltpu.roll(gcum, shift, 0), 0.0)
        shift *= 2
    gg_out[0] = jnp.where(lane < HEADS, gcum, _sigmoid(ab))
    ggt_out[0] = gcum.T[0:SUBLANES, :]

    gdn_dot(4)
    gdn_epilogue(3)
    mem_qk(1)
    mem_softmax(0)
    gdn_dot(5)
    gdn_epilogue(4)
    mem_qk(2)
    mem_pv(0)
    mem_softmax(1)
    gate_out[0, :, 0:hd] = proj(C_GATE, C_GATE + hd).astype(BF16)
    gdn_epilogue(5)
    mem_qk(3)
    mem_pv(1)
    mem_softmax(2)
    gate_out[0, :, hd:2 * hd] = proj(C_GATE + hd, C_GATE + 2 * hd).astype(BF16)
    mem_pv(2)
    mem_softmax(3)
    gmem = _silu(proj(C_GATE + 2 * hd, C_END))
    mem_pv(3)
    for h in range(HEADS):
        cols = slice(HEAD_DIM * h, HEAD_DIM * (h + 1))
        om_out[0, :, cols] = (oo[h] / ll[h] * gmem[:, cols]).astype(BF16)


def _mla_kernel(q_ref, k_ref, v_ref, o_ref, s_scr, p_scr, m_scr, l_scr, a_scr, acc_scr, *, tq):
    i = pl.program_id(1)
    heads = range(HEADS)
    strip = MLA_STRIP

    def step(jk, diagonal):
        start = pl.multiple_of(jk * tq, tq)
        if diagonal:
            half = tq // 2
            items = [(h, r0, half, r0 + half) for h in heads for r0 in (0, half)]
        else:
            items = [(h, 0, tq, tq) for h in heads]

        def scores(item):
            h, r0, nr, nk = item
            s_scr[h, r0:r0 + nr, 0:nk] = _dot_nt(q_ref[0, r0:r0 + nr, QK_WIDTH * h:QK_WIDTH * (h + 1)],
                                                 k_ref[0, pl.ds(start, nk), QK_WIDTH * h:QK_WIDTH * (h + 1)])

        def load_strip(item, r):
            h, r0, nr, nk = item
            s = s_scr[h, r0 + strip * r:r0 + strip * (r + 1), 0:nk]
            if diagonal:
                row = lax.broadcasted_iota(jnp.int32, s.shape, 0) + (r0 + strip * r)
                col = lax.broadcasted_iota(jnp.int32, s.shape, 1)
                s = jnp.where(row >= col, s, -jnp.inf)
            return s

        def lane_tiles(t):
            return [t[:, LANES * g:LANES * (g + 1)] for g in range(t.shape[1] // LANES)]

        def softmax(item):
            h, r0, nr, nk = item
            strips = range(nr // strip)
            rows = [slice(r0 + strip * r, r0 + strip * (r + 1)) for r in strips]
            part = [functools.reduce(jnp.maximum, lane_tiles(load_strip(item, r))) for r in strips]
            peak = [jnp.broadcast_to(jnp.max(t, axis=-1, keepdims=True), (strip, LANES)) for t in part]
            if diagonal:
                m_new = peak
            else:
                m_old = [m_scr[h, rows[r], :] for r in strips]
                m_new = [jnp.maximum(m_old[r], peak[r]) for r in strips]
                alpha = [jnp.exp2(m_old[r] - m_new[r]) for r in strips]
                for r in strips:
                    a_scr[h, rows[r], :] = alpha[r]
            for r in strips:
                m_scr[h, rows[r], :] = m_new[r]
            part = []
            for r in strips:
                p = [jnp.exp2(t - m_new[r]) for t in lane_tiles(load_strip(item, r))]
                for g, t in enumerate(p):
                    p_scr[h, rows[r], LANES * g:LANES * (g + 1)] = t.astype(BF16)
                part.append(functools.reduce(jnp.add, p))
            total = [jnp.broadcast_to(jnp.sum(t, axis=-1, keepdims=True), (strip, LANES)) for t in part]
            for r in strips:
                l_scr[h, rows[r], :] = total[r] if diagonal else alpha[r] * l_scr[h, rows[r], :] + total[r]

        def values(item):
            h, r0, nr, nk = item
            pv = _dot(p_scr[h, r0:r0 + nr, 0:nk], v_ref[0, pl.ds(start, nk), HEAD_DIM * h:HEAD_DIM * (h + 1)])
            if diagonal:
                acc_scr[h, r0:r0 + nr, :] = pv
            else:
                acc_scr[h, r0:r0 + nr, :] = a_scr[h, r0:r0 + nr, :] * acc_scr[h, r0:r0 + nr, :] + pv

        for t in range(len(items) + 2):
            if t < len(items):
                scores(items[t])
            if 0 <= t - 1 < len(items):
                softmax(items[t - 1])
            if 0 <= t - 2 < len(items):
                values(items[t - 2])

    def body(jk, carry):
        step(jk, False)
        return carry

    step(i, True)
    lax.fori_loop(0, i, body, 0)
    for h in heads:
        o_ref[0, :, HEAD_DIM * h:HEAD_DIM * (h + 1)] = (acc_scr[h] / l_scr[h]).astype(BF16)


def _pair_blockdiag(t, lo_half):
    return jnp.concatenate([jnp.where(lo_half, t, 0.0), jnp.where(lo_half, 0.0, t)], axis=0).astype(BF16)


def _gdn_merge_kernel(qkv_ref, gg_ref, ggt_ref, gn_ref, x_ref, omla_ref, om_ref, gate_ref, wout_ref,
                      nf_ref, out_ref, s_ref, og_scr, *, ts):
    j = pl.program_id(1)
    hd = HEADS * HEAD_DIM
    ntile = 2 * LANES

    @pl.when(j == 0)
    def _():
        s_ref[...] = jnp.zeros(s_ref.shape, F32)

    c = GDN_CHUNK
    c2 = 2 * c
    npairs = ts // c2
    row = lax.broadcasted_iota(jnp.int32, (c, c2), 0)
    lane = lax.broadcasted_iota(jnp.int32, (c, c2), 1)
    lo_half = lane < c
    col = jnp.where(lo_half, lane, lane - c)
    incl = row >= col
    strict = row > col
    eye = jnp.where(row == col, 1.0, 0.0)
    zeros_k = jnp.zeros((c, HEAD_DIM), BF16)
    units = [(p2, h) for p2 in range(npairs) for h in range(HEADS)]

    pre = {}
    for (p2, h) in units:
        r2 = slice(c2 * p2, c2 * (p2 + 1))
        cols = slice(HEAD_DIM * h, HEAD_DIM * (h + 1))
        gg = gg_ref[0, r2, :]
        gcol = gg[:, h:h + 1]
        beta = gg[:, HEADS + h:HEADS + h + 1]
        grow = ggt_ref[0, h:h + 1, r2]
        glast = (gg[c - 1:c, h:h + 1], gg[c2 - 1:c2, h:h + 1])
        glast_col = jnp.concatenate([jnp.broadcast_to(glast[0], (c, 1)), jnp.broadcast_to(glast[1], (c, 1))], axis=0)
        per_group = GDN_GROUP // HEAD_DIM
        gcols = slice(HEAD_DIM * (h % per_group), HEAD_DIM * (h % per_group + 1))
        q2 = qkv_ref[0, h // per_group, r2, gcols]
        k2 = qkv_ref[0, GDN_GROUPS // 3 + h // per_group, r2, gcols]
        kf = k2.astype(F32)
        kbeta = kf * beta
        kbeta_b = kbeta.astype(BF16)
        vbeta_b = (qkv_ref[0, 2 * (GDN_GROUPS // 3) + h // per_group, r2, gcols].astype(F32) * beta).astype(BF16)
        eg = jnp.exp(gcol)
        lhs = jnp.concatenate([jnp.concatenate([kbeta_b[:c], kbeta_b[c:]], axis=1),
                               jnp.concatenate([q2[:c], q2[c:]], axis=1)], axis=0)
        rhs = jnp.concatenate([jnp.concatenate([k2[:c], zeros_k], axis=1),
                               jnp.concatenate([zeros_k, k2[c:]], axis=1)], axis=0)
        gcol_pair = jnp.where(lo_half, gcol[:c], gcol[c:])
        pre[(p2, h)] = dict(
            lhs=lhs, rhs=rhs, glast=glast,
            vk=jnp.concatenate([vbeta_b, (kbeta * eg).astype(BF16)], axis=1),
            qg=(q2.astype(F32) * eg).astype(BF16),
            kdec=(kf * jnp.exp(glast_col - gcol)).astype(BF16),
            decay=jnp.exp(jnp.where(incl, gcol_pair - grow, -jnp.inf)))

    kq = {u: _dot_nt(pre[u]["lhs"], pre[u]["rhs"]) for u in units}
    a_pair = {u: kq[u][c:] * pre[u]["decay"] for u in units}
    m = {u: -jnp.where(strict, kq[u][:c] * pre[u]["decay"], 0.0) for u in units}

    p = {u: eye + m[u] for u in units}
    m = {u: _dot(m[u].astype(BF16), _pair_blockdiag(m[u], lo_half)) for u in units}
    for _ in range(int(np.log2(c)) - 2):
        pm = {u: _dot(jnp.concatenate([p[u], m[u]], axis=0).astype(BF16), _pair_blockdiag(m[u], lo_half))
              for u in units}
        p = {u: p[u] + pm[u][:c] for u in units}
        m = {u: pm[u][c:] for u in units}
    pm = {u: _dot(p[u].astype(BF16), _pair_blockdiag(m[u], lo_half)) for u in units}
    t_pair = {u: p[u] + pm[u] for u in units}

    uw = {u: _dot(_pair_blockdiag(t_pair[u], lo_half), pre[u]["vk"]) for u in units}
    a_chunks = {u: (a_pair[u][:, :c].astype(BF16), pltpu.roll(a_pair[u], c, 1)[:, :c].astype(BF16)) for u in units}

    a_mla = (omla_ref[0].astype(F32) * _silu(gate_ref[0, :, 0:hd].astype(F32))).astype(BF16)
    free_parts = [(a_mla, 0), (om_ref[0], 2 * hd)]
    pieces = [(part, n) for part in range(len(free_parts)) for n in range(D_MODEL // ntile)]
    partial = {}

    heads = range(HEADS)
    state = [s_ref[h] for h in heads]
    nsteps = 2 * npairs
    for step in range(nsteps):
        p2, ci = divmod(step, 2)
        rc = slice(c * ci, c * (ci + 1))
        rows = slice(c * step, c * (step + 1))
        ws_qs = [_dot(jnp.concatenate([uw[(p2, h)][rc, HEAD_DIM:].astype(BF16), pre[(p2, h)]["qg"][rc]], axis=0),
                      state[h].astype(BF16)) for h in heads]
        for part, n in pieces[len(pieces) * step // nsteps:len(pieces) * (step + 1) // nsteps]:
            operand, w_lo = free_parts[part]
            partial[(part, n)] = _dot(operand, wout_ref[w_lo:w_lo + hd, ntile * n:ntile * (n + 1)])
        v_new = [(uw[(p2, h)][rc, :HEAD_DIM] - ws_qs[h][:c]).astype(BF16) for h in heads]
        o_intra = [_dot(a_chunks[(p2, h)][ci], v_new[h]) for h in heads]
        ds = [_dot_tn(pre[(p2, h)]["kdec"][rc], v_new[h]) for h in heads]
        state = [state[h] * jnp.exp(pre[(p2, h)]["glast"][ci]) + ds[h] for h in heads]
        for h in heads:
            cols = slice(HEAD_DIM * h, HEAD_DIM * (h + 1))
            gate = _silu(gate_ref[0, rows, hd + HEAD_DIM * h:hd + HEAD_DIM * (h + 1)].astype(F32))
            og_scr[rows, cols] = (_rms(ws_qs[h][c:] + o_intra[h], gn_ref[...]) * gate).astype(BF16)
    for h in heads:
        s_ref[h] = state[h]

    og = og_scr[...]
    ys = []
    for n in range(D_MODEL // ntile):
        ncols = slice(ntile * n, ntile * (n + 1))
        acc = _dot(og, wout_ref[hd:2 * hd, ncols])
        for part in range(len(free_parts)):
            acc = acc + partial[(part, n)]
        ys.append(x_ref[0, :, ncols] + acc)
    ssq = functools.reduce(jnp.add, [jnp.sum(y * y, axis=-1, keepdims=True) for y in ys])
    inv = lax.rsqrt(ssq * (1.0 / D_MODEL) + EPS)
    for n, y in enumerate(ys):
        ncols = slice(ntile * n, ntile * (n + 1))
        out_ref[0, :, ncols] = y * inv * nf_ref[:, ncols]


def _const_spec(shape):
    nd = len(shape)
    return pl.BlockSpec(shape, lambda *_: (0,) * nd)


def _pack_weights(w_in_all, layer, w_q_b, w_kv_b):
    tr = _tile(D_MODEL, 256)
    w_all, w_groups = pl.pallas_call(
        _pack_kernel,
        grid=(D_MODEL // tr,),
        in_specs=[pl.BlockSpec((1, tr, w_in_all.shape[2]), lambda r: (layer, r, 0))],
        out_specs=[pl.BlockSpec((tr, C_END), lambda r: (r, 0)),
                   pl.BlockSpec((GDN_GROUPS, tr, GDN_GROUP), lambda r: (0, r, 0))],
        out_shape=[jax.ShapeDtypeStruct((D_MODEL, C_END), BF16),
                   jax.ShapeDtypeStruct((GDN_GROUPS, D_MODEL, GDN_GROUP), BF16)],
        compiler_params=pltpu.CompilerParams(dimension_semantics=("arbitrary",), vmem_limit_bytes=VMEM_LIMIT),
        name="pack",
    )(w_in_all)
    half = MLA_ROPE // 2
    wq = w_q_b.reshape(Q_LORA, HEADS, HEAD_DIM + MLA_ROPE)
    nope, ropec = wq[..., :HEAD_DIM], wq[..., HEAD_DIM:]
    wqb = jnp.concatenate([nope, ropec, ropec[..., half:], ropec[..., :half]], axis=-1)
    wqb = wqb.reshape(Q_LORA, HEADS * QK_WIDTH).astype(BF16)
    wkv = w_kv_b.reshape(KV_LORA, HEADS, 2 * HEAD_DIM)
    wkk = wkv[..., :HEAD_DIM].reshape(KV_LORA, HEADS * HEAD_DIM).astype(BF16)
    wkvv = wkv[..., HEAD_DIM:].reshape(KV_LORA, HEADS * HEAD_DIM).astype(BF16)
    return w_all, w_groups, wqb, wkk, wkvv


def _lane_row(vec):
    return jnp.zeros((1, LANES), F32).at[0, :vec.shape[0]].set(vec.astype(F32))


def _layer(x, mk, mv, cos, sin, norm_in, w_in_all, layer, q_a_norm, w_q_b, kv_a_norm, w_kv_b, gdn_conv,
           gdn_a_log, gdn_dt_bias, gdn_norm, w_out, out_gain, *, ts_proj, ts_gdn, tq):
    B, S, D = x.shape
    hd = HEADS * HEAD_DIM
    w_all, w_groups, wqb, wkk, wkvv = _pack_weights(w_in_all, layer, w_q_b, w_kv_b)
    conv_groups = gdn_conv.reshape(GDN_CONV, GDN_GROUPS, GDN_GROUP).transpose(1, 0, 2)
    M = mk.shape[1]
    arb2 = pltpu.CompilerParams(dimension_semantics=("arbitrary", "arbitrary"), vmem_limit_bytes=VMEM_LIMIT)

    def row_spec(width, ts):
        return pl.BlockSpec((1, ts, width), lambda b, j: (b, j, 0))

    def bs_shape(width, dtype=BF16):
        return jax.ShapeDtypeStruct((B, S, width), dtype)

    q, k, v, gqkv, gg, ggt, om, gate = pl.pallas_call(
        functools.partial(_proj_kernel, ts=ts_proj),
        grid=(B, S // ts_proj),
        in_specs=[row_spec(D, ts_proj), row_spec(MLA_ROPE // 2, ts_proj), row_spec(MLA_ROPE // 2, ts_proj),
                  _const_spec((1, D)), _const_spec(w_all.shape), _const_spec(w_groups.shape),
                  _const_spec((1, Q_LORA)), _const_spec(wqb.shape), _const_spec((1, KV_LORA)),
                  _const_spec(wkk.shape), _const_spec(wkvv.shape), _const_spec(conv_groups.shape),
                  _const_spec((1, LANES)), _const_spec((1, LANES)),
                  pl.BlockSpec((1, M, hd), lambda b, j: (b, 0, 0)), pl.BlockSpec((1, M, hd), lambda b, j: (b, 0, 0))],
        out_specs=[row_spec(HEADS * QK_WIDTH, ts_proj), row_spec(HEADS * QK_WIDTH, ts_proj), row_spec(hd, ts_proj),
                   pl.BlockSpec((1, GDN_GROUPS, ts_proj, GDN_GROUP), lambda b, j: (b, 0, j, 0)),
                   row_spec(LANES, ts_proj), pl.BlockSpec((1, SUBLANES, ts_proj), lambda b, j: (b, 0, j)),
                   row_spec(hd, ts_proj), row_spec(2 * hd, ts_proj)],
        out_shape=[bs_shape(HEADS * QK_WIDTH), bs_shape(HEADS * QK_WIDTH), bs_shape(hd),
                   jax.ShapeDtypeStruct((B, GDN_GROUPS, S, GDN_GROUP), BF16),
                   bs_shape(LANES, F32), jax.ShapeDtypeStruct((B, SUBLANES, S), F32),
                   bs_shape(hd), bs_shape(2 * hd)],
        scratch_shapes=[pltpu.VMEM((GDN_GROUPS, ts_proj + SUBLANES, GDN_GROUP), F32)],
        compiler_params=arb2,
        name="proj",
    )(x, cos, sin, norm_in.reshape(1, D), w_all, w_groups, q_a_norm.reshape(1, Q_LORA), wqb,
      kv_a_norm.reshape(1, KV_LORA), wkk, wkvv, conv_groups, _lane_row(gdn_a_log), _lane_row(gdn_dt_bias), mk, mv)

    o_mla = pl.pallas_call(
        functools.partial(_mla_kernel, tq=tq),
        grid=(B, S // tq),
        in_specs=[pl.BlockSpec((1, tq, HEADS * QK_WIDTH), lambda b, i: (b, i, 0)),
                  pl.BlockSpec((1, S, HEADS * QK_WIDTH), lambda b, i: (b, 0, 0)),
                  pl.BlockSpec((1, S, hd), lambda b, i: (b, 0, 0))],
        out_specs=pl.BlockSpec((1, tq, hd), lambda b, i: (b, i, 0)),
        out_shape=bs_shape(hd),
        scratch_shapes=[pltpu.VMEM((HEADS, tq, tq), F32), pltpu.VMEM((HEADS, tq, tq), BF16),
                        pltpu.VMEM((HEADS, tq, LANES), F32), pltpu.VMEM((HEADS, tq, LANES), F32),
                        pltpu.VMEM((HEADS, tq, LANES), F32), pltpu.VMEM((HEADS, tq, HEAD_DIM), F32)],
        compiler_params=arb2,
        name="mla",
    )(q, k, v)

    return pl.pallas_call(
        functools.partial(_gdn_merge_kernel, ts=ts_gdn),
        grid=(B, S // ts_gdn),
        in_specs=[pl.BlockSpec((1, GDN_GROUPS, ts_gdn, GDN_GROUP), lambda b, j: (b, 0, j, 0)), row_spec(LANES, ts_gdn),
                  pl.BlockSpec((1, SUBLANES, ts_gdn), lambda b, j: (b, 0, j)), _const_spec((1, HEAD_DIM)),
                  row_spec(D, ts_gdn), row_spec(hd, ts_gdn), row_spec(hd, ts_gdn), row_spec(2 * hd, ts_gdn),
                  _const_spec((D_MIX, D)), _const_spec((1, D))],
        out_specs=row_spec(D, ts_gdn),
        out_shape=bs_shape(D, F32),
        scratch_shapes=[pltpu.VMEM((HEADS, HEAD_DIM, HEAD_DIM), F32), pltpu.VMEM((ts_gdn, hd), BF16)],
        compiler_params=arb2,
        name="gdn_merge",
    )(gqkv, gg, ggt, gdn_norm.reshape(1, HEAD_DIM), x, o_mla, om, gate, w_out.astype(BF16),
      out_gain.reshape(1, D))


def _tile(n, pref):
    return pref if n % pref == 0 else n


def kernel(x, mem, positions, norm_in, w_in, q_a_norm, w_q_b, kv_a_norm, w_kv_b, gdn_conv, gdn_a_log,
           gdn_dt_bias, gdn_norm, mem_norm, w_mem_kv, w_out, norm_final):
    B, S, D = x.shape
    M = mem.shape[1]
    depth = norm_in.shape[0]
    assert depth == 1, "the final norm is fused into the single layer's last kernel"
    hd = HEADS * HEAD_DIM
    half = MLA_ROPE // 2
    inv_freq = 1.0 / (ROPE_THETA ** (jnp.arange(half, dtype=F32) / half))
    per_row = LANES // half
    invf = jnp.tile(inv_freq, per_row).reshape(1, LANES)
    pos_rows = jnp.repeat(positions.reshape(B * S // per_row, per_row), half, axis=1)
    n_rows = B * S // per_row
    tr = _tile(n_rows, 1024)
    cos, sin = pl.pallas_call(
        _rope_kernel,
        grid=(n_rows // tr,),
        in_specs=[pl.BlockSpec((tr, LANES), lambda r: (r, 0)), _const_spec((1, LANES))],
        out_specs=[pl.BlockSpec((tr, LANES), lambda r: (r, 0))] * 2,
        out_shape=[jax.ShapeDtypeStruct((n_rows, LANES), F32)] * 2,
        compiler_params=pltpu.CompilerParams(dimension_semantics=("arbitrary",), vmem_limit_bytes=VMEM_LIMIT),
        name="rope",
    )(pos_rows, invf)
    cos = cos.reshape(B, S, half)
    sin = sin.reshape(B, S, half)
    l = 0
    nb_mem = _tile(B, 4)
    mk, mv = pl.pallas_call(
        _memkv_kernel,
        grid=(B // nb_mem,),
        in_specs=[pl.BlockSpec((nb_mem, M, D), lambda b: (b, 0, 0)), _const_spec((1, D)), _const_spec((D, 2 * hd))],
        out_specs=[pl.BlockSpec((nb_mem, M, hd), lambda b: (b, 0, 0))] * 2,
        out_shape=[jax.ShapeDtypeStruct((B, M, hd), BF16)] * 2,
        compiler_params=pltpu.CompilerParams(dimension_semantics=("arbitrary",), vmem_limit_bytes=VMEM_LIMIT),
        name="memkv",
    )(mem, mem_norm[l].reshape(1, D), w_mem_kv[l].astype(BF16))
    return _layer(x, mk, mv, cos, sin, norm_in[l], w_in, l, q_a_norm[l], w_q_b[l], kv_a_norm[l], w_kv_b[l],
                  gdn_conv[l], gdn_a_log[l], gdn_dt_bias[l], gdn_norm[l], w_out[l], norm_final,
                  ts_proj=_tile(S, 512), ts_gdn=_tile(S, 512), tq=_tile(S, 512))
```

```python
import functools

import jax
import jax.numpy as jnp
import numpy as np
from jax import lax
from jax.experimental import pallas as pl
from jax.experimental.pallas import tpu as pltpu

F32 = jnp.float32
BF16 = jnp.bfloat16

D_MODEL = 1024
HEADS = 4
HEAD_DIM = 128
MLA_ROPE = 64
Q_LORA = 384
KV_LORA = 256
ROPE_THETA = 10000.0
GDN_CONV = 4
GDN_CHUNK = 64
GDN_QKV = 3 * HEADS * HEAD_DIM
D_MIX = 3 * HEADS * HEAD_DIM
IN_SPLITS = (Q_LORA, KV_LORA, MLA_ROPE, GDN_QKV, HEADS, HEADS, HEADS * HEAD_DIM, D_MIX)
EPS = 1e-6
MLA_SCALE = (HEAD_DIM + MLA_ROPE) ** -0.5
LOG2E = 1.4426950408889634
MEM_SCALE = HEAD_DIM ** -0.5
GDN_QSCALE = HEAD_DIM ** -0.5

LANES = 128
SUBLANES = 8
QK_WIDTH = 2 * LANES
MLA_STRIP = 64

C_CQ = 0
C_CKV = C_CQ + Q_LORA
C_KR = C_CKV + KV_LORA
C_AB = C_KR + LANES
C_MQ = C_AB + LANES
C_GATE = C_MQ + HEADS * HEAD_DIM
C_END = C_GATE + D_MIX

GDN_GROUP = 2 * LANES
GDN_GROUPS = GDN_QKV // GDN_GROUP

VMEM_LIMIT = 56 * 1024 * 1024


def _dot(a, b, precision=None):
    return jnp.dot(a, b, preferred_element_type=F32, precision=precision)


def _dot_nt(a, b):
    return lax.dot_general(a, b, (((1,), (1,)), ((), ())), preferred_element_type=F32)


def _dot_tn(a, b):
    return lax.dot_general(a, b, (((0,), (0,)), ((), ())), preferred_element_type=F32)


def _rms(t, gain):
    return t * lax.rsqrt(jnp.mean(t * t, axis=-1, keepdims=True) + EPS) * gain


def _sigmoid(t):
    return 1.0 / (1.0 + jnp.exp(-t))


def _silu(t):
    half = 0.5 * t
    return half + half * jnp.tanh(half)


def _memkv_kernel(mem_ref, gain_ref, w_ref, mk_out, mv_out):
    nb, m, d = mem_ref.shape
    hm = _rms(mem_ref[...].reshape(nb * m, d), gain_ref[...]).astype(BF16)
    kv = _dot(hm, w_ref[...])
    half = HEADS * HEAD_DIM
    mk_out[...] = kv[:, :half].astype(BF16).reshape(nb, m, half)
    mv_out[...] = kv[:, half:].astype(BF16).reshape(nb, m, half)


def _pack_kernel(w_ref, wall_out, wg_out):
    o = [int(v) for v in np.cumsum((0,) + IN_SPLITS)]
    rows = w_ref.shape[1]
    lane = lax.broadcasted_iota(jnp.int32, (rows, LANES), 1)
    half = MLA_ROPE // 2
    xk = w_ref[0, :, o[2]:o[2] + LANES]
    kr2 = jnp.where(lane < MLA_ROPE, xk,
                    jnp.where(lane < MLA_ROPE + half, pltpu.roll(xk, half, 1), pltpu.roll(xk, LANES - half, 1)))
    xab = w_ref[0, :, o[4]:o[4] + LANES]
    wall_out[:, C_CQ:C_KR] = w_ref[0, :, o[0]:o[2]].astype(BF16)
    wall_out[:, C_KR:C_AB] = kr2.astype(BF16)
    wall_out[:, C_AB:C_MQ] = jnp.where(lane < 2 * HEADS, xab, 0.0).astype(BF16)
    wall_out[:, C_MQ:C_GATE] = w_ref[0, :, o[6]:o[7]].astype(BF16)
    wall_out[:, C_GATE:C_END] = w_ref[0, :, o[7]:o[8]].astype(BF16)
    for g in range(GDN_GROUPS):
        wg_out[g] = w_ref[0, :, o[3] + GDN_GROUP * g:o[3] + GDN_GROUP * (g + 1)].astype(BF16)


def _rope_kernel(pos_ref, invf_ref, cos_out, sin_out):
    ang = pos_ref[...].astype(F32) * invf_ref[...]
    cos_out[...] = jnp.cos(ang)
    sin_out[...] = jnp.sin(ang)


def _proj_kernel(x_ref, cos_ref, sin_ref, nin_ref, w_ref, wg_ref, qan_ref, wqb_ref, kvan_ref, wkk_ref, wkv_ref,
                 conv_ref, alog_ref, dtb_ref, mk_ref, mv_ref,
                 q_out, k_out, v_out, gqkv_out, gg_out, ggt_out, om_out, gate_out,
                 cbuf, *, ts):
    j = pl.program_id(1)
    hd = HEADS * HEAD_DIM
    hb = _rms(x_ref[0], nin_ref[...]).astype(BF16)

    def proj(lo, hi):
        return _dot(hb, w_ref[:, lo:hi])

    @pl.when(j == 0)
    def _():
        cbuf[:, 0:SUBLANES, :] = jnp.zeros((GDN_GROUPS, SUBLANES, GDN_GROUP), F32)

    @pl.when(j > 0)
    def _():
        cbuf[:, 0:SUBLANES, :] = cbuf[:, ts:ts + SUBLANES, :]

    def gdn_dot(g):
        cbuf[g, SUBLANES:ts + SUBLANES, :] = _dot(hb, wg_ref[g])

    def gdn_epilogue(g):
        taps = conv_ref[g]
        for sub in range(GDN_GROUP // LANES):
            cols = slice(LANES * sub, LANES * (sub + 1))
            blk = cbuf[g, :, cols]
            acc = taps[GDN_CONV - 1:GDN_CONV, cols] * blk[SUBLANES:]
            for back in range(1, GDN_CONV):
                acc = acc + taps[GDN_CONV - 1 - back:GDN_CONV - back, cols] * pltpu.roll(blk, back, 0)[SUBLANES:]
            y = _silu(acc)
            if g < 2 * (GDN_GROUPS // 3):
                inv = lax.rsqrt(jnp.sum(y * y, axis=-1, keepdims=True) + EPS)
                y = y * (inv * GDN_QSCALE if g < GDN_GROUPS // 3 else inv)
            gqkv_out[0, g, :, cols] = y.astype(BF16)

    narrow = proj(C_CQ, C_MQ)
    cq = narrow[:, C_CQ:C_CKV]
    ckv = narrow[:, C_CKV:C_KR]
    kr_raw = narrow[:, C_KR:C_AB]
    ab = narrow[:, C_AB:C_MQ]
    cqn = _rms(cq, qan_ref[...]).astype(BF16)
    gdn_dot(0)
    ckvn = _rms(ckv, kvan_ref[...]).astype(BF16)
    qf = _dot(cqn, wqb_ref[...])

    c32, s32 = cos_ref[0].T, sin_ref[0].T
    zpad = jnp.zeros((ts, LANES - MLA_ROPE), F32)
    cosm = jnp.concatenate([c32, c32, zpad], axis=1)
    sinm = jnp.concatenate([-s32, s32, zpad], axis=1)

    def rope(r):
        return r * cosm + pltpu.roll(r, MLA_ROPE, 1) * sinm

    gdn_dot(1)
    gdn_epilogue(0)
    kn = _dot(ckvn, wkk_ref[...])
    vv = _dot(ckvn, wkv_ref[...])
    qscale = MLA_SCALE * LOG2E
    for h in range(HEADS):
        lo = QK_WIDTH * h
        q_out[0, :, lo:lo + LANES] = (qf[:, lo:lo + LANES] * qscale).astype(BF16)
        q_out[0, :, lo + LANES:lo + QK_WIDTH] = (rope(qf[:, lo + LANES:lo + QK_WIDTH]) * qscale).astype(BF16)
    gdn_dot(2)
    gdn_epilogue(1)
    mq = proj(C_MQ, C_GATE)
    v_out[0] = vv.astype(BF16)
    kr = rope(kr_raw).astype(BF16)
    for h in range(HEADS):
        lo = QK_WIDTH * h
        k_out[0, :, lo:lo + LANES] = kn[:, LANES * h:LANES * (h + 1)].astype(BF16)
        k_out[0, :, lo + LANES:lo + QK_WIDTH] = kr
    gdn_dot(3)
    gdn_epilogue(2)

    mscale = MEM_SCALE * LOG2E
    sc, pp, ll, oo = {}, {}, {}, {}

    def mem_qk(h):
        cols = slice(HEAD_DIM * h, HEAD_DIM * (h + 1))
        sc[h] = _dot_nt((mq[:, cols] * mscale).astype(BF16), mk_ref[0, :, cols])

    def mem_softmax(h):
        p = jnp.exp2(sc[h] - jnp.max(sc[h], axis=-1, keepdims=True))
        ll[h] = jnp.sum(p, axis=-1, keepdims=True)
        pp[h] = p.astype(BF16)

    def mem_pv(h):
        oo[h] = _dot(pp[h], mv_ref[0, :, HEAD_DIM * h:HEAD_DIM * (h + 1)])

    mem_qk(0)
    z = ab + dtb_ref[...]
    softplus = jnp.maximum(z, 0.0) + jnp.log1p(jnp.exp(-jnp.abs(z)))
    gcum = -jnp.exp(alog_ref[...]) * softplus
    lane = lax.broadcasted_iota(jnp.int32, ab.shape, 1)
    pos_in_chunk = lax.broadcasted_iota(jnp.int32, ab.shape, 0) % GDN_CHUNK
    shift = 1
    while shift < GDN_CHUNK:
        gcum = gcum + jnp.where(pos_in_chunk >= shift, pltpu.roll(gcum, shift, 0), 0.0)
        shift *= 2
    gg_out[0] = jnp.where(lane < HEADS, gcum, _sigmoid(ab))
    ggt_out[0] = gcum.T[0:SUBLANES, :]

    gdn_dot(4)
    gdn_epilogue(3)
    mem_qk(1)
    mem_softmax(0)
    gdn_dot(5)
    gdn_epilogue(4)
    mem_qk(2)
    mem_pv(0)
    mem_softmax(1)
    gate_out[0, :, 0:hd] = proj(C_GATE, C_GATE + hd).astype(BF16)
    gdn_epilogue(5)
    mem_qk(3)
    mem_pv(1)
    mem_softmax(2)
    gate_out[0, :, hd:2 * hd] = proj(C_GATE + hd, C_GATE + 2 * hd).astype(BF16)
    mem_pv(2)
    mem_softmax(3)
    gmem = _silu(proj(C_GATE + 2 * hd, C_END))
    mem_pv(3)
    for h in range(HEADS):
        cols = slice(HEAD_DIM * h, HEAD_DIM * (h + 1))
        om_out[0, :, cols] = (oo[h] / ll[h] * gmem[:, cols]).astype(BF16)


def _mla_kernel(q_ref, k_ref, v_ref, o_ref, s_scr, p_scr, m_scr, l_scr, a_scr, acc_scr, *, tq):
    i = pl.program_id(1)
    heads = range(HEADS)
    strip = MLA_STRIP

    def step(jk, diagonal):
        start = pl.multiple_of(jk * tq, tq)
        if diagonal:
            half = tq // 2
            items = [(h, r0, half, r0 + half) for h in heads for r0 in (0, half)]
        else:
            items = [(h, 0, tq, tq) for h in heads]

        def scores(item):
            h, r0, nr, nk = item
            s_scr[h, r0:r0 + nr, 0:nk] = _dot_nt(q_ref[0, r0:r0 + nr, QK_WIDTH * h:QK_WIDTH * (h + 1)],
                                                 k_ref[0, pl.ds(start, nk), QK_WIDTH * h:QK_WIDTH * (h + 1)])

        def load_strip(item, r):
            h, r0, nr, nk = item
            s = s_scr[h, r0 + strip * r:r0 + strip * (r + 1), 0:nk]
            if diagonal:
                row = lax.broadcasted_iota(jnp.int32, s.shape, 0) + (r0 + strip * r)
                col = lax.broadcasted_iota(jnp.int32, s.shape, 1)
                s = jnp.where(row >= col, s, -jnp.inf)
            return s

        def lane_tiles(t):
            return [t[:, LANES * g:LANES * (g + 1)] for g in range(t.shape[1] // LANES)]

        def softmax(item):
            h, r0, nr, nk = item
            strips = range(nr // strip)
            rows = [slice(r0 + strip * r, r0 + strip * (r + 1)) for r in strips]
            part = [functools.reduce(jnp.maximum, lane_tiles(load_strip(item, r))) for r in strips]
            peak = [jnp.broadcast_to(jnp.max(t, axis=-1, keepdims=True), (strip, LANES)) for t in part]
            if diagonal:
                m_new = peak
            else:
                m_old = [m_scr[h, rows[r], :] for r in strips]
                m_new = [jnp.maximum(m_old[r], peak[r]) for r in strips]
                alpha = [jnp.exp2(m_old[r] - m_new[r]) for r in strips]
                for r in strips:
                    a_scr[h, rows[r], :] = alpha[r]
            for r in strips:
                m_scr[h, rows[r], :] = m_new[r]
            part = []
            for r in strips:
                p = [jnp.exp2(t - m_new[r]) for t in lane_tiles(load_strip(item, r))]
                for g, t in enumerate(p):
                    p_scr[h, rows[r], LANES * g:LANES * (g + 1)] = t.astype(BF16)
                part.append(functools.reduce(jnp.add, p))
            total = [jnp.broadcast_to(jnp.sum(t, axis=-1, keepdims=True), (strip, LANES)) for t in part]
            for r in strips:
                l_scr[h, rows[r], :] = total[r] if diagonal else alpha[r] * l_scr[h, rows[r], :] + total[r]

        def values(item):
            h, r0, nr, nk = item
            pv = _dot(p_scr[h, r0:r0 + nr, 0:nk], v_ref[0, pl.ds(start, nk), HEAD_DIM * h:HEAD_DIM * (h + 1)])
            if diagonal:
                acc_scr[h, r0:r0 + nr, :] = pv
            else:
                acc_scr[h, r0:r0 + nr, :] = a_scr[h, r0:r0 + nr, :] * acc_scr[h, r0:r0 + nr, :] + pv

        for t in range(len(items) + 2):
            if t < len(items):
                scores(items[t])
            if 0 <= t - 1 < len(items):
                softmax(items[t - 1])
            if 0 <= t - 2 < len(items):
                values(items[t - 2])

    def body(jk, carry):
        step(jk, False)
        return carry

    step(i, True)
    lax.fori_loop(0, i, body, 0)
    for h in heads:
        o_ref[0, :, HEAD_DIM * h:HEAD_DIM * (h + 1)] = (acc_scr[h] / l_scr[h]).astype(BF16)


def _pair_blockdiag(t, lo_half):
    return jnp.concatenate([jnp.where(lo_half, t, 0.0), jnp.where(lo_half, 0.0, t)], axis=0).astype(BF16)


def _gdn_merge_kernel(qkv_ref, gg_ref, ggt_ref, gn_ref, x_ref, omla_ref, om_ref, gate_ref, wout_ref,
                      nf_ref, out_ref, s_ref, og_scr, *, ts):
    j = pl.program_id(1)
    hd = HEADS * HEAD_DIM
    ntile = 2 * LANES

    @pl.when(j == 0)
    def _():
        s_ref[...] = jnp.zeros(s_ref.shape, F32)

    c = GDN_CHUNK
    c2 = 2 * c
    npairs = ts // c2
    row = lax.broadcasted_iota(jnp.int32, (c, c2), 0)
    lane = lax.broadcasted_iota(jnp.int32, (c, c2), 1)
    lo_half = lane < c
    col = jnp.where(lo_half, lane, lane - c)
    incl = row >= col
    strict = row > col
    eye = jnp.where(row == col, 1.0, 0.0)
    zeros_k = jnp.zeros((c, HEAD_DIM), BF16)
    units = [(p2, h) for p2 in range(npairs) for h in range(HEADS)]

    pre = {}
    for (p2, h) in units:
        r2 = slice(c2 * p2, c2 * (p2 + 1))
        cols = slice(HEAD_DIM * h, HEAD_DIM * (h + 1))
        gg = gg_ref[0, r2, :]
        gcol = gg[:, h:h + 1]
        beta = gg[:, HEADS + h:HEADS + h + 1]
        grow = ggt_ref[0, h:h + 1, r2]
        glast = (gg[c - 1:c, h:h + 1], gg[c2 - 1:c2, h:h + 1])
        glast_col = jnp.concatenate([jnp.broadcast_to(glast[0], (c, 1)), jnp.broadcast_to(glast[1], (c, 1))], axis=0)
        per_group = GDN_GROUP // HEAD_DIM
        gcols = slice(HEAD_DIM * (h % per_group), HEAD_DIM * (h % per_group + 1))
        q2 = qkv_ref[0, h // per_group, r2, gcols]
        k2 = qkv_ref[0, GDN_GROUPS // 3 + h // per_group, r2, gcols]
        kf = k2.astype(F32)
        kbeta = kf * beta
        kbeta_b = kbeta.astype(BF16)
        vbeta_b = (qkv_ref[0, 2 * (GDN_GROUPS // 3) + h // per_group, r2, gcols].astype(F32) * beta).astype(BF16)
        eg = jnp.exp(gcol)
        lhs = jnp.concatenate([jnp.concatenate([kbeta_b[:c], kbeta_b[c:]], axis=1),
                               jnp.concatenate([q2[:c], q2[c:]], axis=1)], axis=0)
        rhs = jnp.concatenate([jnp.concatenate([k2[:c], zeros_k], axis=1),
                               jnp.concatenate([zeros_k, k2[c:]], axis=1)], axis=0)
        gcol_pair = jnp.where(lo_half, gcol[:c], gcol[c:])
        pre[(p2, h)] = dict(
            lhs=lhs, rhs=rhs, glast=glast,
            vk=jnp.concatenate([vbeta_b, (kbeta * eg).astype(BF16)], axis=1),
            qg=(q2.astype(F32) * eg).astype(BF16),
            kdec=(kf * jnp.exp(glast_col - gcol)).astype(BF16),
            decay=jnp.exp(jnp.where(incl, gcol_pair - grow, -jnp.inf)))

    kq = {u: _dot_nt(pre[u]["lhs"], pre[u]["rhs"]) for u in units}
    a_pair = {u: kq[u][c:] * pre[u]["decay"] for u in units}
    m = {u: -jnp.where(strict, kq[u][:c] * pre[u]["decay"], 0.0) for u in units}

    p = {u: eye + m[u] for u in units}
    m = {u: _dot(m[u].astype(BF16), _pair_blockdiag(m[u], lo_half)) for u in units}
    for _ in range(int(np.log2(c)) - 2):
        pm = {u: _dot(jnp.concatenate([p[u], m[u]], axis=0).astype(BF16), _pair_blockdiag(m[u], lo_half))
              for u in units}
        p = {u: p[u] + pm[u][:c] for u in units}
        m = {u: pm[u][c:] for u in units}
    pm = {u: _dot(p[u].astype(BF16), _pair_blockdiag(m[u], lo_half)) for u in units}
    t_pair = {u: p[u] + pm[u] for u in units}

    uw = {u: _dot(_pair_blockdiag(t_pair[u], lo_half), pre[u]["vk"]) for u in units}
    a_chunks = {u: (a_pair[u][:, :c].astype(BF16), pltpu.roll(a_pair[u], c, 1)[:, :c].astype(BF16)) for u in units}

    a_mla = (omla_ref[0].astype(F32) * _silu(gate_ref[0, :, 0:hd].astype(F32))).astype(BF16)
    free_parts = [(a_mla, 0), (om_ref[0], 2 * hd)]
    pieces = [(part, n) for part in range(len(free_parts)) for n in range(D_MODEL // ntile)]
    partial = {}

    heads = range(HEADS)
    state = [s_ref[h] for h in heads]
    nsteps = 2 * npairs
    for step in range(nsteps):
        p2, ci = divmod(step, 2)
        rc = slice(c * ci, c * (ci + 1))
        rows = slice(c * step, c * (step + 1))
        ws_qs = [_dot(jnp.concatenate([uw[(p2, h)][rc, HEAD_DIM:].astype(BF16), pre[(p2, h)]["qg"][rc]], axis=0),
                      state[h].astype(BF16)) for h in heads]
        for part, n in pieces[len(pieces) * step // nsteps:len(pieces) * (step + 1) // nsteps]:
            operand, w_lo = free_parts[part]
            partial[(part, n)] = _dot(operand, wout_ref[w_lo:w_lo + hd, ntile * n:ntile * (n + 1)])
        v_new = [(uw[(p2, h)][rc, :HEAD_DIM] - ws_qs[h][:c]).astype(BF16) for h in heads]
        o_intra = [_dot(a_chunks[(p2, h)][ci], v_new[h]) for h in heads]
        ds = [_dot_tn(pre[(p2, h)]["kdec"][rc], v_new[h]) for h in heads]
        state = [state[h] * jnp.exp(pre[(p2, h)]["glast"][ci]) + ds[h] for h in heads]
        for h in heads:
            cols = slice(HEAD_DIM * h, HEAD_DIM * (h + 1))
            gate = _silu(gate_ref[0, rows, hd + HEAD_DIM * h:hd + HEAD_DIM * (h + 1)].astype(F32))
            og_scr[rows, cols] = (_rms(ws_qs[h][c:] + o_intra[h], gn_ref[...]) * gate).astype(BF16)
    for h in heads:
        s_ref[h] = state[h]

    og = og_scr[...]
    ys = []
    for n in range(D_MODEL // ntile):
        ncols = slice(ntile * n, ntile * (n + 1))
        acc = _dot(og, wout_ref[hd:2 * hd, ncols])
        for part in range(len(free_parts)):
            acc = acc + partial[(part, n)]
        ys.append(x_ref[0, :, ncols] + acc)
    ssq = functools.reduce(jnp.add, [jnp.sum(y * y, axis=-1, keepdims=True) for y in ys])
    inv = lax.rsqrt(ssq * (1.0 / D_MODEL) + EPS)
    for n, y in enumerate(ys):
        ncols = slice(ntile * n, ntile * (n + 1))
        out_ref[0, :, ncols] = y * inv * nf_ref[:, ncols]


def _const_spec(shape):
    nd = len(shape)
    return pl.BlockSpec(shape, lambda *_: (0,) * nd)


def _pack_weights(w_in_all, layer, w_q_b, w_kv_b):
    tr = _tile(D_MODEL, 256)
    w_all, w_groups = pl.pallas_call(
        _pack_kernel,
        grid=(D_MODEL // tr,),
        in_specs=[pl.BlockSpec((1, tr, w_in_all.shape[2]), lambda r: (layer, r, 0))],
        out_specs=[pl.BlockSpec((tr, C_END), lambda r: (r, 0)),
                   pl.BlockSpec((GDN_GROUPS, tr, GDN_GROUP), lambda r: (0, r, 0))],
        out_shape=[jax.ShapeDtypeStruct((D_MODEL, C_END), BF16),
                   jax.ShapeDtypeStruct((GDN_GROUPS, D_MODEL, GDN_GROUP), BF16)],
        compiler_params=pltpu.CompilerParams(dimension_semantics=("arbitrary",), vmem_limit_bytes=VMEM_LIMIT),
        name="pack",
    )(w_in_all)
    half = MLA_ROPE // 2
    wq = w_q_b.reshape(Q_LORA, HEADS, HEAD_DIM + MLA_ROPE)
    nope, ropec = wq[..., :HEAD_DIM], wq[..., HEAD_DIM:]
    wqb = jnp.concatenate([nope, ropec, ropec[..., half:], ropec[..., :half]], axis=-1)
    wqb = wqb.reshape(Q_LORA, HEADS * QK_WIDTH).astype(BF16)
    wkv = w_kv_b.reshape(KV_LORA, HEADS, 2 * HEAD_DIM)
    wkk = wkv[..., :HEAD_DIM].reshape(KV_LORA, HEADS * HEAD_DIM).astype(BF16)
    wkvv = wkv[..., HEAD_DIM:].reshape(KV_LORA, HEADS * HEAD_DIM).astype(BF16)
    return w_all, w_groups, wqb, wkk, wkvv


def _lane_row(vec):
    return jnp.zeros((1, LANES), F32).at[0, :vec.shape[0]].set(vec.astype(F32))


def _layer(x, mk, mv, cos, sin, norm_in, w_in_all, layer, q_a_norm, w_q_b, kv_a_norm, w_kv_b, gdn_conv,
           gdn_a_log, gdn_dt_bias, gdn_norm, w_out, out_gain, *, ts_proj, ts_gdn, tq):
    B, S, D = x.shape
    hd = HEADS * HEAD_DIM
    w_all, w_groups, wqb, wkk, wkvv = _pack_weights(w_in_all, layer, w_q_b, w_kv_b)
    conv_groups = gdn_conv.reshape(GDN_CONV, GDN_GROUPS, GDN_GROUP).transpose(1, 0, 2)
    M = mk.shape[1]
    arb2 = pltpu.CompilerParams(dimension_semantics=("arbitrary", "arbitrary"), vmem_limit_bytes=VMEM_LIMIT)

    def row_spec(width, ts):
        return pl.BlockSpec((1, ts, width), lambda b, j: (b, j, 0))

    def bs_shape(width, dtype=BF16):
        return jax.ShapeDtypeStruct((B, S, width), dtype)

    tile_spec = pl.BlockSpec((1, MLA_ROPE // 2, ts_proj), lambda b, j: (b * (S // ts_proj) + j, 0, 0))

    q, k, v, gqkv, gg, ggt, om, gate = pl.pallas_call(
        functools.partial(_proj_kernel, ts=ts_proj),
        grid=(B, S // ts_proj),
        in_specs=[row_spec(D, ts_proj), tile_spec, tile_spec,
                  _const_spec((1, D)), _const_spec(w_all.shape), _const_spec(w_groups.shape),
                  _const_spec((1, Q_LORA)), _const_spec(wqb.shape), _const_spec((1, KV_LORA)),
                  _const_spec(wkk.shape), _const_spec(wkvv.shape), _const_spec(conv_groups.shape),
                  _const_spec((1, LANES)), _const_spec((1, LANES)),
                  pl.BlockSpec((1, M, hd), lambda b, j: (b, 0, 0)), pl.BlockSpec((1, M, hd), lambda b, j: (b, 0, 0))],
        out_specs=[row_spec(HEADS * QK_WIDTH, ts_proj), row_spec(HEADS * QK_WIDTH, ts_proj), row_spec(hd, ts_proj),
                   pl.BlockSpec((1, GDN_GROUPS, ts_proj, GDN_GROUP), lambda b, j: (b, 0, j, 0)),
                   row_spec(LANES, ts_proj), pl.BlockSpec((1, SUBLANES, ts_proj), lambda b, j: (b, 0, j)),
                   row_spec(hd, ts_proj), row_spec(2 * hd, ts_proj)],
        out_shape=[bs_shape(HEADS * QK_WIDTH), bs_shape(HEADS * QK_WIDTH), bs_shape(hd),
                   jax.ShapeDtypeStruct((B, GDN_GROUPS, S, GDN_GROUP), BF16),
                   bs_shape(LANES, F32), jax.ShapeDtypeStruct((B, SUBLANES, S), F32),
                   bs_shape(hd), bs_shape(2 * hd)],
        scratch_shapes=[pltpu.VMEM((GDN_GROUPS, ts_proj + SUBLANES, GDN_GROUP), F32)],
        compiler_params=arb2,
        name="proj",
    )(x, cos, sin, norm_in.reshape(1, D), w_all, w_groups, q_a_norm.reshape(1, Q_LORA), wqb,
      kv_a_norm.reshape(1, KV_LORA), wkk, wkvv, conv_groups, _lane_row(gdn_a_log), _lane_row(gdn_dt_bias), mk, mv)

    o_mla = pl.pallas_call(
        functools.partial(_mla_kernel, tq=tq),
        grid=(B, S // tq),
        in_specs=[pl.BlockSpec((1, tq, HEADS * QK_WIDTH), lambda b, i: (b, i, 0)),
                  pl.BlockSpec((1, S, HEADS * QK_WIDTH), lambda b, i: (b, 0, 0)),
                  pl.BlockSpec((1, S, hd), lambda b, i: (b, 0, 0))],
        out_specs=pl.BlockSpec((1, tq, hd), lambda b, i: (b, i, 0)),
        out_shape=bs_shape(hd),
        scratch_shapes=[pltpu.VMEM((HEADS, tq, tq), F32), pltpu.VMEM((HEADS, tq, tq), BF16),
                        pltpu.VMEM((HEADS, tq, LANES), F32), pltpu.VMEM((HEADS, tq, LANES), F32),
                        pltpu.VMEM((HEADS, tq, LANES), F32), pltpu.VMEM((HEADS, tq, HEAD_DIM), F32)],
        compiler_params=arb2,
        name="mla",
    )(q, k, v)

    return pl.pallas_call(
        functools.partial(_gdn_merge_kernel, ts=ts_gdn),
        grid=(B, S // ts_gdn),
        in_specs=[pl.BlockSpec((1, GDN_GROUPS, ts_gdn, GDN_GROUP), lambda b, j: (b, 0, j, 0)), row_spec(LANES, ts_gdn),
                  pl.BlockSpec((1, SUBLANES, ts_gdn), lambda b, j: (b, 0, j)), _const_spec((1, HEAD_DIM)),
                  row_spec(D, ts_gdn), row_spec(hd, ts_gdn), row_spec(hd, ts_gdn), row_spec(2 * hd, ts_gdn),
                  _const_spec((D_MIX, D)), _const_spec((1, D))],
        out_specs=row_spec(D, ts_gdn),
        out_shape=bs_shape(D, F32),
        scratch_shapes=[pltpu.VMEM((HEADS, HEAD_DIM, HEAD_DIM), F32), pltpu.VMEM((ts_gdn, hd), BF16)],
        compiler_params=arb2,
        name="gdn_merge",
    )(gqkv, gg, ggt, gdn_norm.reshape(1, HEAD_DIM), x, o_mla, om, gate, w_out.astype(BF16),
      out_gain.reshape(1, D))


def _tile(n, pref):
    return pref if n % pref == 0 else n


def kernel(x, mem, positions, norm_in, w_in, q_a_norm, w_q_b, kv_a_norm, w_kv_b, gdn_conv, gdn_a_log,
           gdn_dt_bias, gdn_norm, mem_norm, w_mem_kv, w_out, norm_final):
    B, S, D = x.shape
    M = mem.shape[1]
    depth = norm_in.shape[0]
    assert depth == 1, "the final norm is fused into the single layer's last kernel"
    hd = HEADS * HEAD_DIM
    half = MLA_ROPE // 2
    inv_freq = 1.0 / (ROPE_THETA ** (jnp.arange(half, dtype=F32) / half))
    ts_proj = _tile(S, 512)
    n_tiles = B * S // ts_proj
    nt = _tile(n_tiles, 8)
    cos, sin = pl.pallas_call(
        _rope_kernel,
        grid=(n_tiles // nt,),
        in_specs=[pl.BlockSpec((nt, 1, ts_proj), lambda r: (r, 0, 0)), _const_spec((1, half, 1))],
        out_specs=[pl.BlockSpec((nt, half, ts_proj), lambda r: (r, 0, 0))] * 2,
        out_shape=[jax.ShapeDtypeStruct((n_tiles, half, ts_proj), F32)] * 2,
        compiler_params=pltpu.CompilerParams(dimension_semantics=("arbitrary",), vmem_limit_bytes=VMEM_LIMIT),
        name="rope",
    )(positions.reshape(n_tiles, 1, ts_proj), inv_freq.reshape(1, half, 1))
    l = 0
    nb_mem = _tile(B, 4)
    mk, mv = pl.pallas_call(
        _memkv_kernel,
        grid=(B // nb_mem,),
        in_specs=[pl.BlockSpec((nb_mem, M, D), lambda b: (b, 0, 0)), _const_spec((1, D)), _const_spec((D, 2 * hd))],
        out_specs=[pl.BlockSpec((nb_mem, M, hd), lambda b: (b, 0, 0))] * 2,
        out_shape=[jax.ShapeDtypeStruct((B, M, hd), BF16)] * 2,
        compiler_params=pltpu.CompilerParams(dimension_semantics=("arbitrary",), vmem_limit_bytes=VMEM_LIMIT),
        name="memkv",
    )(mem, mem_norm[l].reshape(1, D), w_mem_kv[l].astype(BF16))
    return _layer(x, mk, mv, cos, sin, norm_in[l], w_in, l, q_a_norm[l], w_q_b[l], kv_a_norm[l], w_kv_b[l],
                  gdn_conv[l], gdn_a_log[l], gdn_dt_bias[l], gdn_norm[l], w_out[l], norm_final,
                  ts_proj=ts_proj, ts_gdn=_tile(S, 512), tq=_tile(S, 512))
```

```python
import functools

import jax
import jax.numpy as jnp
import numpy as np
from jax import lax
from jax.experimental import pallas as pl
from jax.experimental.pallas import tpu as pltpu

F32 = jnp.float32
BF16 = jnp.bfloat16

D_MODEL = 1024
HEADS = 4
HEAD_DIM = 128
MLA_ROPE = 64
Q_LORA = 384
KV_LORA = 256
ROPE_THETA = 10000.0
GDN_CONV = 4
GDN_CHUNK = 64
GDN_QKV = 3 * HEADS * HEAD_DIM
D_MIX = 3 * HEADS * HEAD_DIM
IN_SPLITS = (Q_LORA, KV_LORA, MLA_ROPE, GDN_QKV, HEADS, HEADS, HEADS * HEAD_DIM, D_MIX)
EPS = 1e-6
MLA_SCALE = (HEAD_DIM + MLA_ROPE) ** -0.5
LOG2E = 1.4426950408889634
MEM_SCALE = HEAD_DIM ** -0.5
GDN_QSCALE = HEAD_DIM ** -0.5

LANES = 128
SUBLANES = 8
QK_WIDTH = 2 * LANES
MLA_STRIP = 64

C_CQ = 0
C_CKV = C_CQ + Q_LORA
C_KR = C_CKV + KV_LORA
C_AB = C_KR + LANES
C_MQ = C_AB + LANES
C_GATE = C_MQ + HEADS * HEAD_DIM
C_END = C_GATE + D_MIX

GDN_GROUP = 2 * LANES
GDN_GROUPS = GDN_QKV // GDN_GROUP

V7X_VMEM_BYTES = 64 * 1024 * 1024
VMEM_LIMIT = V7X_VMEM_BYTES * 7 // 8


def _dot(a, b, precision=None):
    return jnp.dot(a, b, preferred_element_type=F32, precision=precision)


def _dot_nt(a, b):
    return lax.dot_general(a, b, (((1,), (1,)), ((), ())), preferred_element_type=F32)


def _dot_tn(a, b):
    return lax.dot_general(a, b, (((0,), (0,)), ((), ())), preferred_element_type=F32)


def _rms(t, gain):
    return t * lax.rsqrt(jnp.mean(t * t, axis=-1, keepdims=True) + EPS) * gain


def _sigmoid(t):
    return 1.0 / (1.0 + jnp.exp(-t))


def _silu(t):
    half = 0.5 * t
    return half + half * jnp.tanh(half)


def _memkv_kernel(mem_ref, gain_ref, w_ref, mk_out, mv_out):
    nb, m, d = mem_ref.shape
    hm = _rms(mem_ref[...].reshape(nb * m, d), gain_ref[...]).astype(BF16)
    kv = _dot(hm, w_ref[...])
    half = HEADS * HEAD_DIM
    mk_out[...] = kv[:, :half].astype(BF16).reshape(nb, m, half)
    mv_out[...] = kv[:, half:].astype(BF16).reshape(nb, m, half)


def _pack_kernel(w_ref, wall_out, wg_out):
    o = [int(v) for v in np.cumsum((0,) + IN_SPLITS)]
    rows = w_ref.shape[1]
    lane = lax.broadcasted_iota(jnp.int32, (rows, LANES), 1)
    half = MLA_ROPE // 2
    xk = w_ref[0, :, o[2]:o[2] + LANES]
    kr2 = jnp.where(lane < MLA_ROPE, xk,
                    jnp.where(lane < MLA_ROPE + half, pltpu.roll(xk, half, 1), pltpu.roll(xk, LANES - half, 1)))
    xab = w_ref[0, :, o[4]:o[4] + LANES]
    wall_out[:, C_CQ:C_KR] = w_ref[0, :, o[0]:o[2]].astype(BF16)
    wall_out[:, C_KR:C_AB] = kr2.astype(BF16)
    wall_out[:, C_AB:C_MQ] = jnp.where(lane < 2 * HEADS, xab, 0.0).astype(BF16)
    wall_out[:, C_MQ:C_GATE] = w_ref[0, :, o[6]:o[7]].astype(BF16)
    wall_out[:, C_GATE:C_END] = w_ref[0, :, o[7]:o[8]].astype(BF16)
    for g in range(GDN_GROUPS):
        wg_out[g] = w_ref[0, :, o[3] + GDN_GROUP * g:o[3] + GDN_GROUP * (g + 1)].astype(BF16)


def _rope_kernel(pos_ref, invf_ref, cos_out, sin_out):
    ang = pos_ref[...].astype(F32) * invf_ref[...]
    cos_out[...] = jnp.cos(ang)
    sin_out[...] = jnp.sin(ang)


def _proj_kernel(x_ref, cos_ref, sin_ref, nin_ref, w_ref, wg_ref, qan_ref, wqb_ref, kvan_ref, wkk_ref, wkv_ref,
                 conv_ref, alog_ref, dtb_ref, mk_ref, mv_ref,
                 q_out, k_out, v_out, gqkv_out, gg_out, ggt_out, om_out, gate_out,
                 cbuf, *, ts):
    j = pl.program_id(1)
    hd = HEADS * HEAD_DIM
    hb = _rms(x_ref[0], nin_ref[...]).astype(BF16)

    def proj(lo, hi):
        return _dot(hb, w_ref[:, lo:hi])

    @pl.when(j == 0)
    def _():
        cbuf[:, 0:SUBLANES, :] = jnp.zeros((GDN_GROUPS, SUBLANES, GDN_GROUP), F32)

    @pl.when(j > 0)
    def _():
        cbuf[:, 0:SUBLANES, :] = cbuf[:, ts:ts + SUBLANES, :]

    def gdn_dot(g):
        cbuf[g, SUBLANES:ts + SUBLANES, :] = _dot(hb, wg_ref[g])

    def gdn_epilogue(g):
        taps = conv_ref[g]
        for sub in range(GDN_GROUP // LANES):
            cols = slice(LANES * sub, LANES * (sub + 1))
            blk = cbuf[g, :, cols]
            acc = taps[GDN_CONV - 1:GDN_CONV, cols] * blk[SUBLANES:]
            for back in range(1, GDN_CONV):
                acc = acc + taps[GDN_CONV - 1 - back:GDN_CONV - back, cols] * pltpu.roll(blk, back, 0)[SUBLANES:]
            y = _silu(acc)
            if g < 2 * (GDN_GROUPS // 3):
                inv = lax.rsqrt(jnp.sum(y * y, axis=-1, keepdims=True) + EPS)
                y = y * (inv * GDN_QSCALE if g < GDN_GROUPS // 3 else inv)
            gqkv_out[0, g, :, cols] = y.astype(BF16)

    narrow = proj(C_CQ, C_MQ)
    cq = narrow[:, C_CQ:C_CKV]
    ckv = narrow[:, C_CKV:C_KR]
    kr_raw = narrow[:, C_KR:C_AB]
    ab = narrow[:, C_AB:C_MQ]
    cqn = _rms(cq, qan_ref[...]).astype(BF16)
    gdn_dot(0)
    ckvn = _rms(ckv, kvan_ref[...]).astype(BF16)
    qf = _dot(cqn, wqb_ref[...])

    c32, s32 = cos_ref[0].T, sin_ref[0].T
    zpad = jnp.zeros((ts, LANES - MLA_ROPE), F32)
    cosm = jnp.concatenate([c32, c32, zpad], axis=1)
    sinm = jnp.concatenate([-s32, s32, zpad], axis=1)

    def rope(r):
        return r * cosm + pltpu.roll(r, MLA_ROPE, 1) * sinm

    gdn_dot(1)
    gdn_epilogue(0)
    kn = _dot(ckvn, wkk_ref[...])
    vv = _dot(ckvn, wkv_ref[...])
    qscale = MLA_SCALE * LOG2E
    for h in range(HEADS):
        lo = QK_WIDTH * h
        q_out[0, :, lo:lo + LANES] = (qf[:, lo:lo + LANES] * qscale).astype(BF16)
        q_out[0, :, lo + LANES:lo + QK_WIDTH] = (rope(qf[:, lo + LANES:lo + QK_WIDTH]) * qscale).astype(BF16)
    gdn_dot(2)
    gdn_epilogue(1)
    mq = proj(C_MQ, C_GATE)
    v_out[0] = vv.astype(BF16)
    kr = rope(kr_raw).astype(BF16)
    for h in range(HEADS):
        lo = QK_WIDTH * h
        k_out[0, :, lo:lo + LANES] = kn[:, LANES * h:LANES * (h + 1)].astype(BF16)
        k_out[0, :, lo + LANES:lo + QK_WIDTH] = kr
    gdn_dot(3)
    gdn_epilogue(2)

    mscale = MEM_SCALE * LOG2E
    sc, pp, ll, oo = {}, {}, {}, {}

    def mem_qk(h):
        cols = slice(HEAD_DIM * h, HEAD_DIM * (h + 1))
        sc[h] = _dot_nt((mq[:, cols] * mscale).astype(BF16), mk_ref[0, :, cols])

    def mem_softmax(h):
        p = jnp.exp2(sc[h] - jnp.max(sc[h], axis=-1, keepdims=True))
        ll[h] = jnp.sum(p, axis=-1, keepdims=True)
        pp[h] = p.astype(BF16)

    def mem_pv(h):
        oo[h] = _dot(pp[h], mv_ref[0, :, HEAD_DIM * h:HEAD_DIM * (h + 1)])

    mem_qk(0)
    z = ab + dtb_ref[...]
    softplus = jnp.maximum(z, 0.0) + jnp.log1p(jnp.exp(-jnp.abs(z)))
    gcum = -jnp.exp(alog_ref[...]) * softplus
    lane = lax.broadcasted_iota(jnp.int32, ab.shape, 1)
    pos_in_chunk = lax.broadcasted_iota(jnp.int32, ab.shape, 0) % GDN_CHUNK
    shift = 1
    while shift < GDN_CHUNK:
        gcum = gcum + jnp.where(pos_in_chunk >= shift, pltpu.roll(gcum, shift, 0), 0.0)
        shift *= 2
    gg_out[0] = jnp.where(lane < HEADS, gcum, _sigmoid(ab))
    ggt_out[0] = gcum.T[0:SUBLANES, :]

    gdn_dot(4)
    gdn_epilogue(3)
    mem_qk(1)
    mem_softmax(0)
    gdn_dot(5)
    gdn_epilogue(4)
    mem_qk(2)
    mem_pv(0)
    mem_softmax(1)
    gate_out[0, :, 0:hd] = proj(C_GATE, C_GATE + hd).astype(BF16)
    gdn_epilogue(5)
    mem_qk(3)
    mem_pv(1)
    mem_softmax(2)
    gate_out[0, :, hd:2 * hd] = proj(C_GATE + hd, C_GATE + 2 * hd).astype(BF16)
    mem_pv(2)
    mem_softmax(3)
    gmem = _silu(proj(C_GATE + 2 * hd, C_END))
    mem_pv(3)
    for h in range(HEADS):
        cols = slice(HEAD_DIM * h, HEAD_DIM * (h + 1))
        om_out[0, :, cols] = (oo[h] / ll[h] * gmem[:, cols]).astype(BF16)


def _mla_kernel(q_ref, k_ref, v_ref, o_ref, s_scr, p_scr, m_scr, l_scr, a_scr, acc_scr, *, tq):
    i = pl.program_id(1)
    heads = range(HEADS)
    strip = MLA_STRIP

    def step(jk, diagonal):
        start = pl.multiple_of(jk * tq, tq)
        if diagonal:
            half = tq // 2
            items = [(h, r0, half, r0 + half) for h in heads for r0 in (0, half)]
        else:
            items = [(h, 0, tq, tq) for h in heads]

        def scores(item):
            h, r0, nr, nk = item
            s_scr[h, r0:r0 + nr, 0:nk] = _dot_nt(q_ref[0, r0:r0 + nr, QK_WIDTH * h:QK_WIDTH * (h + 1)],
                                                 k_ref[0, pl.ds(start, nk), QK_WIDTH * h:QK_WIDTH * (h + 1)])

        def load_strip(item, r):
            h, r0, nr, nk = item
            s = s_scr[h, r0 + strip * r:r0 + strip * (r + 1), 0:nk]
            if diagonal:
                row = lax.broadcasted_iota(jnp.int32, s.shape, 0) + (r0 + strip * r)
                col = lax.broadcasted_iota(jnp.int32, s.shape, 1)
                s = jnp.where(row >= col, s, -jnp.inf)
            return s

        def lane_tiles(t):
            return [t[:, LANES * g:LANES * (g + 1)] for g in range(t.shape[1] // LANES)]

        def softmax(item):
            h, r0, nr, nk = item
            strips = range(nr // strip)
            rows = [slice(r0 + strip * r, r0 + strip * (r + 1)) for r in strips]
            part = [functools.reduce(jnp.maximum, lane_tiles(load_strip(item, r))) for r in strips]
            peak = [jnp.broadcast_to(jnp.max(t, axis=-1, keepdims=True), (strip, LANES)) for t in part]
            if diagonal:
                m_new = peak
            else:
                m_old = [m_scr[h, rows[r], :] for r in strips]
                m_new = [jnp.maximum(m_old[r], peak[r]) for r in strips]
                alpha = [jnp.exp2(m_old[r] - m_new[r]) for r in strips]
                for r in strips:
                    a_scr[h, rows[r], :] = alpha[r]
            for r in strips:
                m_scr[h, rows[r], :] = m_new[r]
            part = []
            for r in strips:
                p = [jnp.exp2(t - m_new[r]) for t in lane_tiles(load_strip(item, r))]
                for g, t in enumerate(p):
                    p_scr[h, rows[r], LANES * g:LANES * (g + 1)] = t.astype(BF16)
                part.append(functools.reduce(jnp.add, p))
            total = [jnp.broadcast_to(jnp.sum(t, axis=-1, keepdims=True), (strip, LANES)) for t in part]
            for r in strips:
                l_scr[h, rows[r], :] = total[r] if diagonal else alpha[r] * l_scr[h, rows[r], :] + total[r]

        def values(item):
            h, r0, nr, nk = item
            pv = _dot(p_scr[h, r0:r0 + nr, 0:nk], v_ref[0, pl.ds(start, nk), HEAD_DIM * h:HEAD_DIM * (h + 1)])
            if diagonal:
                acc_scr[h, r0:r0 + nr, :] = pv
            else:
                acc_scr[h, r0:r0 + nr, :] = a_scr[h, r0:r0 + nr, :] * acc_scr[h, r0:r0 + nr, :] + pv

        for t in range(len(items) + 2):
            if t < len(items):
                scores(items[t])
            if 0 <= t - 1 < len(items):
                softmax(items[t - 1])
            if 0 <= t - 2 < len(items):
                values(items[t - 2])

    def body(jk, carry):
        step(jk, False)
        return carry

    step(i, True)
    lax.fori_loop(0, i, body, 0)
    for h in heads:
        o_ref[0, :, HEAD_DIM * h:HEAD_DIM * (h + 1)] = (acc_scr[h] / l_scr[h]).astype(BF16)


def _pair_blockdiag(t, lo_half):
    return jnp.concatenate([jnp.where(lo_half, t, 0.0), jnp.where(lo_half, 0.0, t)], axis=0).astype(BF16)


def _gdn_merge_kernel(qkv_ref, gg_ref, ggt_ref, gn_ref, x_ref, omla_ref, om_ref, gate_ref, wout_ref,
                      nf_ref, out_ref, s_ref, og_scr, *, ts):
    j = pl.program_id(1)
    hd = HEADS * HEAD_DIM
    ntile = 2 * LANES

    @pl.when(j == 0)
    def _():
        s_ref[...] = jnp.zeros(s_ref.shape, F32)

    c = GDN_CHUNK
    c2 = 2 * c
    npairs = ts // c2
    row = lax.broadcasted_iota(jnp.int32, (c, c2), 0)
    lane = lax.broadcasted_iota(jnp.int32, (c, c2), 1)
    lo_half = lane < c
    col = jnp.where(lo_half, lane, lane - c)
    incl = row >= col
    strict = row > col
    eye = jnp.where(row == col, 1.0, 0.0)
    zeros_k = jnp.zeros((c, HEAD_DIM), BF16)
    units = [(p2, h) for p2 in range(npairs) for h in range(HEADS)]

    pre = {}
    for (p2, h) in units:
        r2 = slice(c2 * p2, c2 * (p2 + 1))
        gg = gg_ref[0, r2, :]
        gcol = gg[:, h:h + 1]
        beta = gg[:, HEADS + h:HEADS + h + 1]
        grow = ggt_ref[0, h:h + 1, r2]
        glast = (gg[c - 1:c, h:h + 1], gg[c2 - 1:c2, h:h + 1])
        glast_col = jnp.concatenate([jnp.broadcast_to(glast[0], (c, 1)), jnp.broadcast_to(glast[1], (c, 1))], axis=0)
        per_group = GDN_GROUP // HEAD_DIM
        gcols = slice(HEAD_DIM * (h % per_group), HEAD_DIM * (h % per_group + 1))
        q2 = qkv_ref[0, h // per_group, r2, gcols]
        k2 = qkv_ref[0, GDN_GROUPS // 3 + h // per_group, r2, gcols]
        kf = k2.astype(F32)
        kbeta = kf * beta
        kbeta_b = kbeta.astype(BF16)
        vbeta_b = (qkv_ref[0, 2 * (GDN_GROUPS // 3) + h // per_group, r2, gcols].astype(F32) * beta).astype(BF16)
        eg = jnp.exp(gcol)
        lhs = jnp.concatenate([jnp.concatenate([kbeta_b[:c], kbeta_b[c:]], axis=1),
                               jnp.concatenate([q2[:c], q2[c:]], axis=1)], axis=0)
        rhs = jnp.concatenate([jnp.concatenate([k2[:c], zeros_k], axis=1),
                               jnp.concatenate([zeros_k, k2[c:]], axis=1)], axis=0)
        gcol_pair = jnp.where(lo_half, gcol[:c], gcol[c:])
        pre[(p2, h)] = dict(
            lhs=lhs, rhs=rhs, glast=glast,
            vk=jnp.concatenate([vbeta_b, (kbeta * eg).astype(BF16)], axis=1),
            qg=(q2.astype(F32) * eg).astype(BF16),
            kdec=(kf * jnp.exp(glast_col - gcol)).astype(BF16),
            decay=jnp.exp(jnp.where(incl, gcol_pair - grow, -jnp.inf)))

    kq = {u: _dot_nt(pre[u]["lhs"], pre[u]["rhs"]) for u in units}
    a_pair = {u: kq[u][c:] * pre[u]["decay"] for u in units}
    m = {u: -jnp.where(strict, kq[u][:c] * pre[u]["decay"], 0.0) for u in units}

    p = {u: eye + m[u] for u in units}
    m = {u: _dot(m[u].astype(BF16), _pair_blockdiag(m[u], lo_half)) for u in units}
    for _ in range(int(np.log2(c)) - 2):
        pm = {u: _dot(jnp.concatenate([p[u], m[u]], axis=0).astype(BF16), _pair_blockdiag(m[u], lo_half))
              for u in units}
        p = {u: p[u] + pm[u][:c] for u in units}
        m = {u: pm[u][c:] for u in units}
    pm = {u: _dot(p[u].astype(BF16), _pair_blockdiag(m[u], lo_half)) for u in units}
    t_pair = {u: p[u] + pm[u] for u in units}

    uw = {u: _dot(_pair_blockdiag(t_pair[u], lo_half), pre[u]["vk"]) for u in units}
    a_chunks = {u: (a_pair[u][:, :c].astype(BF16), pltpu.roll(a_pair[u], c, 1)[:, :c].astype(BF16)) for u in units}

    a_mla = (omla_ref[0].astype(F32) * _silu(gate_ref[0, :, 0:hd].astype(F32))).astype(BF16)
    free_parts = [(a_mla, 0), (om_ref[0], 2 * hd)]
    pieces = [(part, n) for part in range(len(free_parts)) for n in range(D_MODEL // ntile)]
    partial = {}

    heads = range(HEADS)
    state = [s_ref[h] for h in heads]
    nsteps = 2 * npairs
    for step in range(nsteps):
        p2, ci = divmod(step, 2)
        rc = slice(c * ci, c * (ci + 1))
        rows = slice(c * step, c * (step + 1))
        ws_qs = [_dot(jnp.concatenate([uw[(p2, h)][rc, HEAD_DIM:].astype(BF16), pre[(p2, h)]["qg"][rc]], axis=0),
                      state[h].astype(BF16)) for h in heads]
        for part, n in pieces[len(pieces) * step // nsteps:len(pieces) * (step + 1) // nsteps]:
            operand, w_lo = free_parts[part]
            partial[(part, n)] = _dot(operand, wout_ref[w_lo:w_lo + hd, ntile * n:ntile * (n + 1)])
        v_new = [(uw[(p2, h)][rc, :HEAD_DIM] - ws_qs[h][:c]).astype(BF16) for h in heads]
        o_intra = [_dot(a_chunks[(p2, h)][ci], v_new[h]) for h in heads]
        ds = [_dot_tn(pre[(p2, h)]["kdec"][rc], v_new[h]) for h in heads]
        state = [state[h] * jnp.exp(pre[(p2, h)]["glast"][ci]) + ds[h] for h in heads]
        for h in heads:
            cols = slice(HEAD_DIM * h, HEAD_DIM * (h + 1))
            gate = _silu(gate_ref[0, rows, hd + HEAD_DIM * h:hd + HEAD_DIM * (h + 1)].astype(F32))
            og_scr[rows, cols] = (_rms(ws_qs[h][c:] + o_intra[h], gn_ref[...]) * gate).astype(BF16)
    for h in heads:
        s_ref[h] = state[h]

    og = og_scr[...]
    ys = []
    for n in range(D_MODEL // ntile):
        ncols = slice(ntile * n, ntile * (n + 1))
        acc = _dot(og, wout_ref[hd:2 * hd, ncols])
        for part in range(len(free_parts)):
            acc = acc + partial[(part, n)]
        ys.append(x_ref[0, :, ncols] + acc)
    ssq = functools.reduce(jnp.add, [jnp.sum(y * y, axis=-1, keepdims=True) for y in ys])
    inv = lax.rsqrt(ssq * (1.0 / D_MODEL) + EPS)
    for n, y in enumerate(ys):
        ncols = slice(ntile * n, ntile * (n + 1))
        out_ref[0, :, ncols] = y * inv * nf_ref[:, ncols]


def _const_spec(shape):
    nd = len(shape)
    return pl.BlockSpec(shape, lambda *_: (0,) * nd)


def _pack_weights(w_in_all, layer, w_q_b, w_kv_b):
    tr = _tile(D_MODEL, 256)
    w_all, w_groups = pl.pallas_call(
        _pack_kernel,
        grid=(D_MODEL // tr,),
        in_specs=[pl.BlockSpec((1, tr, w_in_all.shape[2]), lambda r: (layer, r, 0))],
        out_specs=[pl.BlockSpec((tr, C_END), lambda r: (r, 0)),
                   pl.BlockSpec((GDN_GROUPS, tr, GDN_GROUP), lambda r: (0, r, 0))],
        out_shape=[jax.ShapeDtypeStruct((D_MODEL, C_END), BF16),
                   jax.ShapeDtypeStruct((GDN_GROUPS, D_MODEL, GDN_GROUP), BF16)],
        compiler_params=pltpu.CompilerParams(dimension_semantics=("arbitrary",), vmem_limit_bytes=VMEM_LIMIT),
        name="pack",
    )(w_in_all)
    half = MLA_ROPE // 2
    wq = w_q_b.reshape(Q_LORA, HEADS, HEAD_DIM + MLA_ROPE)
    nope, ropec = wq[..., :HEAD_DIM], wq[..., HEAD_DIM:]
    wqb = jnp.concatenate([nope, ropec, ropec[..., half:], ropec[..., :half]], axis=-1)
    wqb = wqb.reshape(Q_LORA, HEADS * QK_WIDTH).astype(BF16)
    wkv = w_kv_b.reshape(KV_LORA, HEADS, 2 * HEAD_DIM)
    wkk = wkv[..., :HEAD_DIM].reshape(KV_LORA, HEADS * HEAD_DIM).astype(BF16)
    wkvv = wkv[..., HEAD_DIM:].reshape(KV_LORA, HEADS * HEAD_DIM).astype(BF16)
    return w_all, w_groups, wqb, wkk, wkvv


def _lane_row(vec):
    return jnp.zeros((1, LANES), F32).at[0, :vec.shape[0]].set(vec.astype(F32))


def _layer(x, mk, mv, cos, sin, norm_in, w_in_all, layer, q_a_norm, w_q_b, kv_a_norm, w_kv_b, gdn_conv,
           gdn_a_log, gdn_dt_bias, gdn_norm, w_out, out_gain, *, ts_proj, ts_gdn, tq):
    B, S, D = x.shape
    hd = HEADS * HEAD_DIM
    w_all, w_groups, wqb, wkk, wkvv = _pack_weights(w_in_all, layer, w_q_b, w_kv_b)
    conv_groups = gdn_conv.reshape(GDN_CONV, GDN_GROUPS, GDN_GROUP).transpose(1, 0, 2)
    M = mk.shape[1]
    arb2 = pltpu.CompilerParams(dimension_semantics=("arbitrary", "arbitrary"), vmem_limit_bytes=VMEM_LIMIT)

    def row_spec(width, ts):
        return pl.BlockSpec((1, ts, width), lambda b, j: (b, j, 0))

    def bs_shape(width, dtype=BF16):
        return jax.ShapeDtypeStruct((B, S, width), dtype)

    tile_spec = pl.BlockSpec((1, MLA_ROPE // 2, ts_proj), lambda b, j: (b * (S // ts_proj) + j, 0, 0))

    q, k, v, gqkv, gg, ggt, om, gate = pl.pallas_call(
        functools.partial(_proj_kernel, ts=ts_proj),
        grid=(B, S // ts_proj),
        in_specs=[row_spec(D, ts_proj), tile_spec, tile_spec,
                  _const_spec((1, D)), _const_spec(w_all.shape), _const_spec(w_groups.shape),
                  _const_spec((1, Q_LORA)), _const_spec(wqb.shape), _const_spec((1, KV_LORA)),
                  _const_spec(wkk.shape), _const_spec(wkvv.shape), _const_spec(conv_groups.shape),
                  _const_spec((1, LANES)), _const_spec((1, LANES)),
                  pl.BlockSpec((1, M, hd), lambda b, j: (b, 0, 0)), pl.BlockSpec((1, M, hd), lambda b, j: (b, 0, 0))],
        out_specs=[row_spec(HEADS * QK_WIDTH, ts_proj), row_spec(HEADS * QK_WIDTH, ts_proj), row_spec(hd, ts_proj),
                   pl.BlockSpec((1, GDN_GROUPS, ts_proj, GDN_GROUP), lambda b, j: (b, 0, j, 0)),
                   row_spec(LANES, ts_proj), pl.BlockSpec((1, SUBLANES, ts_proj), lambda b, j: (b, 0, j)),
                   row_spec(hd, ts_proj), row_spec(2 * hd, ts_proj)],
        out_shape=[bs_shape(HEADS * QK_WIDTH), bs_shape(HEADS * QK_WIDTH), bs_shape(hd),
                   jax.ShapeDtypeStruct((B, GDN_GROUPS, S, GDN_GROUP), BF16),
                   bs_shape(LANES, F32), jax.ShapeDtypeStruct((B, SUBLANES, S), F32),
                   bs_shape(hd), bs_shape(2 * hd)],
        scratch_shapes=[pltpu.VMEM((GDN_GROUPS, ts_proj + SUBLANES, GDN_GROUP), F32)],
        compiler_params=arb2,
        name="proj",
    )(x, cos, sin, norm_in.reshape(1, D), w_all, w_groups, q_a_norm.reshape(1, Q_LORA), wqb,
      kv_a_norm.reshape(1, KV_LORA), wkk, wkvv, conv_groups, _lane_row(gdn_a_log), _lane_row(gdn_dt_bias), mk, mv)

    o_mla = pl.pallas_call(
        functools.partial(_mla_kernel, tq=tq),
        grid=(B, S // tq),
        in_specs=[pl.BlockSpec((1, tq, HEADS * QK_WIDTH), lambda b, i: (b, i, 0)),
                  pl.BlockSpec((1, S, HEADS * QK_WIDTH), lambda b, i: (b, 0, 0)),
                  pl.BlockSpec((1, S, hd), lambda b, i: (b, 0, 0))],
        out_specs=pl.BlockSpec((1, tq, hd), lambda b, i: (b, i, 0)),
        out_shape=bs_shape(hd),
        scratch_shapes=[pltpu.VMEM((HEADS, tq, tq), F32), pltpu.VMEM((HEADS, tq, tq), BF16),
                        pltpu.VMEM((HEADS, tq, LANES), F32), pltpu.VMEM((HEADS, tq, LANES), F32),
                        pltpu.VMEM((HEADS, tq, LANES), F32), pltpu.VMEM((HEADS, tq, HEAD_DIM), F32)],
        compiler_params=arb2,
        name="mla",
    )(q, k, v)

    return pl.pallas_call(
        functools.partial(_gdn_merge_kernel, ts=ts_gdn),
        grid=(B, S // ts_gdn),
        in_specs=[pl.BlockSpec((1, GDN_GROUPS, ts_gdn, GDN_GROUP), lambda b, j: (b, 0, j, 0)), row_spec(LANES, ts_gdn),
                  pl.BlockSpec((1, SUBLANES, ts_gdn), lambda b, j: (b, 0, j)), _const_spec((1, HEAD_DIM)),
                  row_spec(D, ts_gdn), row_spec(hd, ts_gdn), row_spec(hd, ts_gdn), row_spec(2 * hd, ts_gdn),
                  _const_spec((D_MIX, D)), _const_spec((1, D))],
        out_specs=row_spec(D, ts_gdn),
        out_shape=bs_shape(D, F32),
        scratch_shapes=[pltpu.VMEM((HEADS, HEAD_DIM, HEAD_DIM), F32), pltpu.VMEM((ts_gdn, hd), BF16)],
        compiler_params=arb2,
        name="gdn_merge",
    )(gqkv, gg, ggt, gdn_norm.reshape(1, HEAD_DIM), x, o_mla, om, gate, w_out.astype(BF16),
      out_gain.reshape(1, D))


def _tile(n, pref):
    return pref if n % pref == 0 else n


def kernel(x, mem, positions, norm_in, w_in, q_a_norm, w_q_b, kv_a_norm, w_kv_b, gdn_conv, gdn_a_log,
           gdn_dt_bias, gdn_norm, mem_norm, w_mem_kv, w_out, norm_final):
    B, S, D = x.shape
    M = mem.shape[1]
    depth = norm_in.shape[0]
    assert depth == 1, "the final norm is fused into the single layer's last kernel"
    hd = HEADS * HEAD_DIM
    half = MLA_ROPE // 2
    inv_freq = 1.0 / (ROPE_THETA ** (jnp.arange(half, dtype=F32) / half))
    ts_proj = _tile(S, 512)
    n_tiles = B * S // ts_proj
    nt = _tile(n_tiles, 8)
    cos, sin = pl.pallas_call(
        _rope_kernel,
        grid=(n_tiles // nt,),
        in_specs=[pl.BlockSpec((nt, 1, ts_proj), lambda r: (r, 0, 0)), _const_spec((1, half, 1))],
        out_specs=[pl.BlockSpec((nt, half, ts_proj), lambda r: (r, 0, 0))] * 2,
        out_shape=[jax.ShapeDtypeStruct((n_tiles, half, ts_proj), F32)] * 2,
        compiler_params=pltpu.CompilerParams(dimension_semantics=("arbitrary",), vmem_limit_bytes=VMEM_LIMIT),
        name="rope",
    )(positions.reshape(n_tiles, 1, ts_proj), inv_freq.reshape(1, half, 1))
    l = 0
    nb_mem = _tile(B, 4)
    mk, mv = pl.pallas_call(
        _memkv_kernel,
        grid=(B // nb_mem,),
        in_specs=[pl.BlockSpec((nb_mem, M, D), lambda b: (b, 0, 0)), _const_spec((1, D)), _const_spec((D, 2 * hd))],
        out_specs=[pl.BlockSpec((nb_mem, M, hd), lambda b: (b, 0, 0))] * 2,
        out_shape=[jax.ShapeDtypeStruct((B, M, hd), BF16)] * 2,
        compiler_params=pltpu.CompilerParams(dimension_semantics=("arbitrary",), vmem_limit_bytes=VMEM_LIMIT),
        name="memkv",
    )(mem, mem_norm[l].reshape(1, D), w_mem_kv[l].astype(BF16))
    return _layer(x, mk, mv, cos, sin, norm_in[l], w_in, l, q_a_norm[l], w_q_b[l], kv_a_norm[l], w_kv_b[l],
                  gdn_conv[l], gdn_a_log[l], gdn_dt_bias[l], gdn_norm[l], w_out[l], norm_final,
                  ts_proj=ts_proj, ts_gdn=_tile(S, 512), tq=_tile(S, 512))
```

```python
import functools

import jax
import jax.numpy as jnp
import numpy as np
from jax import lax
from jax.experimental import pallas as pl
from jax.experimental.pallas import tpu as pltpu

F32 = jnp.float32
BF16 = jnp.bfloat16

D_MODEL = 1024
HEADS = 4
HEAD_DIM = 128
MLA_ROPE = 64
Q_LORA = 384
KV_LORA = 256
ROPE_THETA = 10000.0
GDN_CONV = 4
GDN_CHUNK = 64
GDN_QKV = 3 * HEADS * HEAD_DIM
D_MIX = 3 * HEADS * HEAD_DIM
IN_SPLITS = (Q_LORA, KV_LORA, MLA_ROPE, GDN_QKV, HEADS, HEADS, HEADS * HEAD_DIM, D_MIX)
EPS = 1e-6
MLA_SCALE = (HEAD_DIM + MLA_ROPE) ** -0.5
LOG2E = 1.4426950408889634
MEM_SCALE = HEAD_DIM ** -0.5
GDN_QSCALE = HEAD_DIM ** -0.5

LANES = 128
SUBLANES = 8
QK_WIDTH = 2 * LANES
MLA_STRIP = 64

C_CQ = 0
C_CKV = C_CQ + Q_LORA
C_KR = C_CKV + KV_LORA
C_AB = C_KR + LANES
C_MQ = C_AB + LANES
C_GATE = C_MQ + HEADS * HEAD_DIM
C_END = C_GATE + D_MIX

GDN_GROUP = 2 * LANES
GDN_GROUPS = GDN_QKV // GDN_GROUP

V7X_VMEM_BYTES = 64 * 1024 * 1024
VMEM_LIMIT = V7X_VMEM_BYTES * 7 // 8


def _dot(a, b, precision=None):
    return jnp.dot(a, b, preferred_element_type=F32, precision=precision)


def _dot_nt(a, b):
    return lax.dot_general(a, b, (((1,), (1,)), ((), ())), preferred_element_type=F32)


def _dot_tn(a, b):
    return lax.dot_general(a, b, (((0,), (0,)), ((), ())), preferred_element_type=F32)


def _rms(t, gain):
    return t * lax.rsqrt(jnp.mean(t * t, axis=-1, keepdims=True) + EPS) * gain


def _sigmoid(t):
    return 1.0 / (1.0 + jnp.exp(-t))


def _silu(t):
    half = 0.5 * t
    return half + half * jnp.tanh(half)


def _memkv_kernel(mem_ref, gain_ref, w_ref, mk_out, mv_out):
    nb, m, d = mem_ref.shape
    hm = _rms(mem_ref[...].reshape(nb * m, d), gain_ref[...]).astype(BF16)
    kv = _dot(hm, w_ref[...])
    half = HEADS * HEAD_DIM
    mk_out[...] = kv[:, :half].astype(BF16).reshape(nb, m, half)
    mv_out[...] = kv[:, half:].astype(BF16).reshape(nb, m, half)


def _pack_kernel(w_ref, wall_out, wg_out):
    o = [int(v) for v in np.cumsum((0,) + IN_SPLITS)]
    rows = w_ref.shape[1]
    lane = lax.broadcasted_iota(jnp.int32, (rows, LANES), 1)
    half = MLA_ROPE // 2
    xk = w_ref[0, :, o[2]:o[2] + LANES]
    kr2 = jnp.where(lane < MLA_ROPE, xk,
                    jnp.where(lane < MLA_ROPE + half, pltpu.roll(xk, half, 1), pltpu.roll(xk, LANES - half, 1)))
    xab = w_ref[0, :, o[4]:o[4] + LANES]
    wall_out[:, C_CQ:C_KR] = w_ref[0, :, o[0]:o[2]].astype(BF16)
    wall_out[:, C_KR:C_AB] = kr2.astype(BF16)
    wall_out[:, C_AB:C_MQ] = jnp.where(lane < 2 * HEADS, xab, 0.0).astype(BF16)
    wall_out[:, C_MQ:C_GATE] = w_ref[0, :, o[6]:o[7]].astype(BF16)
    wall_out[:, C_GATE:C_END] = w_ref[0, :, o[7]:o[8]].astype(BF16)
    for g in range(GDN_GROUPS):
        wg_out[g] = w_ref[0, :, o[3] + GDN_GROUP * g:o[3] + GDN_GROUP * (g + 1)].astype(BF16)


def _rope_kernel(pos_ref, invf_ref, cos_out, sin_out):
    ang = pos_ref[...].astype(F32) * invf_ref[...]
    cos_out[...] = jnp.cos(ang)
    sin_out[...] = jnp.sin(ang)


def _proj_kernel(x_ref, cos_ref, sin_ref, nin_ref, w_ref, wg_ref, qan_ref, wqb_ref, kvan_ref, wkk_ref, wkv_ref,
                 conv_ref, alog_ref, dtb_ref, mk_ref, mv_ref,
                 q_out, k_out, v_out, gqkv_out, gg_out, ggt_out, om_out, gate_out,
                 cbuf, *, ts):
    j = pl.program_id(1)
    hd = HEADS * HEAD_DIM
    hb = _rms(x_ref[0], nin_ref[...]).astype(BF16)

    def proj(lo, hi):
        return _dot(hb, w_ref[:, lo:hi])

    @pl.when(j == 0)
    def _():
        cbuf[:, 0:SUBLANES, :] = jnp.zeros((GDN_GROUPS, SUBLANES, GDN_GROUP), F32)

    @pl.when(j > 0)
    def _():
        cbuf[:, 0:SUBLANES, :] = cbuf[:, ts:ts + SUBLANES, :]

    def gdn_dot(g):
        cbuf[g, SUBLANES:ts + SUBLANES, :] = _dot(hb, wg_ref[g])

    def gdn_epilogue(g):
        taps = conv_ref[g]
        for sub in range(GDN_GROUP // LANES):
            cols = slice(LANES * sub, LANES * (sub + 1))
            blk = cbuf[g, :, cols]
            acc = taps[GDN_CONV - 1:GDN_CONV, cols] * blk[SUBLANES:]
            for back in range(1, GDN_CONV):
                acc = acc + taps[GDN_CONV - 1 - back:GDN_CONV - back, cols] * pltpu.roll(blk, back, 0)[SUBLANES:]
            y = _silu(acc)
            if g < 2 * (GDN_GROUPS // 3):
                inv = lax.rsqrt(jnp.sum(y * y, axis=-1, keepdims=True) + EPS)
                y = y * (inv * GDN_QSCALE if g < GDN_GROUPS // 3 else inv)
            gqkv_out[0, g, :, cols] = y.astype(BF16)

    narrow = proj(C_CQ, C_MQ)
    cq = narrow[:, C_CQ:C_CKV]
    ckv = narrow[:, C_CKV:C_KR]
    kr_raw = narrow[:, C_KR:C_AB]
    ab = narrow[:, C_AB:C_MQ]
    cqn = _rms(cq, qan_ref[...]).astype(BF16)
    gdn_dot(0)
    ckvn = _rms(ckv, kvan_ref[...]).astype(BF16)
    qf = _dot(cqn, wqb_ref[...])

    c32, s32 = cos_ref[0].T, sin_ref[0].T
    zpad = jnp.zeros((ts, LANES - MLA_ROPE), F32)
    cosm = jnp.concatenate([c32, c32, zpad], axis=1)
    sinm = jnp.concatenate([-s32, s32, zpad], axis=1)

    def rope(r):
        return r * cosm + pltpu.roll(r, MLA_ROPE, 1) * sinm

    gdn_dot(1)
    gdn_epilogue(0)
    kn = _dot(ckvn, wkk_ref[...])
    vv = _dot(ckvn, wkv_ref[...])
    qscale = MLA_SCALE * LOG2E
    for h in range(HEADS):
        lo = QK_WIDTH * h
        q_out[0, :, lo:lo + LANES] = (qf[:, lo:lo + LANES] * qscale).astype(BF16)
        q_out[0, :, lo + LANES:lo + QK_WIDTH] = (rope(qf[:, lo + LANES:lo + QK_WIDTH]) * qscale).astype(BF16)
    gdn_dot(2)
    gdn_epilogue(1)
    mq = proj(C_MQ, C_GATE)
    v_out[0] = vv.astype(BF16)
    kr = rope(kr_raw).astype(BF16)
    for h in range(HEADS):
        lo = QK_WIDTH * h
        k_out[0, :, lo:lo + LANES] = kn[:, LANES * h:LANES * (h + 1)].astype(BF16)
        k_out[0, :, lo + LANES:lo + QK_WIDTH] = kr
    gdn_dot(3)
    gdn_epilogue(2)

    mscale = MEM_SCALE * LOG2E
    sc, pp, ll, oo = {}, {}, {}, {}

    def mem_qk(h):
        cols = slice(HEAD_DIM * h, HEAD_DIM * (h + 1))
        sc[h] = _dot_nt((mq[:, cols] * mscale).astype(BF16), mk_ref[0, :, cols])

    def mem_softmax(h):
        p = jnp.exp2(sc[h] - jnp.max(sc[h], axis=-1, keepdims=True))
        ll[h] = jnp.sum(p, axis=-1, keepdims=True)
        pp[h] = p.astype(BF16)

    def mem_pv(h):
        oo[h] = _dot(pp[h], mv_ref[0, :, HEAD_DIM * h:HEAD_DIM * (h + 1)])

    mem_qk(0)
    z = ab + dtb_ref[...]
    softplus = jnp.maximum(z, 0.0) + jnp.log1p(jnp.exp(-jnp.abs(z)))
    gcum = -jnp.exp(alog_ref[...]) * softplus
    lane = lax.broadcasted_iota(jnp.int32, ab.shape, 1)
    pos_in_chunk = lax.broadcasted_iota(jnp.int32, ab.shape, 0) % GDN_CHUNK
    shift = 1
    while shift < GDN_CHUNK:
        gcum = gcum + jnp.where(pos_in_chunk >= shift, pltpu.roll(gcum, shift, 0), 0.0)
        shift *= 2
    gg_out[0] = jnp.where(lane < HEADS, gcum, _sigmoid(ab))
    ggt_out[0] = gcum.T[0:SUBLANES, :]

    gdn_dot(4)
    gdn_epilogue(3)
    mem_qk(1)
    mem_softmax(0)
    gdn_dot(5)
    gdn_epilogue(4)
    mem_qk(2)
    mem_pv(0)
    mem_softmax(1)
    gate_out[0, :, 0:hd] = proj(C_GATE, C_GATE + hd).astype(BF16)
    gdn_epilogue(5)
    mem_qk(3)
    mem_pv(1)
    mem_softmax(2)
    gate_out[0, :, hd:2 * hd] = proj(C_GATE + hd, C_GATE + 2 * hd).astype(BF16)
    mem_pv(2)
    mem_softmax(3)
    gmem = _silu(proj(C_GATE + 2 * hd, C_END))
    mem_pv(3)
    for h in range(HEADS):
        cols = slice(HEAD_DIM * h, HEAD_DIM * (h + 1))
        om_out[0, :, cols] = (oo[h] / ll[h] * gmem[:, cols]).astype(BF16)


def _mla_kernel(q_ref, k_ref, v_ref, o_ref, s_scr, p_scr, m_scr, l_scr, a_scr, acc_scr, *, tq):
    i = pl.program_id(1)
    heads = range(HEADS)
    strip = MLA_STRIP

    def run(n_full):
        half = tq // 2
        items = [(h, r0, half, r0 + half, n_full * tq, True) for h in heads for r0 in (0, half)]
        items += [(h, 0, tq, tq, jk * tq, False) for jk in range(n_full) for h in heads]

        def scores(item):
            h, r0, nr, nk, start, _ = item
            s_scr[h, r0:r0 + nr, 0:nk] = _dot_nt(q_ref[0, r0:r0 + nr, QK_WIDTH * h:QK_WIDTH * (h + 1)],
                                                 k_ref[0, start:start + nk, QK_WIDTH * h:QK_WIDTH * (h + 1)])

        def load_strip(item, r):
            h, r0, nr, nk, _, diagonal = item
            s = s_scr[h, r0 + strip * r:r0 + strip * (r + 1), 0:nk]
            if diagonal:
                row = lax.broadcasted_iota(jnp.int32, s.shape, 0) + (r0 + strip * r)
                col = lax.broadcasted_iota(jnp.int32, s.shape, 1)
                s = jnp.where(row >= col, s, -jnp.inf)
            return s

        def lane_tiles(t):
            return [t[:, LANES * g:LANES * (g + 1)] for g in range(t.shape[1] // LANES)]

        def softmax(item):
            h, r0, nr, nk, _, diagonal = item
            strips = range(nr // strip)
            rows = [slice(r0 + strip * r, r0 + strip * (r + 1)) for r in strips]
            part = [functools.reduce(jnp.maximum, lane_tiles(load_strip(item, r))) for r in strips]
            peak = [jnp.broadcast_to(jnp.max(t, axis=-1, keepdims=True), (strip, LANES)) for t in part]
            if diagonal:
                m_new = peak
            else:
                m_old = [m_scr[h, rows[r], :] for r in strips]
                m_new = [jnp.maximum(m_old[r], peak[r]) for r in strips]
                alpha = [jnp.exp2(m_old[r] - m_new[r]) for r in strips]
                for r in strips:
                    a_scr[h, rows[r], :] = alpha[r]
            for r in strips:
                m_scr[h, rows[r], :] = m_new[r]
            part = []
            for r in strips:
                p = [jnp.exp2(t - m_new[r]) for t in lane_tiles(load_strip(item, r))]
                for g, t in enumerate(p):
                    p_scr[h, rows[r], LANES * g:LANES * (g + 1)] = t.astype(BF16)
                part.append(functools.reduce(jnp.add, p))
            total = [jnp.broadcast_to(jnp.sum(t, axis=-1, keepdims=True), (strip, LANES)) for t in part]
            for r in strips:
                l_scr[h, rows[r], :] = total[r] if diagonal else alpha[r] * l_scr[h, rows[r], :] + total[r]

        def values(item):
            h, r0, nr, nk, start, diagonal = item
            pv = _dot(p_scr[h, r0:r0 + nr, 0:nk], v_ref[0, start:start + nk, HEAD_DIM * h:HEAD_DIM * (h + 1)])
            if diagonal:
                acc_scr[h, r0:r0 + nr, :] = pv
            else:
                acc_scr[h, r0:r0 + nr, :] = a_scr[h, r0:r0 + nr, :] * acc_scr[h, r0:r0 + nr, :] + pv

        for t in range(len(items) + 2):
            if t < len(items):
                scores(items[t])
            if 0 <= t - 1 < len(items):
                softmax(items[t - 1])
            if 0 <= t - 2 < len(items):
                values(items[t - 2])

        for h in heads:
            o_ref[0, :, HEAD_DIM * h:HEAD_DIM * (h + 1)] = (acc_scr[h] / l_scr[h]).astype(BF16)

    for n_full in range(k_ref.shape[1] // tq):
        pl.when(i == n_full)(functools.partial(run, n_full))


def _pair_blockdiag(t, lo_half):
    return jnp.concatenate([jnp.where(lo_half, t, 0.0), jnp.where(lo_half, 0.0, t)], axis=0).astype(BF16)


def _gdn_merge_kernel(qkv_ref, gg_ref, ggt_ref, gn_ref, x_ref, omla_ref, om_ref, gate_ref, wout_ref,
                      nf_ref, out_ref, s_ref, og_scr, *, ts):
    j = pl.program_id(1)
    hd = HEADS * HEAD_DIM
    ntile = 2 * LANES

    @pl.when(j == 0)
    def _():
        s_ref[...] = jnp.zeros(s_ref.shape, F32)

    c = GDN_CHUNK
    c2 = 2 * c
    npairs = ts // c2
    row = lax.broadcasted_iota(jnp.int32, (c, c2), 0)
    lane = lax.broadcasted_iota(jnp.int32, (c, c2), 1)
    lo_half = lane < c
    col = jnp.where(lo_half, lane, lane - c)
    incl = row >= col
    strict = row > col
    eye = jnp.where(row == col, 1.0, 0.0)
    zeros_k = jnp.zeros((c, HEAD_DIM), BF16)
    units = [(p2, h) for p2 in range(npairs) for h in range(HEADS)]

    pre = {}
    for (p2, h) in units:
        r2 = slice(c2 * p2, c2 * (p2 + 1))
        gg = gg_ref[0, r2, :]
        gcol = gg[:, h:h + 1]
        beta = gg[:, HEADS + h:HEADS + h + 1]
        grow = ggt_ref[0, h:h + 1, r2]
        glast = (gg[c - 1:c, h:h + 1], gg[c2 - 1:c2, h:h + 1])
        glast_col = jnp.concatenate([jnp.broadcast_to(glast[0], (c, 1)), jnp.broadcast_to(glast[1], (c, 1))], axis=0)
        per_group = GDN_GROUP // HEAD_DIM
        gcols = slice(HEAD_DIM * (h % per_group), HEAD_DIM * (h % per_group + 1))
        q2 = qkv_ref[0, h // per_group, r2, gcols]
        k2 = qkv_ref[0, GDN_GROUPS // 3 + h // per_group, r2, gcols]
        kf = k2.astype(F32)
        kbeta = kf * beta
        kbeta_b = kbeta.astype(BF16)
        vbeta_b = (qkv_ref[0, 2 * (GDN_GROUPS // 3) + h // per_group, r2, gcols].astype(F32) * beta).astype(BF16)
        eg = jnp.exp(gcol)
        lhs = jnp.concatenate([jnp.concatenate([kbeta_b[:c], kbeta_b[c:]], axis=1),
                               jnp.concatenate([q2[:c], q2[c:]], axis=1)], axis=0)
        rhs = jnp.concatenate([jnp.concatenate([k2[:c], zeros_k], axis=1),
                               jnp.concatenate([zeros_k, k2[c:]], axis=1)], axis=0)
        gcol_pair = jnp.where(lo_half, gcol[:c], gcol[c:])
        pre[(p2, h)] = dict(
            lhs=lhs, rhs=rhs, glast=glast,
            vk=jnp.concatenate([vbeta_b, (kbeta * eg).astype(BF16)], axis=1),
            qg=(q2.astype(F32) * eg).astype(BF16),
            kdec=(kf * jnp.exp(glast_col - gcol)).astype(BF16),
            decay=jnp.exp(jnp.where(incl, gcol_pair - grow, -jnp.inf)))

    kq = {u: _dot_nt(pre[u]["lhs"], pre[u]["rhs"]) for u in units}
    a_pair = {u: kq[u][c:] * pre[u]["decay"] for u in units}
    m = {u: -jnp.where(strict, kq[u][:c] * pre[u]["decay"], 0.0) for u in units}

    p = {u: eye + m[u] for u in units}
    m = {u: _dot(m[u].astype(BF16), _pair_blockdiag(m[u], lo_half)) for u in units}
    for _ in range(int(np.log2(c)) - 2):
        pm = {u: _dot(jnp.concatenate([p[u], m[u]], axis=0).astype(BF16), _pair_blockdiag(m[u], lo_half))
              for u in units}
        p = {u: p[u] + pm[u][:c] for u in units}
        m = {u: pm[u][c:] for u in units}
    pm = {u: _dot(p[u].astype(BF16), _pair_blockdiag(m[u], lo_half)) for u in units}
    t_pair = {u: p[u] + pm[u] for u in units}

    uw = {u: _dot(_pair_blockdiag(t_pair[u], lo_half), pre[u]["vk"]) for u in units}
    a_chunks = {u: (a_pair[u][:, :c].astype(BF16), pltpu.roll(a_pair[u], c, 1)[:, :c].astype(BF16)) for u in units}

    a_mla = (omla_ref[0].astype(F32) * _silu(gate_ref[0, :, 0:hd].astype(F32))).astype(BF16)
    free_parts = [(a_mla, 0), (om_ref[0], 2 * hd)]
    pieces = [(part, n) for part in range(len(free_parts)) for n in range(D_MODEL // ntile)]
    partial = {}

    heads = range(HEADS)
    state = [s_ref[h] for h in heads]
    nsteps = 2 * npairs
    for step in range(nsteps):
        p2, ci = divmod(step, 2)
        rc = slice(c * ci, c * (ci + 1))
        rows = slice(c * step, c * (step + 1))
        ws_qs = [_dot(jnp.concatenate([uw[(p2, h)][rc, HEAD_DIM:].astype(BF16), pre[(p2, h)]["qg"][rc]], axis=0),
                      state[h].astype(BF16)) for h in heads]
        for part, n in pieces[len(pieces) * step // nsteps:len(pieces) * (step + 1) // nsteps]:
            operand, w_lo = free_parts[part]
            partial[(part, n)] = _dot(operand, wout_ref[w_lo:w_lo + hd, ntile * n:ntile * (n + 1)])
        v_new = [(uw[(p2, h)][rc, :HEAD_DIM] - ws_qs[h][:c]).astype(BF16) for h in heads]
        o_intra = [_dot(a_chunks[(p2, h)][ci], v_new[h]) for h in heads]
        ds = [_dot_tn(pre[(p2, h)]["kdec"][rc], v_new[h]) for h in heads]
        state = [state[h] * jnp.exp(pre[(p2, h)]["glast"][ci]) + ds[h] for h in heads]
        for h in heads:
            cols = slice(HEAD_DIM * h, HEAD_DIM * (h + 1))
            gate = _silu(gate_ref[0, rows, hd + HEAD_DIM * h:hd + HEAD_DIM * (h + 1)].astype(F32))
            og_scr[rows, cols] = (_rms(ws_qs[h][c:] + o_intra[h], gn_ref[...]) * gate).astype(BF16)
    for h in heads:
        s_ref[h] = state[h]

    og = og_scr[...]
    ys = []
    for n in range(D_MODEL // ntile):
        ncols = slice(ntile * n, ntile * (n + 1))
        acc = _dot(og, wout_ref[hd:2 * hd, ncols])
        for part in range(len(free_parts)):
            acc = acc + partial[(part, n)]
        ys.append(x_ref[0, :, ncols] + acc)
    ssq = functools.reduce(jnp.add, [jnp.sum(y * y, axis=-1, keepdims=True) for y in ys])
    inv = lax.rsqrt(ssq * (1.0 / D_MODEL) + EPS)
    for n, y in enumerate(ys):
        ncols = slice(ntile * n, ntile * (n + 1))
        out_ref[0, :, ncols] = y * inv * nf_ref[:, ncols]


def _const_spec(shape):
    nd = len(shape)
    return pl.BlockSpec(shape, lambda *_: (0,) * nd)


def _pack_weights(w_in_all, layer, w_q_b, w_kv_b):
    tr = _tile(D_MODEL, 256)
    w_all, w_groups = pl.pallas_call(
        _pack_kernel,
        grid=(D_MODEL // tr,),
        in_specs=[pl.BlockSpec((1, tr, w_in_all.shape[2]), lambda r: (layer, r, 0))],
        out_specs=[pl.BlockSpec((tr, C_END), lambda r: (r, 0)),
                   pl.BlockSpec((GDN_GROUPS, tr, GDN_GROUP), lambda r: (0, r, 0))],
        out_shape=[jax.ShapeDtypeStruct((D_MODEL, C_END), BF16),
                   jax.ShapeDtypeStruct((GDN_GROUPS, D_MODEL, GDN_GROUP), BF16)],
        compiler_params=pltpu.CompilerParams(dimension_semantics=("arbitrary",), vmem_limit_bytes=VMEM_LIMIT),
        name="pack",
    )(w_in_all)
    half = MLA_ROPE // 2
    wq = w_q_b.reshape(Q_LORA, HEADS, HEAD_DIM + MLA_ROPE)
    nope, ropec = wq[..., :HEAD_DIM], wq[..., HEAD_DIM:]
    wqb = jnp.concatenate([nope, ropec, ropec[..., half:], ropec[..., :half]], axis=-1)
    wqb = wqb.reshape(Q_LORA, HEADS * QK_WIDTH).astype(BF16)
    wkv = w_kv_b.reshape(KV_LORA, HEADS, 2 * HEAD_DIM)
    wkk = wkv[..., :HEAD_DIM].reshape(KV_LORA, HEADS * HEAD_DIM).astype(BF16)
    wkvv = wkv[..., HEAD_DIM:].reshape(KV_LORA, HEADS * HEAD_DIM).astype(BF16)
    return w_all, w_groups, wqb, wkk, wkvv


def _lane_row(vec):
    return jnp.zeros((1, LANES), F32).at[0, :vec.shape[0]].set(vec.astype(F32))


def _layer(x, mk, mv, cos, sin, norm_in, w_in_all, layer, q_a_norm, w_q_b, kv_a_norm, w_kv_b, gdn_conv,
           gdn_a_log, gdn_dt_bias, gdn_norm, w_out, out_gain, *, ts_proj, ts_gdn, tq):
    B, S, D = x.shape
    hd = HEADS * HEAD_DIM
    w_all, w_groups, wqb, wkk, wkvv = _pack_weights(w_in_all, layer, w_q_b, w_kv_b)
    conv_groups = gdn_conv.reshape(GDN_CONV, GDN_GROUPS, GDN_GROUP).transpose(1, 0, 2)
    M = mk.shape[1]
    arb2 = pltpu.CompilerParams(dimension_semantics=("arbitrary", "arbitrary"), vmem_limit_bytes=VMEM_LIMIT)

    def row_spec(width, ts):
        return pl.BlockSpec((1, ts, width), lambda b, j: (b, j, 0))

    def bs_shape(width, dtype=BF16):
        return jax.ShapeDtypeStruct((B, S, width), dtype)

    tile_spec = pl.BlockSpec((1, MLA_ROPE // 2, ts_proj), lambda b, j: (b * (S // ts_proj) + j, 0, 0))

    q, k, v, gqkv, gg, ggt, om, gate = pl.pallas_call(
        functools.partial(_proj_kernel, ts=ts_proj),
        grid=(B, S // ts_proj),
        in_specs=[row_spec(D, ts_proj), tile_spec, tile_spec,
                  _const_spec((1, D)), _const_spec(w_all.shape), _const_spec(w_groups.shape),
                  _const_spec((1, Q_LORA)), _const_spec(wqb.shape), _const_spec((1, KV_LORA)),
                  _const_spec(wkk.shape), _const_spec(wkvv.shape), _const_spec(conv_groups.shape),
                  _const_spec((1, LANES)), _const_spec((1, LANES)),
                  pl.BlockSpec((1, M, hd), lambda b, j: (b, 0, 0)), pl.BlockSpec((1, M, hd), lambda b, j: (b, 0, 0))],
        out_specs=[row_spec(HEADS * QK_WIDTH, ts_proj), row_spec(HEADS * QK_WIDTH, ts_proj), row_spec(hd, ts_proj),
                   pl.BlockSpec((1, GDN_GROUPS, ts_proj, GDN_GROUP), lambda b, j: (b, 0, j, 0)),
                   row_spec(LANES, ts_proj), pl.BlockSpec((1, SUBLANES, ts_proj), lambda b, j: (b, 0, j)),
                   row_spec(hd, ts_proj), row_spec(2 * hd, ts_proj)],
        out_shape=[bs_shape(HEADS * QK_WIDTH), bs_shape(HEADS * QK_WIDTH), bs_shape(hd),
                   jax.ShapeDtypeStruct((B, GDN_GROUPS, S, GDN_GROUP), BF16),
                   bs_shape(LANES, F32), jax.ShapeDtypeStruct((B, SUBLANES, S), F32),
                   bs_shape(hd), bs_shape(2 * hd)],
        scratch_shapes=[pltpu.VMEM((GDN_GROUPS, ts_proj + SUBLANES, GDN_GROUP), F32)],
        compiler_params=arb2,
        name="proj",
    )(x, cos, sin, norm_in.reshape(1, D), w_all, w_groups, q_a_norm.reshape(1, Q_LORA), wqb,
      kv_a_norm.reshape(1, KV_LORA), wkk, wkvv, conv_groups, _lane_row(gdn_a_log), _lane_row(gdn_dt_bias), mk, mv)

    o_mla = pl.pallas_call(
        functools.partial(_mla_kernel, tq=tq),
        grid=(B, S // tq),
        in_specs=[pl.BlockSpec((1, tq, HEADS * QK_WIDTH), lambda b, i: (b, i, 0)),
                  pl.BlockSpec((1, S, HEADS * QK_WIDTH), lambda b, i: (b, 0, 0)),
                  pl.BlockSpec((1, S, hd), lambda b, i: (b, 0, 0))],
        out_specs=pl.BlockSpec((1, tq, hd), lambda b, i: (b, i, 0)),
        out_shape=bs_shape(hd),
        scratch_shapes=[pltpu.VMEM((HEADS, tq, tq), F32), pltpu.VMEM((HEADS, tq, tq), BF16),
                        pltpu.VMEM((HEADS, tq, LANES), F32), pltpu.VMEM((HEADS, tq, LANES), F32),
                        pltpu.VMEM((HEADS, tq, LANES), F32), pltpu.VMEM((HEADS, tq, HEAD_DIM), F32)],
        compiler_params=arb2,
        name="mla",
    )(q, k, v)

    return pl.pallas_call(
        functools.partial(_gdn_merge_kernel, ts=ts_gdn),
        grid=(B, S // ts_gdn),
        in_specs=[pl.BlockSpec((1, GDN_GROUPS, ts_gdn, GDN_GROUP), lambda b, j: (b, 0, j, 0)), row_spec(LANES, ts_gdn),
                  pl.BlockSpec((1, SUBLANES, ts_gdn), lambda b, j: (b, 0, j)), _const_spec((1, HEAD_DIM)),
                  row_spec(D, ts_gdn), row_spec(hd, ts_gdn), row_spec(hd, ts_gdn), row_spec(2 * hd, ts_gdn),
                  _const_spec((D_MIX, D)), _const_spec((1, D))],
        out_specs=row_spec(D, ts_gdn),
        out_shape=bs_shape(D, F32),
        scratch_shapes=[pltpu.VMEM((HEADS, HEAD_DIM, HEAD_DIM), F32), pltpu.VMEM((ts_gdn, hd), BF16)],
        compiler_params=arb2,
        name="gdn_merge",
    )(gqkv, gg, ggt, gdn_norm.reshape(1, HEAD_DIM), x, o_mla, om, gate, w_out.astype(BF16),
      out_gain.reshape(1, D))


def _tile(n, pref):
    return pref if n % pref == 0 else n


def kernel(x, mem, positions, norm_in, w_in, q_a_norm, w_q_b, kv_a_norm, w_kv_b, gdn_conv, gdn_a_log,
           gdn_dt_bias, gdn_norm, mem_norm, w_mem_kv, w_out, norm_final):
    B, S, D = x.shape
    M = mem.shape[1]
    depth = norm_in.shape[0]
    assert depth == 1, "the final norm is fused into the single layer's last kernel"
    hd = HEADS * HEAD_DIM
    half = MLA_ROPE // 2
    inv_freq = 1.0 / (ROPE_THETA ** (jnp.arange(half, dtype=F32) / half))
    ts_proj = _tile(S, 512)
    n_tiles = B * S // ts_proj
    nt = _tile(n_tiles, 8)
    cos, sin = pl.pallas_call(
        _rope_kernel,
        grid=(n_tiles // nt,),
        in_specs=[pl.BlockSpec((nt, 1, ts_proj), lambda r: (r, 0, 0)), _const_spec((1, half, 1))],
        out_specs=[pl.BlockSpec((nt, half, ts_proj), lambda r: (r, 0, 0))] * 2,
        out_shape=[jax.ShapeDtypeStruct((n_tiles, half, ts_proj), F32)] * 2,
        compiler_params=pltpu.CompilerParams(dimension_semantics=("arbitrary",), vmem_limit_bytes=VMEM_LIMIT),
        name="rope",
    )(positions.reshape(n_tiles, 1, ts_proj), inv_freq.reshape(1, half, 1))
    l = 0
    nb_mem = _tile(B, 4)
    mk, mv = pl.pallas_call(
        _memkv_kernel,
        grid=(B // nb_mem,),
        in_specs=[pl.BlockSpec((nb_mem, M, D), lambda b: (b, 0, 0)), _const_spec((1, D)), _const_spec((D, 2 * hd))],
        out_specs=[pl.BlockSpec((nb_mem, M, hd), lambda b: (b, 0, 0))] * 2,
        out_shape=[jax.ShapeDtypeStruct((B, M, hd), BF16)] * 2,
        compiler_params=pltpu.CompilerParams(dimension_semantics=("arbitrary",), vmem_limit_bytes=VMEM_LIMIT),
        name="memkv",
    )(mem, mem_norm[l].reshape(1, D), w_mem_kv[l].astype(BF16))
    return _layer(x, mk, mv, cos, sin, norm_in[l], w_in, l, q_a_norm[l], w_q_b[l], kv_a_norm[l], w_kv_b[l],
                  gdn_conv[l], gdn_a_log[l], gdn_dt_bias[l], gdn_norm[l], w_out[l], norm_final,
                  ts_proj=ts_proj, ts_gdn=_tile(S, 512), tq=_tile(S, 512))
```

```python
import functools

import jax
import jax.numpy as jnp
import numpy as np
from jax import lax
from jax.experimental import pallas as pl
from jax.experimental.pallas import tpu as pltpu

F32 = jnp.float32
BF16 = jnp.bfloat16

D_MODEL = 1024
HEADS = 4
HEAD_DIM = 128
MLA_ROPE = 64
Q_LORA = 384
KV_LORA = 256
ROPE_THETA = 10000.0
GDN_CONV = 4
GDN_CHUNK = 64
GDN_QKV = 3 * HEADS * HEAD_DIM
D_MIX = 3 * HEADS * HEAD_DIM
IN_SPLITS = (Q_LORA, KV_LORA, MLA_ROPE, GDN_QKV, HEADS, HEADS, HEADS * HEAD_DIM, D_MIX)
EPS = 1e-6
MLA_SCALE = (HEAD_DIM + MLA_ROPE) ** -0.5
LOG2E = 1.4426950408889634
MEM_SCALE = HEAD_DIM ** -0.5
GDN_QSCALE = HEAD_DIM ** -0.5

LANES = 128
SUBLANES = 8
QK_WIDTH = 2 * LANES
MLA_STRIP = 64

C_CQ = 0
C_CKV = C_CQ + Q_LORA
C_KR = C_CKV + KV_LORA
C_AB = C_KR + LANES
C_MQ = C_AB + LANES
C_GATE = C_MQ + HEADS * HEAD_DIM
C_END = C_GATE + D_MIX

GDN_GROUP = 2 * LANES
GDN_GROUPS = GDN_QKV // GDN_GROUP

V7X_VMEM_BYTES = 64 * 1024 * 1024
VMEM_LIMIT = V7X_VMEM_BYTES * 7 // 8


def _dot(a, b, precision=None):
    return jnp.dot(a, b, preferred_element_type=F32, precision=precision)


def _dot_nt(a, b):
    return lax.dot_general(a, b, (((1,), (1,)), ((), ())), preferred_element_type=F32)


def _dot_tn(a, b):
    return lax.dot_general(a, b, (((0,), (0,)), ((), ())), preferred_element_type=F32)


def _rms(t, gain):
    return t * lax.rsqrt(jnp.mean(t * t, axis=-1, keepdims=True) + EPS) * gain


def _sigmoid(t):
    return 1.0 / (1.0 + jnp.exp(-t))


def _silu(t):
    half = 0.5 * t
    return half + half * jnp.tanh(half)


def _memkv_kernel(mem_ref, gain_ref, w_ref, mk_out, mv_out):
    nb, m, d = mem_ref.shape
    hm = _rms(mem_ref[...].reshape(nb * m, d), gain_ref[...]).astype(BF16)
    kv = _dot(hm, w_ref[...])
    half = HEADS * HEAD_DIM
    mk_out[...] = kv[:, :half].astype(BF16).reshape(nb, m, half)
    mv_out[...] = kv[:, half:].astype(BF16).reshape(nb, m, half)


def _pack_kernel(w_ref, wall_out, wg_out):
    o = [int(v) for v in np.cumsum((0,) + IN_SPLITS)]
    rows = w_ref.shape[1]
    lane = lax.broadcasted_iota(jnp.int32, (rows, LANES), 1)
    half = MLA_ROPE // 2
    xk = w_ref[0, :, o[2]:o[2] + LANES]
    kr2 = jnp.where(lane < MLA_ROPE, xk,
                    jnp.where(lane < MLA_ROPE + half, pltpu.roll(xk, half, 1), pltpu.roll(xk, LANES - half, 1)))
    xab = w_ref[0, :, o[4]:o[4] + LANES]
    wall_out[:, C_CQ:C_KR] = w_ref[0, :, o[0]:o[2]].astype(BF16)
    wall_out[:, C_KR:C_AB] = kr2.astype(BF16)
    wall_out[:, C_AB:C_MQ] = jnp.where(lane < 2 * HEADS, xab, 0.0).astype(BF16)
    wall_out[:, C_MQ:C_GATE] = w_ref[0, :, o[6]:o[7]].astype(BF16)
    wall_out[:, C_GATE:C_END] = w_ref[0, :, o[7]:o[8]].astype(BF16)
    for g in range(GDN_GROUPS):
        wg_out[g] = w_ref[0, :, o[3] + GDN_GROUP * g:o[3] + GDN_GROUP * (g + 1)].astype(BF16)


def _rope_kernel(pos_ref, invf_ref, cos_out, sin_out):
    ang = pos_ref[...].astype(F32) * invf_ref[...]
    cos_out[...] = jnp.cos(ang)
    sin_out[...] = jnp.sin(ang)


def _proj_kernel(x_ref, cos_ref, sin_ref, nin_ref, w_ref, wg_ref, qan_ref, wqb_ref, kvan_ref, wkk_ref, wkv_ref,
                 conv_ref, alog_ref, dtb_ref, mk_ref, mv_ref,
                 q_out, k_out, v_out, gqkv_out, gg_out, ggt_out, om_out, gate_out,
                 cbuf, *, ts):
    j = pl.program_id(1)
    hd = HEADS * HEAD_DIM
    hb = _rms(x_ref[0], nin_ref[...]).astype(BF16)

    def proj(lo, hi):
        return _dot(hb, w_ref[:, lo:hi])

    @pl.when(j == 0)
    def _():
        cbuf[:, 0:SUBLANES, :] = jnp.zeros((GDN_GROUPS, SUBLANES, GDN_GROUP), F32)

    @pl.when(j > 0)
    def _():
        cbuf[:, 0:SUBLANES, :] = cbuf[:, ts:ts + SUBLANES, :]

    def gdn_dot(g):
        cbuf[g, SUBLANES:ts + SUBLANES, :] = _dot(hb, wg_ref[g])

    def gdn_epilogue(g):
        taps = conv_ref[g]
        for sub in range(GDN_GROUP // LANES):
            cols = slice(LANES * sub, LANES * (sub + 1))
            blk = cbuf[g, :, cols]
            acc = taps[GDN_CONV - 1:GDN_CONV, cols] * blk[SUBLANES:]
            for back in range(1, GDN_CONV):
                acc = acc + taps[GDN_CONV - 1 - back:GDN_CONV - back, cols] * pltpu.roll(blk, back, 0)[SUBLANES:]
            y = _silu(acc)
            if g < 2 * (GDN_GROUPS // 3):
                inv = lax.rsqrt(jnp.sum(y * y, axis=-1, keepdims=True) + EPS)
                y = y * (inv * GDN_QSCALE if g < GDN_GROUPS // 3 else inv)
            gqkv_out[0, g, :, cols] = y.astype(BF16)

    narrow = proj(C_CQ, C_MQ)
    cq = narrow[:, C_CQ:C_CKV]
    ckv = narrow[:, C_CKV:C_KR]
    kr_raw = narrow[:, C_KR:C_AB]
    ab = narrow[:, C_AB:C_MQ]
    cqn = _rms(cq, qan_ref[...]).astype(BF16)
    gdn_dot(0)
    ckvn = _rms(ckv, kvan_ref[...]).astype(BF16)
    qf = _dot(cqn, wqb_ref[...])

    c32, s32 = cos_ref[0].T, sin_ref[0].T
    zpad = jnp.zeros((ts, LANES - MLA_ROPE), F32)
    cosm = jnp.concatenate([c32, c32, zpad], axis=1)
    sinm = jnp.concatenate([-s32, s32, zpad], axis=1)

    def rope(r):
        return r * cosm + pltpu.roll(r, MLA_ROPE, 1) * sinm

    gdn_epilogue(0)
    gdn_dot(1)
    kn = _dot(ckvn, wkk_ref[...])
    vv = _dot(ckvn, wkv_ref[...])
    mq = proj(C_MQ, C_GATE)
    qscale = MLA_SCALE * LOG2E
    for h in range(HEADS):
        lo = QK_WIDTH * h
        q_out[0, :, lo:lo + LANES] = (qf[:, lo:lo + LANES] * qscale).astype(BF16)
        q_out[0, :, lo + LANES:lo + QK_WIDTH] = (rope(qf[:, lo + LANES:lo + QK_WIDTH]) * qscale).astype(BF16)
    gdn_epilogue(1)
    gdn_dot(2)
    gate_out[0, :, 0:hd] = proj(C_GATE, C_GATE + hd).astype(BF16)
    v_out[0] = vv.astype(BF16)
    kr = rope(kr_raw).astype(BF16)
    for h in range(HEADS):
        lo = QK_WIDTH * h
        k_out[0, :, lo:lo + LANES] = kn[:, LANES * h:LANES * (h + 1)].astype(BF16)
        k_out[0, :, lo + LANES:lo + QK_WIDTH] = kr
    gdn_epilogue(2)
    gdn_dot(3)
    gate_out[0, :, hd:2 * hd] = proj(C_GATE + hd, C_GATE + 2 * hd).astype(BF16)

    mscale = MEM_SCALE * LOG2E
    sc, pp, ll, oo = {}, {}, {}, {}

    def mem_qk(h):
        cols = slice(HEAD_DIM * h, HEAD_DIM * (h + 1))
        sc[h] = _dot_nt((mq[:, cols] * mscale).astype(BF16), mk_ref[0, :, cols])

    def mem_softmax(h):
        p = jnp.exp2(sc[h] - jnp.max(sc[h], axis=-1, keepdims=True))
        ll[h] = jnp.sum(p, axis=-1, keepdims=True)
        pp[h] = p.astype(BF16)

    def mem_pv(h):
        oo[h] = _dot(pp[h], mv_ref[0, :, HEAD_DIM * h:HEAD_DIM * (h + 1)])

    mem_qk(0)
    z = ab + dtb_ref[...]
    softplus = jnp.maximum(z, 0.0) + jnp.log1p(jnp.exp(-jnp.abs(z)))
    gcum = -jnp.exp(alog_ref[...]) * softplus
    lane = lax.broadcasted_iota(jnp.int32, ab.shape, 1)
    pos_in_chunk = lax.broadcasted_iota(jnp.int32, ab.shape, 0) % GDN_CHUNK
    shift = 1
    while shift < GDN_CHUNK:
        gcum = gcum + jnp.where(pos_in_chunk >= shift, pltpu.roll(gcum, shift, 0), 0.0)
        shift *= 2
    gg_out[0] = jnp.where(lane < HEADS, gcum, _sigmoid(ab))
    ggt_out[0] = gcum.T[0:SUBLANES, :]

    gdn_epilogue(3)
    gdn_dot(4)
    mem_qk(1)
    mem_softmax(0)
    gdn_epilogue(4)
    gdn_dot(5)
    mem_qk(2)
    mem_pv(0)
    mem_softmax(1)
    gdn_epilogue(5)
    mem_qk(3)
    mem_pv(1)
    mem_softmax(2)
    mem_pv(2)
    mem_softmax(3)
    gmem = _silu(proj(C_GATE + 2 * hd, C_END))
    mem_pv(3)
    for h in range(HEADS):
        cols = slice(HEAD_DIM * h, HEAD_DIM * (h + 1))
        om_out[0, :, cols] = (oo[h] / ll[h] * gmem[:, cols]).astype(BF16)


def _mla_kernel(q_ref, k_ref, v_ref, o_ref, s_scr, p_scr, m_scr, l_scr, a_scr, acc_scr, *, tq):
    i = pl.program_id(1)
    heads = range(HEADS)
    strip = MLA_STRIP

    def run(n_full):
        half = tq // 2
        items = [(h, r0, half, r0 + half, n_full * tq, True) for h in heads for r0 in (0, half)]
        items += [(h, 0, tq, tq, jk * tq, False) for jk in range(n_full) for h in heads]

        def scores(item):
            h, r0, nr, nk, start, _ = item
            s_scr[h, r0:r0 + nr, 0:nk] = _dot_nt(q_ref[0, r0:r0 + nr, QK_WIDTH * h:QK_WIDTH * (h + 1)],
                                                 k_ref[0, start:start + nk, QK_WIDTH * h:QK_WIDTH * (h + 1)])

        def load_strip(item, r):
            h, r0, nr, nk, _, diagonal = item
            s = s_scr[h, r0 + strip * r:r0 + strip * (r + 1), 0:nk]
            if diagonal:
                row = lax.broadcasted_iota(jnp.int32, s.shape, 0) + (r0 + strip * r)
                col = lax.broadcasted_iota(jnp.int32, s.shape, 1)
                s = jnp.where(row >= col, s, -jnp.inf)
            return s

        def lane_tiles(t):
            return [t[:, LANES * g:LANES * (g + 1)] for g in range(t.shape[1] // LANES)]

        def softmax(item):
            h, r0, nr, nk, _, diagonal = item
            strips = range(nr // strip)
            rows = [slice(r0 + strip * r, r0 + strip * (r + 1)) for r in strips]
            part = [functools.reduce(jnp.maximum, lane_tiles(load_strip(item, r))) for r in strips]
            peak = [jnp.broadcast_to(jnp.max(t, axis=-1, keepdims=True), (strip, LANES)) for t in part]
            if diagonal:
                m_new = peak
            else:
                m_old = [m_scr[h, rows[r], :] for r in strips]
                m_new = [jnp.maximum(m_old[r], peak[r]) for r in strips]
                alpha = [jnp.exp2(m_old[r] - m_new[r]) for r in strips]
                for r in strips:
                    a_scr[h, rows[r], :] = alpha[r]
            for r in strips:
                m_scr[h, rows[r], :] = m_new[r]
            part = []
            for r in strips:
                p = [jnp.exp2(t - m_new[r]) for t in lane_tiles(load_strip(item, r))]
                for g, t in enumerate(p):
                    p_scr[h, rows[r], LANES * g:LANES * (g + 1)] = t.astype(BF16)
                part.append(functools.reduce(jnp.add, p))
            total = [jnp.broadcast_to(jnp.sum(t, axis=-1, keepdims=True), (strip, LANES)) for t in part]
            for r in strips:
                l_scr[h, rows[r], :] = total[r] if diagonal else alpha[r] * l_scr[h, rows[r], :] + total[r]

        def values(item):
            h, r0, nr, nk, start, diagonal = item
            pv = _dot(p_scr[h, r0:r0 + nr, 0:nk], v_ref[0, start:start + nk, HEAD_DIM * h:HEAD_DIM * (h + 1)])
            if diagonal:
                acc_scr[h, r0:r0 + nr, :] = pv
            else:
                acc_scr[h, r0:r0 + nr, :] = a_scr[h, r0:r0 + nr, :] * acc_scr[h, r0:r0 + nr, :] + pv

        for t in range(len(items) + 2):
            if t < len(items):
                scores(items[t])
            if 0 <= t - 1 < len(items):
                softmax(items[t - 1])
            if 0 <= t - 2 < len(items):
                values(items[t - 2])

        for h in heads:
            o_ref[0, :, HEAD_DIM * h:HEAD_DIM * (h + 1)] = (acc_scr[h] / l_scr[h]).astype(BF16)

    for n_full in range(k_ref.shape[1] // tq):
        pl.when(i == n_full)(functools.partial(run, n_full))


def _pair_blockdiag(t, lo_half):
    return jnp.concatenate([jnp.where(lo_half, t, 0.0), jnp.where(lo_half, 0.0, t)], axis=0).astype(BF16)


def _gdn_merge_kernel(qkv_ref, gg_ref, ggt_ref, gn_ref, x_ref, omla_ref, om_ref, gate_ref, wout_ref,
                      nf_ref, out_ref, s_ref, og_scr, *, ts):
    j = pl.program_id(1)
    hd = HEADS * HEAD_DIM
    ntile = 2 * LANES

    @pl.when(j == 0)
    def _():
        s_ref[...] = jnp.zeros(s_ref.shape, F32)

    c = GDN_CHUNK
    c2 = 2 * c
    npairs = ts // c2
    row = lax.broadcasted_iota(jnp.int32, (c, c2), 0)
    lane = lax.broadcasted_iota(jnp.int32, (c, c2), 1)
    lo_half = lane < c
    col = jnp.where(lo_half, lane, lane - c)
    incl = row >= col
    strict = row > col
    eye = jnp.where(row == col, 1.0, 0.0)
    zeros_k = jnp.zeros((c, HEAD_DIM), BF16)
    units = [(p2, h) for p2 in range(npairs) for h in range(HEADS)]

    pre = {}
    for (p2, h) in units:
        r2 = slice(c2 * p2, c2 * (p2 + 1))
        gg = gg_ref[0, r2, :]
        gcol = gg[:, h:h + 1]
        beta = gg[:, HEADS + h:HEADS + h + 1]
        grow = ggt_ref[0, h:h + 1, r2]
        glast = (gg[c - 1:c, h:h + 1], gg[c2 - 1:c2, h:h + 1])
        glast_col = jnp.concatenate([jnp.broadcast_to(glast[0], (c, 1)), jnp.broadcast_to(glast[1], (c, 1))], axis=0)
        per_group = GDN_GROUP // HEAD_DIM
        gcols = slice(HEAD_DIM * (h % per_group), HEAD_DIM * (h % per_group + 1))
        q2 = qkv_ref[0, h // per_group, r2, gcols]
        k2 = qkv_ref[0, GDN_GROUPS // 3 + h // per_group, r2, gcols]
        kf = k2.astype(F32)
        kbeta = kf * beta
        kbeta_b = kbeta.astype(BF16)
        vbeta_b = (qkv_ref[0, 2 * (GDN_GROUPS // 3) + h // per_group, r2, gcols].astype(F32) * beta).astype(BF16)
        eg = jnp.exp(gcol)
        lhs = jnp.concatenate([jnp.concatenate([kbeta_b[:c], kbeta_b[c:]], axis=1),
                               jnp.concatenate([q2[:c], q2[c:]], axis=1)], axis=0)
        rhs = jnp.concatenate([jnp.concatenate([k2[:c], zeros_k], axis=1),
                               jnp.concatenate([zeros_k, k2[c:]], axis=1)], axis=0)
        gcol_pair = jnp.where(lo_half, gcol[:c], gcol[c:])
        pre[(p2, h)] = dict(
            lhs=lhs, rhs=rhs, glast=glast,
            vk=jnp.concatenate([vbeta_b, (kbeta * eg).astype(BF16)], axis=1),
            qg=(q2.astype(F32) * eg).astype(BF16),
            kdec=(kf * jnp.exp(glast_col - gcol)).astype(BF16),
            decay=jnp.exp(jnp.where(incl, gcol_pair - grow, -jnp.inf)))

    kq = {u: _dot_nt(pre[u]["lhs"], pre[u]["rhs"]) for u in units}
    a_pair = {u: kq[u][c:] * pre[u]["decay"] for u in units}
    m = {u: -jnp.where(strict, kq[u][:c] * pre[u]["decay"], 0.0) for u in units}

    p = {u: eye + m[u] for u in units}
    m = {u: _dot(m[u].astype(BF16), _pair_blockdiag(m[u], lo_half)) for u in units}
    for _ in range(int(np.log2(c)) - 2):
        pm = {u: _dot(jnp.concatenate([p[u], m[u]], axis=0).astype(BF16), _pair_blockdiag(m[u], lo_half))
              for u in units}
        p = {u: p[u] + pm[u][:c] for u in units}
        m = {u: pm[u][c:] for u in units}
    pm = {u: _dot(p[u].astype(BF16), _pair_blockdiag(m[u], lo_half)) for u in units}
    t_pair = {u: p[u] + pm[u] for u in units}

    uw = {u: _dot(_pair_blockdiag(t_pair[u], lo_half), pre[u]["vk"]) for u in units}
    a_chunks = {u: (a_pair[u][:, :c].astype(BF16), pltpu.roll(a_pair[u], c, 1)[:, :c].astype(BF16)) for u in units}

    a_mla = (omla_ref[0].astype(F32) * _silu(gate_ref[0, :, 0:hd].astype(F32))).astype(BF16)
    free_parts = [(a_mla, 0), (om_ref[0], 2 * hd)]
    pieces = [(part, n) for part in range(len(free_parts)) for n in range(D_MODEL // ntile)]
    partial = {}

    heads = range(HEADS)
    state = [s_ref[h] for h in heads]
    nsteps = 2 * npairs
    for step in range(nsteps):
        p2, ci = divmod(step, 2)
        rc = slice(c * ci, c * (ci + 1))
        rows = slice(c * step, c * (step + 1))
        ws_qs = [_dot(jnp.concatenate([uw[(p2, h)][rc, HEAD_DIM:].astype(BF16), pre[(p2, h)]["qg"][rc]], axis=0),
                      state[h].astype(BF16)) for h in heads]
        for part, n in pieces[len(pieces) * step // nsteps:len(pieces) * (step + 1) // nsteps]:
            operand, w_lo = free_parts[part]
            partial[(part, n)] = _dot(operand, wout_ref[w_lo:w_lo + hd, ntile * n:ntile * (n + 1)])
        v_new = [(uw[(p2, h)][rc, :HEAD_DIM] - ws_qs[h][:c]).astype(BF16) for h in heads]
        o_intra = [_dot(a_chunks[(p2, h)][ci], v_new[h]) for h in heads]
        ds = [_dot_tn(pre[(p2, h)]["kdec"][rc], v_new[h]) for h in heads]
        state = [state[h] * jnp.exp(pre[(p2, h)]["glast"][ci]) + ds[h] for h in heads]
        for h in heads:
            cols = slice(HEAD_DIM * h, HEAD_DIM * (h + 1))
            gate = _silu(gate_ref[0, rows, hd + HEAD_DIM * h:hd + HEAD_DIM * (h + 1)].astype(F32))
            og_scr[rows, cols] = (_rms(ws_qs[h][c:] + o_intra[h], gn_ref[...]) * gate).astype(BF16)
    for h in heads:
        s_ref[h] = state[h]

    og = og_scr[...]
    ys = []
    for n in range(D_MODEL // ntile):
        ncols = slice(ntile * n, ntile * (n + 1))
        acc = _dot(og, wout_ref[hd:2 * hd, ncols])
        for part in range(len(free_parts)):
            acc = acc + partial[(part, n)]
        ys.append(x_ref[0, :, ncols] + acc)
    ssq = functools.reduce(jnp.add, [jnp.sum(y * y, axis=-1, keepdims=True) for y in ys])
    inv = lax.rsqrt(ssq * (1.0 / D_MODEL) + EPS)
    for n, y in enumerate(ys):
        ncols = slice(ntile * n, ntile * (n + 1))
        out_ref[0, :, ncols] = y * inv * nf_ref[:, ncols]


def _const_spec(shape):
    nd = len(shape)
    return pl.BlockSpec(shape, lambda *_: (0,) * nd)


def _pack_weights(w_in_all, layer, w_q_b, w_kv_b):
    tr = _tile(D_MODEL, 256)
    w_all, w_groups = pl.pallas_call(
        _pack_kernel,
        grid=(D_MODEL // tr,),
        in_specs=[pl.BlockSpec((1, tr, w_in_all.shape[2]), lambda r: (layer, r, 0))],
        out_specs=[pl.BlockSpec((tr, C_END), lambda r: (r, 0)),
                   pl.BlockSpec((GDN_GROUPS, tr, GDN_GROUP), lambda r: (0, r, 0))],
        out_shape=[jax.ShapeDtypeStruct((D_MODEL, C_END), BF16),
                   jax.ShapeDtypeStruct((GDN_GROUPS, D_MODEL, GDN_GROUP), BF16)],
        compiler_params=pltpu.CompilerParams(dimension_semantics=("arbitrary",), vmem_limit_bytes=VMEM_LIMIT),
        name="pack",
    )(w_in_all)
    half = MLA_ROPE // 2
    wq = w_q_b.reshape(Q_LORA, HEADS, HEAD_DIM + MLA_ROPE)
    nope, ropec = wq[..., :HEAD_DIM], wq[..., HEAD_DIM:]
    wqb = jnp.concatenate([nope, ropec, ropec[..., half:], ropec[..., :half]], axis=-1)
    wqb = wqb.reshape(Q_LORA, HEADS * QK_WIDTH).astype(BF16)
    wkv = w_kv_b.reshape(KV_LORA, HEADS, 2 * HEAD_DIM)
    wkk = wkv[..., :HEAD_DIM].reshape(KV_LORA, HEADS * HEAD_DIM).astype(BF16)
    wkvv = wkv[..., HEAD_DIM:].reshape(KV_LORA, HEADS * HEAD_DIM).astype(BF16)
    return w_all, w_groups, wqb, wkk, wkvv


def _lane_row(vec):
    return jnp.zeros((1, LANES), F32).at[0, :vec.shape[0]].set(vec.astype(F32))


def _layer(x, mk, mv, cos, sin, norm_in, w_in_all, layer, q_a_norm, w_q_b, kv_a_norm, w_kv_b, gdn_conv,
           gdn_a_log, gdn_dt_bias, gdn_norm, w_out, out_gain, *, ts_proj, ts_gdn, tq):
    B, S, D = x.shape
    hd = HEADS * HEAD_DIM
    w_all, w_groups, wqb, wkk, wkvv = _pack_weights(w_in_all, layer, w_q_b, w_kv_b)
    conv_groups = gdn_conv.reshape(GDN_CONV, GDN_GROUPS, GDN_GROUP).transpose(1, 0, 2)
    M = mk.shape[1]
    arb2 = pltpu.CompilerParams(dimension_semantics=("arbitrary", "arbitrary"), vmem_limit_bytes=VMEM_LIMIT)

    def row_spec(width, ts):
        return pl.BlockSpec((1, ts, width), lambda b, j: (b, j, 0))

    def bs_shape(width, dtype=BF16):
        return jax.ShapeDtypeStruct((B, S, width), dtype)

    tile_spec = pl.BlockSpec((1, MLA_ROPE // 2, ts_proj), lambda b, j: (b * (S // ts_proj) + j, 0, 0))

    q, k, v, gqkv, gg, ggt, om, gate = pl.pallas_call(
        functools.partial(_proj_kernel, ts=ts_proj),
        grid=(B, S // ts_proj),
        in_specs=[row_spec(D, ts_proj), tile_spec, tile_spec,
                  _const_spec((1, D)), _const_spec(w_all.shape), _const_spec(w_groups.shape),
                  _const_spec((1, Q_LORA)), _const_spec(wqb.shape), _const_spec((1, KV_LORA)),
                  _const_spec(wkk.shape), _const_spec(wkvv.shape), _const_spec(conv_groups.shape),
                  _const_spec((1, LANES)), _const_spec((1, LANES)),
                  pl.BlockSpec((1, M, hd), lambda b, j: (b, 0, 0)), pl.BlockSpec((1, M, hd), lambda b, j: (b, 0, 0))],
        out_specs=[row_spec(HEADS * QK_WIDTH, ts_proj), row_spec(HEADS * QK_WIDTH, ts_proj), row_spec(hd, ts_proj),
                   pl.BlockSpec((1, GDN_GROUPS, ts_proj, GDN_GROUP), lambda b, j: (b, 0, j, 0)),
                   row_spec(LANES, ts_proj), pl.BlockSpec((1, SUBLANES, ts_proj), lambda b, j: (b, 0, j)),
                   row_spec(hd, ts_proj), row_spec(2 * hd, ts_proj)],
        out_shape=[bs_shape(HEADS * QK_WIDTH), bs_shape(HEADS * QK_WIDTH), bs_shape(hd),
                   jax.ShapeDtypeStruct((B, GDN_GROUPS, S, GDN_GROUP), BF16),
                   bs_shape(LANES, F32), jax.ShapeDtypeStruct((B, SUBLANES, S), F32),
                   bs_shape(hd), bs_shape(2 * hd)],
        scratch_shapes=[pltpu.VMEM((GDN_GROUPS, ts_proj + SUBLANES, GDN_GROUP), F32)],
        compiler_params=arb2,
        name="proj",
    )(x, cos, sin, norm_in.reshape(1, D), w_all, w_groups, q_a_norm.reshape(1, Q_LORA), wqb,
      kv_a_norm.reshape(1, KV_LORA), wkk, wkvv, conv_groups, _lane_row(gdn_a_log), _lane_row(gdn_dt_bias), mk, mv)

    o_mla = pl.pallas_call(
        functools.partial(_mla_kernel, tq=tq),
        grid=(B, S // tq),
        in_specs=[pl.BlockSpec((1, tq, HEADS * QK_WIDTH), lambda b, i: (b, i, 0)),
                  pl.BlockSpec((1, S, HEADS * QK_WIDTH), lambda b, i: (b, 0, 0)),
                  pl.BlockSpec((1, S, hd), lambda b, i: (b, 0, 0))],
        out_specs=pl.BlockSpec((1, tq, hd), lambda b, i: (b, i, 0)),
        out_shape=bs_shape(hd),
        scratch_shapes=[pltpu.VMEM((HEADS, tq, tq), F32), pltpu.VMEM((HEADS, tq, tq), BF16),
                        pltpu.VMEM((HEADS, tq, LANES), F32), pltpu.VMEM((HEADS, tq, LANES), F32),
                        pltpu.VMEM((HEADS, tq, LANES), F32), pltpu.VMEM((HEADS, tq, HEAD_DIM), F32)],
        compiler_params=arb2,
        name="mla",
    )(q, k, v)

    return pl.pallas_call(
        functools.partial(_gdn_merge_kernel, ts=ts_gdn),
        grid=(B, S // ts_gdn),
        in_specs=[pl.BlockSpec((1, GDN_GROUPS, ts_gdn, GDN_GROUP), lambda b, j: (b, 0, j, 0)), row_spec(LANES, ts_gdn),
                  pl.BlockSpec((1, SUBLANES, ts_gdn), lambda b, j: (b, 0, j)), _const_spec((1, HEAD_DIM)),
                  row_spec(D, ts_gdn), row_spec(hd, ts_gdn), row_spec(hd, ts_gdn), row_spec(2 * hd, ts_gdn),
                  _const_spec((D_MIX, D)), _const_spec((1, D))],
        out_specs=row_spec(D, ts_gdn),
        out_shape=bs_shape(D, F32),
        scratch_shapes=[pltpu.VMEM((HEADS, HEAD_DIM, HEAD_DIM), F32), pltpu.VMEM((ts_gdn, hd), BF16)],
        compiler_params=arb2,
        name="gdn_merge",
    )(gqkv, gg, ggt, gdn_norm.reshape(1, HEAD_DIM), x, o_mla, om, gate, w_out.astype(BF16),
      out_gain.reshape(1, D))


def _tile(n, pref):
    return pref if n % pref == 0 else n


def kernel(x, mem, positions, norm_in, w_in, q_a_norm, w_q_b, kv_a_norm, w_kv_b, gdn_conv, gdn_a_log,
           gdn_dt_bias, gdn_norm, mem_norm, w_mem_kv, w_out, norm_final):
    B, S, D = x.shape
    M = mem.shape[1]
    depth = norm_in.shape[0]
    assert depth == 1, "the final norm is fused into the single layer's last kernel"
    hd = HEADS * HEAD_DIM
    half = MLA_ROPE // 2
    inv_freq = 1.0 / (ROPE_THETA ** (jnp.arange(half, dtype=F32) / half))
    ts_proj = _tile(S, 512)
    n_tiles = B * S // ts_proj
    nt = _tile(n_tiles, 8)
    cos, sin = pl.pallas_call(
        _rope_kernel,
        grid=(n_tiles // nt,),
        in_specs=[pl.BlockSpec((nt, 1, ts_proj), lambda r: (r, 0, 0)), _const_spec((1, half, 1))],
        out_specs=[pl.BlockSpec((nt, half, ts_proj), lambda r: (r, 0, 0))] * 2,
        out_shape=[jax.ShapeDtypeStruct((n_tiles, half, ts_proj), F32)] * 2,
        compiler_params=pltpu.CompilerParams(dimension_semantics=("arbitrary",), vmem_limit_bytes=VMEM_LIMIT),
        name="rope",
    )(positions.reshape(n_tiles, 1, ts_proj), inv_freq.reshape(1, half, 1))
    l = 0
    nb_mem = _tile(B, 4)
    mk, mv = pl.pallas_call(
        _memkv_kernel,
        grid=(B // nb_mem,),
        in_specs=[pl.BlockSpec((nb_mem, M, D), lambda b: (b, 0, 0)), _const_spec((1, D)), _const_spec((D, 2 * hd))],
        out_specs=[pl.BlockSpec((nb_mem, M, hd), lambda b: (b, 0, 0))] * 2,
        out_shape=[jax.ShapeDtypeStruct((B, M, hd), BF16)] * 2,
        compiler_params=pltpu.CompilerParams(dimension_semantics=("arbitrary",), vmem_limit_bytes=VMEM_LIMIT),
        name="memkv",
    )(mem, mem_norm[l].reshape(1, D), w_mem_kv[l].astype(BF16))
    return _layer(x, mk, mv, cos, sin, norm_in[l], w_in, l, q_a_norm[l], w_q_b[l], kv_a_norm[l], w_kv_b[l],
                  gdn_conv[l], gdn_a_log[l], gdn_dt_bias[l], gdn_norm[l], w_out[l], norm_final,
                  ts_proj=ts_proj, ts_gdn=_tile(S, 512), tq=_tile(S, 512))
```

```python
import functools

import jax
import jax.numpy as jnp
import numpy as np
from jax import lax
from jax.experimental import pallas as pl
from jax.experimental.pallas import tpu as pltpu

F32 = jnp.float32
BF16 = jnp.bfloat16

D_MODEL = 1024
HEADS = 4
HEAD_DIM = 128
MLA_ROPE = 64
Q_LORA = 384
KV_LORA = 256
ROPE_THETA = 10000.0
GDN_CONV = 4
GDN_CHUNK = 64
GDN_QKV = 3 * HEADS * HEAD_DIM
D_MIX = 3 * HEADS * HEAD_DIM
IN_SPLITS = (Q_LORA, KV_LORA, MLA_ROPE, GDN_QKV, HEADS, HEADS, HEADS * HEAD_DIM, D_MIX)
EPS = 1e-6
MLA_SCALE = (HEAD_DIM + MLA_ROPE) ** -0.5
LOG2E = 1.4426950408889634
MEM_SCALE = HEAD_DIM ** -0.5
GDN_QSCALE = HEAD_DIM ** -0.5

LANES = 128
SUBLANES = 8
QK_WIDTH = 2 * LANES
MLA_STRIP = 64

C_CQ = 0
C_CKV = C_CQ + Q_LORA
C_KR = C_CKV + KV_LORA
C_AB = C_KR + LANES
C_MQ = C_AB + LANES
C_GATE = C_MQ + HEADS * HEAD_DIM
C_END = C_GATE + D_MIX

GDN_GROUP = 2 * LANES
GDN_GROUPS = GDN_QKV // GDN_GROUP

V7X_VMEM_BYTES = 64 * 1024 * 1024
VMEM_LIMIT = V7X_VMEM_BYTES * 7 // 8


def _dot(a, b, precision=None):
    return jnp.dot(a, b, preferred_element_type=F32, precision=precision)


def _dot_nt(a, b):
    return lax.dot_general(a, b, (((1,), (1,)), ((), ())), preferred_element_type=F32)


def _dot_tn(a, b):
    return lax.dot_general(a, b, (((0,), (0,)), ((), ())), preferred_element_type=F32)


def _rms(t, gain):
    return t * lax.rsqrt(jnp.mean(t * t, axis=-1, keepdims=True) + EPS) * gain


def _sigmoid(t):
    return 1.0 / (1.0 + jnp.exp(-t))


def _silu(t):
    half = 0.5 * t
    return half + half * jnp.tanh(half)


def _memkv_kernel(mem_ref, gain_ref, w_ref, mk_out, mv_out):
    nb, m, d = mem_ref.shape
    hm = _rms(mem_ref[...].reshape(nb * m, d), gain_ref[...]).astype(BF16)
    kv = _dot(hm, w_ref[...])
    half = HEADS * HEAD_DIM
    mk_out[...] = kv[:, :half].astype(BF16).reshape(nb, m, half)
    mv_out[...] = kv[:, half:].astype(BF16).reshape(nb, m, half)


def _pack_kernel(w_ref, wall_out, wg_out):
    o = [int(v) for v in np.cumsum((0,) + IN_SPLITS)]
    rows = w_ref.shape[1]
    lane = lax.broadcasted_iota(jnp.int32, (rows, LANES), 1)
    half = MLA_ROPE // 2
    xk = w_ref[0, :, o[2]:o[2] + LANES]
    kr2 = jnp.where(lane < MLA_ROPE, xk,
                    jnp.where(lane < MLA_ROPE + half, pltpu.roll(xk, half, 1), pltpu.roll(xk, LANES - half, 1)))
    xab = w_ref[0, :, o[4]:o[4] + LANES]
    wall_out[:, C_CQ:C_KR] = w_ref[0, :, o[0]:o[2]].astype(BF16)
    wall_out[:, C_KR:C_AB] = kr2.astype(BF16)
    wall_out[:, C_AB:C_MQ] = jnp.where(lane < 2 * HEADS, xab, 0.0).astype(BF16)
    wall_out[:, C_MQ:C_GATE] = w_ref[0, :, o[6]:o[7]].astype(BF16)
    wall_out[:, C_GATE:C_END] = w_ref[0, :, o[7]:o[8]].astype(BF16)
    for g in range(GDN_GROUPS):
        wg_out[g] = w_ref[0, :, o[3] + GDN_GROUP * g:o[3] + GDN_GROUP * (g + 1)].astype(BF16)


def _rope_kernel(pos_ref, invf_ref, cos_out, sin_out):
    ang = pos_ref[...].astype(F32) * invf_ref[...]
    cos_out[...] = jnp.cos(ang)
    sin_out[...] = jnp.sin(ang)


def _proj_kernel(x_ref, cos_ref, sin_ref, nin_ref, w_ref, wg_ref, qan_ref, wqb_ref, kvan_ref, wkk_ref, wkv_ref,
                 conv_ref, alog_ref, dtb_ref, mk_ref, mv_ref,
                 q_out, k_out, v_out, gqkv_out, gg_out, ggt_out, om_out, gate_out,
                 cbuf, *, ts):
    j = pl.program_id(1)
    hd = HEADS * HEAD_DIM
    hb = _rms(x_ref[0], nin_ref[...]).astype(BF16)

    def proj(lo, hi):
        return _dot(hb, w_ref[:, lo:hi])

    @pl.when(j == 0)
    def _():
        cbuf[:, 0:SUBLANES, :] = jnp.zeros((GDN_GROUPS, SUBLANES, GDN_GROUP), F32)

    @pl.when(j > 0)
    def _():
        cbuf[:, 0:SUBLANES, :] = cbuf[:, ts:ts + SUBLANES, :]

    def gdn_dot(g):
        cbuf[g, SUBLANES:ts + SUBLANES, :] = _dot(hb, wg_ref[g])

    def gdn_epilogue(g):
        taps = conv_ref[g]
        for sub in range(GDN_GROUP // LANES):
            cols = slice(LANES * sub, LANES * (sub + 1))
            blk = cbuf[g, :, cols]
            acc = taps[GDN_CONV - 1:GDN_CONV, cols] * blk[SUBLANES:]
            for back in range(1, GDN_CONV):
                acc = acc + taps[GDN_CONV - 1 - back:GDN_CONV - back, cols] * pltpu.roll(blk, back, 0)[SUBLANES:]
            y = _silu(acc)
            if g < 2 * (GDN_GROUPS // 3):
                inv = lax.rsqrt(jnp.sum(y * y, axis=-1, keepdims=True) + EPS)
                y = y * (inv * GDN_QSCALE if g < GDN_GROUPS // 3 else inv)
            gqkv_out[0, g, :, cols] = y.astype(BF16)

    narrow = proj(C_CQ, C_MQ)
    cq = narrow[:, C_CQ:C_CKV]
    ckv = narrow[:, C_CKV:C_KR]
    kr_raw = narrow[:, C_KR:C_AB]
    ab = narrow[:, C_AB:C_MQ]
    cqn = _rms(cq, qan_ref[...]).astype(BF16)
    gdn_dot(0)
    ckvn = _rms(ckv, kvan_ref[...]).astype(BF16)
    qf = _dot(cqn, wqb_ref[...])

    c32, s32 = cos_ref[0].T, sin_ref[0].T
    zpad = jnp.zeros((ts, LANES - MLA_ROPE), F32)
    cosm = jnp.concatenate([c32, c32, zpad], axis=1)
    sinm = jnp.concatenate([-s32, s32, zpad], axis=1)

    def rope(r):
        return r * cosm + pltpu.roll(r, MLA_ROPE, 1) * sinm

    gdn_epilogue(0)
    gdn_dot(1)
    kn = _dot(ckvn, wkk_ref[...])
    vv = _dot(ckvn, wkv_ref[...])
    mq = proj(C_MQ, C_GATE)
    qscale = MLA_SCALE * LOG2E
    for h in range(HEADS):
        lo = QK_WIDTH * h
        q_out[0, :, lo:lo + LANES] = (qf[:, lo:lo + LANES] * qscale).astype(BF16)
        q_out[0, :, lo + LANES:lo + QK_WIDTH] = (rope(qf[:, lo + LANES:lo + QK_WIDTH]) * qscale).astype(BF16)
    gdn_epilogue(1)
    gdn_dot(2)
    gate_out[0, :, 0:hd] = proj(C_GATE, C_GATE + hd).astype(BF16)
    v_out[0] = vv.astype(BF16)
    kr = rope(kr_raw).astype(BF16)
    for h in range(HEADS):
        lo = QK_WIDTH * h
        k_out[0, :, lo:lo + LANES] = kn[:, LANES * h:LANES * (h + 1)].astype(BF16)
        k_out[0, :, lo + LANES:lo + QK_WIDTH] = kr
    gdn_epilogue(2)
    gdn_dot(3)
    gate_out[0, :, hd:2 * hd] = proj(C_GATE + hd, C_GATE + 2 * hd).astype(BF16)

    mscale = MEM_SCALE * LOG2E
    sc, pp, ll, oo = {}, {}, {}, {}

    def mem_qk(h):
        cols = slice(HEAD_DIM * h, HEAD_DIM * (h + 1))
        sc[h] = _dot_nt((mq[:, cols] * mscale).astype(BF16), mk_ref[0, :, cols])

    def mem_softmax(h):
        p = jnp.exp2(sc[h] - jnp.max(sc[h], axis=-1, keepdims=True))
        ll[h] = jnp.sum(p, axis=-1, keepdims=True)
        pp[h] = p.astype(BF16)

    def mem_pv(h):
        oo[h] = _dot(pp[h], mv_ref[0, :, HEAD_DIM * h:HEAD_DIM * (h + 1)])

    mem_qk(0)
    z = ab + dtb_ref[...]
    softplus = jnp.maximum(z, 0.0) + jnp.log1p(jnp.exp(-jnp.abs(z)))
    gcum = -jnp.exp(alog_ref[...]) * softplus
    lane = lax.broadcasted_iota(jnp.int32, ab.shape, 1)
    pos_in_chunk = lax.broadcasted_iota(jnp.int32, ab.shape, 0) % GDN_CHUNK
    shift = 1
    while shift < GDN_CHUNK:
        gcum = gcum + jnp.where(pos_in_chunk >= shift, pltpu.roll(gcum, shift, 0), 0.0)
        shift *= 2
    gg_out[0] = jnp.where(lane < HEADS, gcum, _sigmoid(ab))
    ggt_out[0] = gcum.T[0:SUBLANES, :]

    gdn_epilogue(3)
    gdn_dot(4)
    mem_qk(1)
    mem_softmax(0)
    gdn_epilogue(4)
    gdn_dot(5)
    mem_qk(2)
    mem_pv(0)
    mem_softmax(1)
    gdn_epilogue(5)
    mem_qk(3)
    mem_pv(1)
    mem_softmax(2)
    mem_pv(2)
    mem_softmax(3)
    gmem = _silu(proj(C_GATE + 2 * hd, C_END))
    mem_pv(3)
    for h in range(HEADS):
        cols = slice(HEAD_DIM * h, HEAD_DIM * (h + 1))
        om_out[0, :, cols] = (oo[h] / ll[h] * gmem[:, cols]).astype(BF16)


def _mla_kernel(q_ref, k_ref, v_ref, o_ref, s_scr, p_scr, m_scr, l_scr, a_scr, acc_scr, *, tq):
    i = pl.program_id(1)
    heads = range(HEADS)
    strip = MLA_STRIP

    def run(n_full):
        half = tq // 2
        items = [(h, r0, half, r0 + half, n_full * tq, True) for h in heads for r0 in (0, half)]
        items += [(h, 0, tq, tq, jk * tq, False) for jk in range(n_full) for h in heads]

        def scores(item):
            h, r0, nr, nk, start, _ = item
            s_scr[h, r0:r0 + nr, 0:nk] = _dot_nt(q_ref[0, r0:r0 + nr, QK_WIDTH * h:QK_WIDTH * (h + 1)],
                                                 k_ref[0, start:start + nk, QK_WIDTH * h:QK_WIDTH * (h + 1)])

        def load_strip(item, r):
            h, r0, nr, nk, _, diagonal = item
            s = s_scr[h, r0 + strip * r:r0 + strip * (r + 1), 0:nk]
            if diagonal:
                row = lax.broadcasted_iota(jnp.int32, s.shape, 0) + (r0 + strip * r)
                col = lax.broadcasted_iota(jnp.int32, s.shape, 1)
                s = jnp.where(row >= col, s, -jnp.inf)
            return s

        def lane_tiles(t):
            return [t[:, LANES * g:LANES * (g + 1)] for g in range(t.shape[1] // LANES)]

        def softmax(item):
            h, r0, nr, nk, _, diagonal = item
            strips = range(nr // strip)
            rows = [slice(r0 + strip * r, r0 + strip * (r + 1)) for r in strips]
            part = [functools.reduce(jnp.maximum, lane_tiles(load_strip(item, r))) for r in strips]
            peak = [jnp.broadcast_to(jnp.max(t, axis=-1, keepdims=True), (strip, LANES)) for t in part]
            if diagonal:
                m_new = peak
            else:
                m_old = [m_scr[h, rows[r], :] for r in strips]
                m_new = [jnp.maximum(m_old[r], peak[r]) for r in strips]
                alpha = [jnp.exp2(m_old[r] - m_new[r]) for r in strips]
                for r in strips:
                    a_scr[h, rows[r], :] = alpha[r]
            for r in strips:
                m_scr[h, rows[r], :] = m_new[r]
            part = []
            for r in strips:
                p = [jnp.exp2(t - m_new[r]) for t in lane_tiles(load_strip(item, r))]
                for g, t in enumerate(p):
                    p_scr[h, rows[r], LANES * g:LANES * (g + 1)] = t.astype(BF16)
                part.append(functools.reduce(jnp.add, p))
            total = [jnp.broadcast_to(jnp.sum(t, axis=-1, keepdims=True), (strip, LANES)) for t in part]
            for r in strips:
                l_scr[h, rows[r], :] = total[r] if diagonal else alpha[r] * l_scr[h, rows[r], :] + total[r]

        def values(item):
            h, r0, nr, nk, start, diagonal = item
            pv = _dot(p_scr[h, r0:r0 + nr, 0:nk], v_ref[0, start:start + nk, HEAD_DIM * h:HEAD_DIM * (h + 1)])
            if diagonal:
                acc_scr[h, r0:r0 + nr, :] = pv
            else:
                acc_scr[h, r0:r0 + nr, :] = a_scr[h, r0:r0 + nr, :] * acc_scr[h, r0:r0 + nr, :] + pv

        for t in range(len(items) + 3):
            if t < len(items):
                scores(items[t])
            if 0 <= t - 1 < len(items):
                softmax(items[t - 1])
            if 0 <= t - 3 < len(items):
                values(items[t - 3])

        for h in heads:
            o_ref[0, :, HEAD_DIM * h:HEAD_DIM * (h + 1)] = (acc_scr[h] / l_scr[h]).astype(BF16)

    for n_full in range(k_ref.shape[1] // tq):
        pl.when(i == n_full)(functools.partial(run, n_full))


def _pair_blockdiag(t, lo_half):
    return jnp.concatenate([jnp.where(lo_half, t, 0.0), jnp.where(lo_half, 0.0, t)], axis=0).astype(BF16)


def _gdn_merge_kernel(qkv_ref, gg_ref, ggt_ref, gn_ref, x_ref, omla_ref, om_ref, gate_ref, wout_ref,
                      nf_ref, out_ref, s_ref, og_scr, *, ts):
    j = pl.program_id(1)
    hd = HEADS * HEAD_DIM
    ntile = 2 * LANES

    @pl.when(j == 0)
    def _():
        s_ref[...] = jnp.zeros(s_ref.shape, F32)

    c = GDN_CHUNK
    c2 = 2 * c
    npairs = ts // c2
    row = lax.broadcasted_iota(jnp.int32, (c, c2), 0)
    lane = lax.broadcasted_iota(jnp.int32, (c, c2), 1)
    lo_half = lane < c
    col = jnp.where(lo_half, lane, lane - c)
    incl = row >= col
    strict = row > col
    eye = jnp.where(row == col, 1.0, 0.0)
    zeros_k = jnp.zeros((c, HEAD_DIM), BF16)
    units = [(p2, h) for p2 in range(npairs) for h in range(HEADS)]

    pre = {}
    for (p2, h) in units:
        r2 = slice(c2 * p2, c2 * (p2 + 1))
        gg = gg_ref[0, r2, :]
        gcol = gg[:, h:h + 1]
        beta = gg[:, HEADS + h:HEADS + h + 1]
        grow = ggt_ref[0, h:h + 1, r2]
        glast = (gg[c - 1:c, h:h + 1], gg[c2 - 1:c2, h:h + 1])
        glast_col = jnp.concatenate([jnp.broadcast_to(glast[0], (c, 1)), jnp.broadcast_to(glast[1], (c, 1))], axis=0)
        per_group = GDN_GROUP // HEAD_DIM
        gcols = slice(HEAD_DIM * (h % per_group), HEAD_DIM * (h % per_group + 1))
        q2 = qkv_ref[0, h // per_group, r2, gcols]
        k2 = qkv_ref[0, GDN_GROUPS // 3 + h // per_group, r2, gcols]
        kf = k2.astype(F32)
        kbeta = kf * beta
        kbeta_b = kbeta.astype(BF16)
        vbeta_b = (qkv_ref[0, 2 * (GDN_GROUPS // 3) + h // per_group, r2, gcols].astype(F32) * beta).astype(BF16)
        eg = jnp.exp(gcol)
        lhs = jnp.concatenate([jnp.concatenate([kbeta_b[:c], kbeta_b[c:]], axis=1),
                               jnp.concatenate([q2[:c], q2[c:]], axis=1)], axis=0)
        rhs = jnp.concatenate([jnp.concatenate([k2[:c], zeros_k], axis=1),
                               jnp.concatenate([zeros_k, k2[c:]], axis=1)], axis=0)
        gcol_pair = jnp.where(lo_half, gcol[:c], gcol[c:])
        pre[(p2, h)] = dict(
            lhs=lhs, rhs=rhs, glast=glast,
            vk=jnp.concatenate([vbeta_b, (kbeta * eg).astype(BF16)], axis=1),
            qg=(q2.astype(F32) * eg).astype(BF16),
            kdec=(kf * jnp.exp(glast_col - gcol)).astype(BF16),
            decay=jnp.exp(jnp.where(incl, gcol_pair - grow, -jnp.inf)))

    kq = {u: _dot_nt(pre[u]["lhs"], pre[u]["rhs"]) for u in units}
    a_pair = {u: kq[u][c:] * pre[u]["decay"] for u in units}
    m = {u: -jnp.where(strict, kq[u][:c] * pre[u]["decay"], 0.0) for u in units}

    p = {u: eye + m[u] for u in units}
    m = {u: _dot(m[u].astype(BF16), _pair_blockdiag(m[u], lo_half)) for u in units}
    for _ in range(int(np.log2(c)) - 2):
        pm = {u: _dot(jnp.concatenate([p[u], m[u]], axis=0).astype(BF16), _pair_blockdiag(m[u], lo_half))
              for u in units}
        p = {u: p[u] + pm[u][:c] for u in units}
        m = {u: pm[u][c:] for u in units}
    pm = {u: _dot(p[u].astype(BF16), _pair_blockdiag(m[u], lo_half)) for u in units}
    t_pair = {u: p[u] + pm[u] for u in units}

    uw = {u: _dot(_pair_blockdiag(t_pair[u], lo_half), pre[u]["vk"]) for u in units}
    a_chunks = {u: (a_pair[u][:, :c].astype(BF16), pltpu.roll(a_pair[u], c, 1)[:, :c].astype(BF16)) for u in units}

    a_mla = (omla_ref[0].astype(F32) * _silu(gate_ref[0, :, 0:hd].astype(F32))).astype(BF16)
    free_parts = [(a_mla, 0), (om_ref[0], 2 * hd)]
    pieces = [(part, n) for part in range(len(free_parts)) for n in range(D_MODEL // ntile)]
    partial = {}

    heads = range(HEADS)
    state = [s_ref[h] for h in heads]
    nsteps = 2 * npairs
    for step in range(nsteps):
        p2, ci = divmod(step, 2)
        rc = slice(c * ci, c * (ci + 1))
        rows = slice(c * step, c * (step + 1))
        ws_qs = [_dot(jnp.concatenate([uw[(p2, h)][rc, HEAD_DIM:].astype(BF16), pre[(p2, h)]["qg"][rc]], axis=0),
                      state[h].astype(BF16)) for h in heads]
        for part, n in pieces[len(pieces) * step // nsteps:len(pieces) * (step + 1) // nsteps]:
            operand, w_lo = free_parts[part]
            partial[(part, n)] = _dot(operand, wout_ref[w_lo:w_lo + hd, ntile * n:ntile * (n + 1)])
        v_new = [(uw[(p2, h)][rc, :HEAD_DIM] - ws_qs[h][:c]).astype(BF16) for h in heads]
        o_intra = [_dot(a_chunks[(p2, h)][ci], v_new[h]) for h in heads]
        ds = [_dot_tn(pre[(p2, h)]["kdec"][rc], v_new[h]) for h in heads]
        state = [state[h] * jnp.exp(pre[(p2, h)]["glast"][ci]) + ds[h] for h in heads]
        for h in heads:
            cols = slice(HEAD_DIM * h, HEAD_DIM * (h + 1))
            gate = _silu(gate_ref[0, rows, hd + HEAD_DIM * h:hd + HEAD_DIM * (h + 1)].astype(F32))
            og_scr[rows, cols] = (_rms(ws_qs[h][c:] + o_intra[h], gn_ref[...]) * gate).astype(BF16)
    for h in heads:
        s_ref[h] = state[h]

    og = og_scr[...]
    ys = []
    for n in range(D_MODEL // ntile):
        ncols = slice(ntile * n, ntile * (n + 1))
        acc = _dot(og, wout_ref[hd:2 * hd, ncols])
        for part in range(len(free_parts)):
            acc = acc + partial[(part, n)]
        ys.append(x_ref[0, :, ncols] + acc)
    ssq = functools.reduce(jnp.add, [jnp.sum(y * y, axis=-1, keepdims=True) for y in ys])
    inv = lax.rsqrt(ssq * (1.0 / D_MODEL) + EPS)
    for n, y in enumerate(ys):
        ncols = slice(ntile * n, ntile * (n + 1))
        out_ref[0, :, ncols] = y * inv * nf_ref[:, ncols]


def _const_spec(shape):
    nd = len(shape)
    return pl.BlockSpec(shape, lambda *_: (0,) * nd)


def _pack_weights(w_in_all, layer, w_q_b, w_kv_b):
    tr = _tile(D_MODEL, 256)
    w_all, w_groups = pl.pallas_call(
        _pack_kernel,
        grid=(D_MODEL // tr,),
        in_specs=[pl.BlockSpec((1, tr, w_in_all.shape[2]), lambda r: (layer, r, 0))],
        out_specs=[pl.BlockSpec((tr, C_END), lambda r: (r, 0)),
                   pl.BlockSpec((GDN_GROUPS, tr, GDN_GROUP), lambda r: (0, r, 0))],
        out_shape=[jax.ShapeDtypeStruct((D_MODEL, C_END), BF16),
                   jax.ShapeDtypeStruct((GDN_GROUPS, D_MODEL, GDN_GROUP), BF16)],
        compiler_params=pltpu.CompilerParams(dimension_semantics=("arbitrary",), vmem_limit_bytes=VMEM_LIMIT),
        name="pack",
    )(w_in_all)
    half = MLA_ROPE // 2
    wq = w_q_b.reshape(Q_LORA, HEADS, HEAD_DIM + MLA_ROPE)
    nope, ropec = wq[..., :HEAD_DIM], wq[..., HEAD_DIM:]
    wqb = jnp.concatenate([nope, ropec, ropec[..., half:], ropec[..., :half]], axis=-1)
    wqb = wqb.reshape(Q_LORA, HEADS * QK_WIDTH).astype(BF16)
    wkv = w_kv_b.reshape(KV_LORA, HEADS, 2 * HEAD_DIM)
    wkk = wkv[..., :HEAD_DIM].reshape(KV_LORA, HEADS * HEAD_DIM).astype(BF16)
    wkvv = wkv[..., HEAD_DIM:].reshape(KV_LORA, HEADS * HEAD_DIM).astype(BF16)
    return w_all, w_groups, wqb, wkk, wkvv


def _lane_row(vec):
    return jnp.zeros((1, LANES), F32).at[0, :vec.shape[0]].set(vec.astype(F32))


def _layer(x, mk, mv, cos, sin, norm_in, w_in_all, layer, q_a_norm, w_q_b, kv_a_norm, w_kv_b, gdn_conv,
           gdn_a_log, gdn_dt_bias, gdn_norm, w_out, out_gain, *, ts_proj, ts_gdn, tq):
    B, S, D = x.shape
    hd = HEADS * HEAD_DIM
    w_all, w_groups, wqb, wkk, wkvv = _pack_weights(w_in_all, layer, w_q_b, w_kv_b)
    conv_groups = gdn_conv.reshape(GDN_CONV, GDN_GROUPS, GDN_GROUP).transpose(1, 0, 2)
    M = mk.shape[1]
    arb2 = pltpu.CompilerParams(dimension_semantics=("arbitrary", "arbitrary"), vmem_limit_bytes=VMEM_LIMIT)

    def row_spec(width, ts):
        return pl.BlockSpec((1, ts, width), lambda b, j: (b, j, 0))

    def bs_shape(width, dtype=BF16):
        return jax.ShapeDtypeStruct((B, S, width), dtype)

    tile_spec = pl.BlockSpec((1, MLA_ROPE // 2, ts_proj), lambda b, j: (b * (S // ts_proj) + j, 0, 0))

    q, k, v, gqkv, gg, ggt, om, gate = pl.pallas_call(
        functools.partial(_proj_kernel, ts=ts_proj),
        grid=(B, S // ts_proj),
        in_specs=[row_spec(D, ts_proj), tile_spec, tile_spec,
                  _const_spec((1, D)), _const_spec(w_all.shape), _const_spec(w_groups.shape),
                  _const_spec((1, Q_LORA)), _const_spec(wqb.shape), _const_spec((1, KV_LORA)),
                  _const_spec(wkk.shape), _const_spec(wkvv.shape), _const_spec(conv_groups.shape),
                  _const_spec((1, LANES)), _const_spec((1, LANES)),
                  pl.BlockSpec((1, M, hd), lambda b, j: (b, 0, 0)), pl.BlockSpec((1, M, hd), lambda b, j: (b, 0, 0))],
        out_specs=[row_spec(HEADS * QK_WIDTH, ts_proj), row_spec(HEADS * QK_WIDTH, ts_proj), row_spec(hd, ts_proj),
                   pl.BlockSpec((1, GDN_GROUPS, ts_proj, GDN_GROUP), lambda b, j: (b, 0, j, 0)),
                   row_spec(LANES, ts_proj), pl.BlockSpec((1, SUBLANES, ts_proj), lambda b, j: (b, 0, j)),
                   row_spec(hd, ts_proj), row_spec(2 * hd, ts_proj)],
        out_shape=[bs_shape(HEADS * QK_WIDTH), bs_shape(HEADS * QK_WIDTH), bs_shape(hd),
                   jax.ShapeDtypeStruct((B, GDN_GROUPS, S, GDN_GROUP), BF16),
                   bs_shape(LANES, F32), jax.ShapeDtypeStruct((B, SUBLANES, S), F32),
                   bs_shape(hd), bs_shape(2 * hd)],
        scratch_shapes=[pltpu.VMEM((GDN_GROUPS, ts_proj + SUBLANES, GDN_GROUP), F32)],
        compiler_params=arb2,
        name="proj",
    )(x, cos, sin, norm_in.reshape(1, D), w_all, w_groups, q_a_norm.reshape(1, Q_LORA), wqb,
      kv_a_norm.reshape(1, KV_LORA), wkk, wkvv, conv_groups, _lane_row(gdn_a_log), _lane_row(gdn_dt_bias), mk, mv)

    o_mla = pl.pallas_call(
        functools.partial(_mla_kernel, tq=tq),
        grid=(B, S // tq),
        in_specs=[pl.BlockSpec((1, tq, HEADS * QK_WIDTH), lambda b, i: (b, i, 0)),
                  pl.BlockSpec((1, S, HEADS * QK_WIDTH), lambda b, i: (b, 0, 0)),
                  pl.BlockSpec((1, S, hd), lambda b, i: (b, 0, 0))],
        out_specs=pl.BlockSpec((1, tq, hd), lambda b, i: (b, i, 0)),
        out_shape=bs_shape(hd),
        scratch_shapes=[pltpu.VMEM((HEADS, tq, tq), F32), pltpu.VMEM((HEADS, tq, tq), BF16),
                        pltpu.VMEM((HEADS, tq, LANES), F32), pltpu.VMEM((HEADS, tq, LANES), F32),
                        pltpu.VMEM((HEADS, tq, LANES), F32), pltpu.VMEM((HEADS, tq, HEAD_DIM), F32)],
        compiler_params=arb2,
        name="mla",
    )(q, k, v)

    return pl.pallas_call(
        functools.partial(_gdn_merge_kernel, ts=ts_gdn),
        grid=(B, S // ts_gdn),
        in_specs=[pl.BlockSpec((1, GDN_GROUPS, ts_gdn, GDN_GROUP), lambda b, j: (b, 0, j, 0)), row_spec(LANES, ts_gdn),
                  pl.BlockSpec((1, SUBLANES, ts_gdn), lambda b, j: (b, 0, j)), _const_spec((1, HEAD_DIM)),
                  row_spec(D, ts_gdn), row_spec(hd, ts_gdn), row_spec(hd, ts_gdn), row_spec(2 * hd, ts_gdn),
                  _const_spec((D_MIX, D)), _const_spec((1, D))],
        out_specs=row_spec(D, ts_gdn),
        out_shape=bs_shape(D, F32),
        scratch_shapes=[pltpu.VMEM((HEADS, HEAD_DIM, HEAD_DIM), F32), pltpu.VMEM((ts_gdn, hd), BF16)],
        compiler_params=arb2,
        name="gdn_merge",
    )(gqkv, gg, ggt, gdn_norm.reshape(1, HEAD_DIM), x, o_mla, om, gate, w_out.astype(BF16),
      out_gain.reshape(1, D))


def _tile(n, pref):
    return pref if n % pref == 0 else n


def kernel(x, mem, positions, norm_in, w_in, q_a_norm, w_q_b, kv_a_norm, w_kv_b, gdn_conv, gdn_a_log,
           gdn_dt_bias, gdn_norm, mem_norm, w_mem_kv, w_out, norm_final):
    B, S, D = x.shape
    M = mem.shape[1]
    depth = norm_in.shape[0]
    assert depth == 1, "the final norm is fused into the single layer's last kernel"
    hd = HEADS * HEAD_DIM
    half = MLA_ROPE // 2
    inv_freq = 1.0 / (ROPE_THETA ** (jnp.arange(half, dtype=F32) / half))
    ts_proj = _tile(S, 512)
    n_tiles = B * S // ts_proj
    nt = _tile(n_tiles, 8)
    cos, sin = pl.pallas_call(
        _rope_kernel,
        grid=(n_tiles // nt,),
        in_specs=[pl.BlockSpec((nt, 1, ts_proj), lambda r: (r, 0, 0)), _const_spec((1, half, 1))],
        out_specs=[pl.BlockSpec((nt, half, ts_proj), lambda r: (r, 0, 0))] * 2,
        out_shape=[jax.ShapeDtypeStruct((n_tiles, half, ts_proj), F32)] * 2,
        compiler_params=pltpu.CompilerParams(dimension_semantics=("arbitrary",), vmem_limit_bytes=VMEM_LIMIT),
        name="rope",
    )(positions.reshape(n_tiles, 1, ts_proj), inv_freq.reshape(1, half, 1))
    l = 0
    nb_mem = _tile(B, 4)
    mk, mv = pl.pallas_call(
        _memkv_kernel,
        grid=(B // nb_mem,),
        in_specs=[pl.BlockSpec((nb_mem, M, D), lambda b: (b, 0, 0)), _const_spec((1, D)), _const_spec((D, 2 * hd))],
        out_specs=[pl.BlockSpec((nb_mem, M, hd), lambda b: (b, 0, 0))] * 2,
        out_shape=[jax.ShapeDtypeStruct((B, M, hd), BF16)] * 2,
        compiler_params=pltpu.CompilerParams(dimension_semantics=("arbitrary",), vmem_limit_bytes=VMEM_LIMIT),
        name="memkv",
    )(mem, mem_norm[l].reshape(1, D), w_mem_kv[l].astype(BF16))
    return _layer(x, mk, mv, cos, sin, norm_in[l], w_in, l, q_a_norm[l], w_q_b[l], kv_a_norm[l], w_kv_b[l],
                  gdn_conv[l], gdn_a_log[l], gdn_dt_bias[l], gdn_norm[l], w_out[l], norm_final,
                  ts_proj=ts_proj, ts_gdn=_tile(S, 512), tq=_tile(S, 512))
```

```python
import functools

import jax
import jax.numpy as jnp
import numpy as np
from jax import lax
from jax.experimental import pallas as pl
from jax.experimental.pallas import tpu as pltpu

F32 = jnp.float32
BF16 = jnp.bfloat16

D_MODEL = 1024
HEADS = 4
HEAD_DIM = 128
MLA_ROPE = 64
Q_LORA = 384
KV_LORA = 256
ROPE_THETA = 10000.0
GDN_CONV = 4
GDN_CHUNK = 64
GDN_QKV = 3 * HEADS * HEAD_DIM
D_MIX = 3 * HEADS * HEAD_DIM
IN_SPLITS = (Q_LORA, KV_LORA, MLA_ROPE, GDN_QKV, HEADS, HEADS, HEADS * HEAD_DIM, D_MIX)
EPS = 1e-6
MLA_SCALE = (HEAD_DIM + MLA_ROPE) ** -0.5
LOG2E = 1.4426950408889634
MEM_SCALE = HEAD_DIM ** -0.5
GDN_QSCALE = HEAD_DIM ** -0.5

LANES = 128
SUBLANES = 8
QK_WIDTH = 2 * LANES
MLA_STRIP = 64

C_CQ = 0
C_CKV = C_CQ + Q_LORA
C_KR = C_CKV + KV_LORA
C_AB = C_KR + LANES
C_MQ = C_AB + LANES
C_GATE = C_MQ + HEADS * HEAD_DIM
C_END = C_GATE + D_MIX

GDN_GROUP = 2 * LANES
GDN_GROUPS = GDN_QKV // GDN_GROUP

V7X_VMEM_BYTES = 64 * 1024 * 1024
VMEM_LIMIT = V7X_VMEM_BYTES * 7 // 8


def _dot(a, b, precision=None):
    return jnp.dot(a, b, preferred_element_type=F32, precision=precision)


def _dot_nt(a, b):
    return lax.dot_general(a, b, (((1,), (1,)), ((), ())), preferred_element_type=F32)


def _dot_tn(a, b):
    return lax.dot_general(a, b, (((0,), (0,)), ((), ())), preferred_element_type=F32)


def _rms(t, gain):
    return t * lax.rsqrt(jnp.mean(t * t, axis=-1, keepdims=True) + EPS) * gain


def _sigmoid(t):
    return 1.0 / (1.0 + jnp.exp(-t))


def _silu(t):
    half = 0.5 * t
    return half + half * jnp.tanh(half)


def _memkv_kernel(mem_ref, gain_ref, w_ref, mk_out, mv_out):
    nb, m, d = mem_ref.shape
    hm = _rms(mem_ref[...].reshape(nb * m, d), gain_ref[...]).astype(BF16)
    kv = _dot(hm, w_ref[...])
    half = HEADS * HEAD_DIM
    mk_out[...] = kv[:, :half].astype(BF16).reshape(nb, m, half)
    mv_out[...] = kv[:, half:].astype(BF16).reshape(nb, m, half)


def _pack_kernel(w_ref, wall_out, wg_out):
    o = [int(v) for v in np.cumsum((0,) + IN_SPLITS)]
    rows = w_ref.shape[1]
    lane = lax.broadcasted_iota(jnp.int32, (rows, LANES), 1)
    half = MLA_ROPE // 2
    xk = w_ref[0, :, o[2]:o[2] + LANES]
    kr2 = jnp.where(lane < MLA_ROPE, xk,
                    jnp.where(lane < MLA_ROPE + half, pltpu.roll(xk, half, 1), pltpu.roll(xk, LANES - half, 1)))
    xab = w_ref[0, :, o[4]:o[4] + LANES]
    wall_out[:, C_CQ:C_KR] = w_ref[0, :, o[0]:o[2]].astype(BF16)
    wall_out[:, C_KR:C_AB] = kr2.astype(BF16)
    wall_out[:, C_AB:C_MQ] = jnp.where(lane < 2 * HEADS, xab, 0.0).astype(BF16)
    wall_out[:, C_MQ:C_GATE] = w_ref[0, :, o[6]:o[7]].astype(BF16)
    wall_out[:, C_GATE:C_END] = w_ref[0, :, o[7]:o[8]].astype(BF16)
    for g in range(GDN_GROUPS):
        wg_out[g] = w_ref[0, :, o[3] + GDN_GROUP * g:o[3] + GDN_GROUP * (g + 1)].astype(BF16)


def _rope_kernel(pos_ref, invf_ref, cos_out, sin_out):
    ang = pos_ref[...].astype(F32) * invf_ref[...]
    cos_out[...] = jnp.cos(ang)
    sin_out[...] = jnp.sin(ang)


def _proj_kernel(x_ref, cos_ref, sin_ref, nin_ref, w_ref, wg_ref, qan_ref, wqb_ref, kvan_ref, wkk_ref, wkv_ref,
                 conv_ref, alog_ref, dtb_ref, mk_ref, mv_ref,
                 q_out, k_out, v_out, gqkv_out, gg_out, ggt_out, om_out, gate_out,
                 cbuf, *, ts):
    j = pl.program_id(1)
    hd = HEADS * HEAD_DIM
    hb = _rms(x_ref[0], nin_ref[...]).astype(BF16)

    def proj(lo, hi):
        return _dot(hb, w_ref[:, lo:hi])

    @pl.when(j == 0)
    def _():
        cbuf[:, 0:SUBLANES, :] = jnp.zeros((GDN_GROUPS, SUBLANES, GDN_GROUP), F32)

    @pl.when(j > 0)
    def _():
        cbuf[:, 0:SUBLANES, :] = cbuf[:, ts:ts + SUBLANES, :]

    def gdn_dot(g):
        cbuf[g, SUBLANES:ts + SUBLANES, :] = _dot(hb, wg_ref[g])

    def gdn_epilogue(g):
        taps = conv_ref[g]
        for sub in range(GDN_GROUP // LANES):
            cols = slice(LANES * sub, LANES * (sub + 1))
            blk = cbuf[g, :, cols]
            acc = taps[GDN_CONV - 1:GDN_CONV, cols] * blk[SUBLANES:]
            for back in range(1, GDN_CONV):
                acc = acc + taps[GDN_CONV - 1 - back:GDN_CONV - back, cols] * pltpu.roll(blk, back, 0)[SUBLANES:]
            y = _silu(acc)
            if g < 2 * (GDN_GROUPS // 3):
                inv = lax.rsqrt(jnp.sum(y * y, axis=-1, keepdims=True) + EPS)
                y = y * (inv * GDN_QSCALE if g < GDN_GROUPS // 3 else inv)
            gqkv_out[0, g, :, cols] = y.astype(BF16)

    narrow = proj(C_CQ, C_MQ)
    cq = narrow[:, C_CQ:C_CKV]
    ckv = narrow[:, C_CKV:C_KR]
    kr_raw = narrow[:, C_KR:C_AB]
    ab = narrow[:, C_AB:C_MQ]
    cqn = _rms(cq, qan_ref[...]).astype(BF16)
    gdn_dot(0)
    ckvn = _rms(ckv, kvan_ref[...]).astype(BF16)
    qf = _dot(cqn, wqb_ref[...])

    c32, s32 = cos_ref[0].T, sin_ref[0].T
    zpad = jnp.zeros((ts, LANES - MLA_ROPE), F32)
    cosm = jnp.concatenate([c32, c32, zpad], axis=1)
    sinm = jnp.concatenate([-s32, s32, zpad], axis=1)

    def rope(r):
        return r * cosm + pltpu.roll(r, MLA_ROPE, 1) * sinm

    gdn_epilogue(0)
    gdn_dot(1)
    kn = _dot(ckvn, wkk_ref[...])
    vv = _dot(ckvn, wkv_ref[...])
    mq = proj(C_MQ, C_GATE)
    qscale = MLA_SCALE * LOG2E
    for h in range(HEADS):
        lo = QK_WIDTH * h
        q_out[0, :, lo:lo + LANES] = (qf[:, lo:lo + LANES] * qscale).astype(BF16)
        q_out[0, :, lo + LANES:lo + QK_WIDTH] = (rope(qf[:, lo + LANES:lo + QK_WIDTH]) * qscale).astype(BF16)
    gdn_epilogue(1)
    gdn_dot(2)
    gate_out[0, :, 0:hd] = proj(C_GATE, C_GATE + hd).astype(BF16)
    v_out[0] = vv.astype(BF16)
    kr = rope(kr_raw).astype(BF16)
    for h in range(HEADS):
        lo = QK_WIDTH * h
        k_out[0, :, lo:lo + LANES] = kn[:, LANES * h:LANES * (h + 1)].astype(BF16)
        k_out[0, :, lo + LANES:lo + QK_WIDTH] = kr
    gdn_epilogue(2)
    gdn_dot(3)
    gate_out[0, :, hd:2 * hd] = proj(C_GATE + hd, C_GATE + 2 * hd).astype(BF16)

    mscale = MEM_SCALE * LOG2E
    sc, pp, ll, oo = {}, {}, {}, {}

    def mem_qk(h):
        cols = slice(HEAD_DIM * h, HEAD_DIM * (h + 1))
        sc[h] = _dot_nt((mq[:, cols] * mscale).astype(BF16), mk_ref[0, :, cols])

    def mem_softmax(h):
        p = jnp.exp2(sc[h] - jnp.max(sc[h], axis=-1, keepdims=True))
        ll[h] = jnp.sum(p, axis=-1, keepdims=True)
        pp[h] = p.astype(BF16)

    def mem_pv(h):
        oo[h] = _dot(pp[h], mv_ref[0, :, HEAD_DIM * h:HEAD_DIM * (h + 1)])

    mem_qk(0)
    z = ab + dtb_ref[...]
    softplus = jnp.maximum(z, 0.0) + jnp.log1p(jnp.exp(-jnp.abs(z)))
    gcum = -jnp.exp(alog_ref[...]) * softplus
    lane = lax.broadcasted_iota(jnp.int32, ab.shape, 1)
    pos_in_chunk = lax.broadcasted_iota(jnp.int32, ab.shape, 0) % GDN_CHUNK
    shift = 1
    while shift < GDN_CHUNK:
        gcum = gcum + jnp.where(pos_in_chunk >= shift, pltpu.roll(gcum, shift, 0), 0.0)
        shift *= 2
    gg_out[0] = jnp.where(lane < HEADS, gcum, _sigmoid(ab))
    ggt_out[0] = gcum.T[0:SUBLANES, :]

    gdn_epilogue(3)
    gdn_dot(4)
    mem_qk(1)
    mem_softmax(0)
    gdn_epilogue(4)
    gdn_dot(5)
    mem_qk(2)
    mem_pv(0)
    mem_softmax(1)
    gdn_epilogue(5)
    mem_qk(3)
    mem_pv(1)
    mem_softmax(2)
    mem_pv(2)
    mem_softmax(3)
    gmem = _silu(proj(C_GATE + 2 * hd, C_END))
    mem_pv(3)
    for h in range(HEADS):
        cols = slice(HEAD_DIM * h, HEAD_DIM * (h + 1))
        om_out[0, :, cols] = (oo[h] / ll[h] * gmem[:, cols]).astype(BF16)


def _mla_kernel(q_ref, k_ref, v_ref, o_ref, s_scr, p_scr, m_scr, a_scr, acc_scr, *, tq):
    i = pl.program_id(1)
    heads = range(HEADS)
    strip = MLA_STRIP

    def run(n_full):
        half = tq // 2
        items = [(h, r0, half, r0 + half, n_full * tq, True) for h in heads for r0 in (0, half)]
        items += [(h, 0, tq, tq, jk * tq, False) for jk in range(n_full) for h in heads]

        def scores(item):
            h, r0, nr, nk, start, _ = item
            s_scr[h, r0:r0 + nr, 0:nk] = _dot_nt(q_ref[0, r0:r0 + nr, QK_WIDTH * h:QK_WIDTH * (h + 1)],
                                                 k_ref[0, start:start + nk, QK_WIDTH * h:QK_WIDTH * (h + 1)])

        def load_strip(item, r):
            h, r0, nr, nk, _, diagonal = item
            s = s_scr[h, r0 + strip * r:r0 + strip * (r + 1), 0:nk]
            if diagonal:
                row = lax.broadcasted_iota(jnp.int32, s.shape, 0) + (r0 + strip * r)
                col = lax.broadcasted_iota(jnp.int32, s.shape, 1)
                s = jnp.where(row >= col, s, -jnp.inf)
            return s

        def lane_tiles(t):
            return [t[:, LANES * g:LANES * (g + 1)] for g in range(t.shape[1] // LANES)]

        def softmax(item):
            h, r0, nr, nk, _, diagonal = item
            strips = range(nr // strip)
            rows = [slice(r0 + strip * r, r0 + strip * (r + 1)) for r in strips]
            part = [functools.reduce(jnp.maximum, lane_tiles(load_strip(item, r))) for r in strips]
            peak = [jnp.broadcast_to(jnp.max(t, axis=-1, keepdims=True), (strip, LANES)) for t in part]
            if diagonal:
                m_new = peak
            else:
                m_old = [m_scr[h, rows[r], :] for r in strips]
                m_new = [jnp.maximum(m_old[r], peak[r]) for r in strips]
                alpha = [jnp.exp2(m_old[r] - m_new[r]) for r in strips]
                for r in strips:
                    a_scr[h, rows[r], :] = alpha[r]
            for r in strips:
                m_scr[h, rows[r], :] = m_new[r]
            for r in strips:
                for g, t in enumerate(lane_tiles(load_strip(item, r))):
                    p_scr[h, rows[r], LANES * g:LANES * (g + 1)] = jnp.exp2(t - m_new[r]).astype(BF16)

        def values(item):
            h, r0, nr, nk, start, diagonal = item
            v = v_ref[0, start:start + nk, HEAD_DIM * h:HEAD_DIM * (h + 1)]
            pv = _dot(p_scr[h, r0:r0 + nr, 0:nk], jnp.concatenate([v, jnp.ones((nk, LANES), BF16)], axis=1))
            if diagonal:
                acc_scr[h, r0:r0 + nr, :] = pv
            else:
                a = a_scr[h, r0:r0 + nr, :]
                acc_scr[h, r0:r0 + nr, :] = jnp.concatenate([a, a], axis=1) * acc_scr[h, r0:r0 + nr, :] + pv

        for t in range(len(items) + 3):
            if t < len(items):
                scores(items[t])
            if 0 <= t - 1 < len(items):
                softmax(items[t - 1])
            if 0 <= t - 3 < len(items):
                values(items[t - 3])

        for h in heads:
            o_ref[0, :, HEAD_DIM * h:HEAD_DIM * (h + 1)] = (
                acc_scr[h, :, 0:HEAD_DIM] / acc_scr[h, :, HEAD_DIM:HEAD_DIM + LANES]).astype(BF16)

    for n_full in range(k_ref.shape[1] // tq):
        pl.when(i == n_full)(functools.partial(run, n_full))


def _pair_blockdiag(t, lo_half):
    return jnp.concatenate([jnp.where(lo_half, t, 0.0), jnp.where(lo_half, 0.0, t)], axis=0).astype(BF16)


def _gdn_merge_kernel(qkv_ref, gg_ref, ggt_ref, gn_ref, x_ref, omla_ref, om_ref, gate_ref, wout_ref,
                      nf_ref, out_ref, s_ref, og_scr, *, ts):
    j = pl.program_id(1)
    hd = HEADS * HEAD_DIM
    ntile = 2 * LANES

    @pl.when(j == 0)
    def _():
        s_ref[...] = jnp.zeros(s_ref.shape, F32)

    c = GDN_CHUNK
    c2 = 2 * c
    npairs = ts // c2
    row = lax.broadcasted_iota(jnp.int32, (c, c2), 0)
    lane = lax.broadcasted_iota(jnp.int32, (c, c2), 1)
    lo_half = lane < c
    col = jnp.where(lo_half, lane, lane - c)
    incl = row >= col
    strict = row > col
    eye = jnp.where(row == col, 1.0, 0.0)
    zeros_k = jnp.zeros((c, HEAD_DIM), BF16)
    units = [(p2, h) for p2 in range(npairs) for h in range(HEADS)]

    pre = {}
    for (p2, h) in units:
        r2 = slice(c2 * p2, c2 * (p2 + 1))
        gg = gg_ref[0, r2, :]
        gcol = gg[:, h:h + 1]
        beta = gg[:, HEADS + h:HEADS + h + 1]
        grow = ggt_ref[0, h:h + 1, r2]
        glast = (gg[c - 1:c, h:h + 1], gg[c2 - 1:c2, h:h + 1])
        glast_col = jnp.concatenate([jnp.broadcast_to(glast[0], (c, 1)), jnp.broadcast_to(glast[1], (c, 1))], axis=0)
        per_group = GDN_GROUP // HEAD_DIM
        gcols = slice(HEAD_DIM * (h % per_group), HEAD_DIM * (h % per_group + 1))
        q2 = qkv_ref[0, h // per_group, r2, gcols]
        k2 = qkv_ref[0, GDN_GROUPS // 3 + h // per_group, r2, gcols]
        kf = k2.astype(F32)
        kbeta = kf * beta
        kbeta_b = kbeta.astype(BF16)
        vbeta_b = (qkv_ref[0, 2 * (GDN_GROUPS // 3) + h // per_group, r2, gcols].astype(F32) * beta).astype(BF16)
        eg = jnp.exp(gcol)
        lhs = jnp.concatenate([jnp.concatenate([kbeta_b[:c], kbeta_b[c:]], axis=1),
                               jnp.concatenate([q2[:c], q2[c:]], axis=1)], axis=0)
        rhs = jnp.concatenate([jnp.concatenate([k2[:c], zeros_k], axis=1),
                               jnp.concatenate([zeros_k, k2[c:]], axis=1)], axis=0)
        gcol_pair = jnp.where(lo_half, gcol[:c], gcol[c:])
        pre[(p2, h)] = dict(
            lhs=lhs, rhs=rhs, glast=glast,
            vk=jnp.concatenate([vbeta_b, (kbeta * eg).astype(BF16)], axis=1),
            qg=(q2.astype(F32) * eg).astype(BF16),
            kdec=(kf * jnp.exp(glast_col - gcol)).astype(BF16),
            decay=jnp.exp(jnp.where(incl, gcol_pair - grow, -jnp.inf)))

    kq = {u: _dot_nt(pre[u]["lhs"], pre[u]["rhs"]) for u in units}
    a_pair = {u: kq[u][c:] * pre[u]["decay"] for u in units}
    m = {u: -jnp.where(strict, kq[u][:c] * pre[u]["decay"], 0.0) for u in units}

    p = {u: eye + m[u] for u in units}
    m = {u: _dot(m[u].astype(BF16), _pair_blockdiag(m[u], lo_half)) for u in units}
    for _ in range(int(np.log2(c)) - 2):
        pm = {u: _dot(jnp.concatenate([p[u], m[u]], axis=0).astype(BF16), _pair_blockdiag(m[u], lo_half))
              for u in units}
        p = {u: p[u] + pm[u][:c] for u in units}
        m = {u: pm[u][c:] for u in units}
    pm = {u: _dot(p[u].astype(BF16), _pair_blockdiag(m[u], lo_half)) for u in units}
    t_pair = {u: p[u] + pm[u] for u in units}

    uw = {u: _dot(_pair_blockdiag(t_pair[u], lo_half), pre[u]["vk"]) for u in units}
    a_chunks = {u: (a_pair[u][:, :c].astype(BF16), pltpu.roll(a_pair[u], c, 1)[:, :c].astype(BF16)) for u in units}

    a_mla = (omla_ref[0].astype(F32) * _silu(gate_ref[0, :, 0:hd].astype(F32))).astype(BF16)
    free_parts = [(a_mla, 0), (om_ref[0], 2 * hd)]
    pieces = [(part, n) for part in range(len(free_parts)) for n in range(D_MODEL // ntile)]
    partial = {}

    heads = range(HEADS)
    state = [s_ref[h] for h in heads]
    nsteps = 2 * npairs
    for step in range(nsteps):
        p2, ci = divmod(step, 2)
        rc = slice(c * ci, c * (ci + 1))
        rows = slice(c * step, c * (step + 1))
        ws_qs = [_dot(jnp.concatenate([uw[(p2, h)][rc, HEAD_DIM:].astype(BF16), pre[(p2, h)]["qg"][rc]], axis=0),
                      state[h].astype(BF16)) for h in heads]
        for part, n in pieces[len(pieces) * step // nsteps:len(pieces) * (step + 1) // nsteps]:
            operand, w_lo = free_parts[part]
            partial[(part, n)] = _dot(operand, wout_ref[w_lo:w_lo + hd, ntile * n:ntile * (n + 1)])
        v_new = [(uw[(p2, h)][rc, :HEAD_DIM] - ws_qs[h][:c]).astype(BF16) for h in heads]
        o_intra = [_dot(a_chunks[(p2, h)][ci], v_new[h]) for h in heads]
        ds = [_dot_tn(pre[(p2, h)]["kdec"][rc], v_new[h]) for h in heads]
        state = [state[h] * jnp.exp(pre[(p2, h)]["glast"][ci]) + ds[h] for h in heads]
        for h in heads:
            cols = slice(HEAD_DIM * h, HEAD_DIM * (h + 1))
            gate = _silu(gate_ref[0, rows, hd + HEAD_DIM * h:hd + HEAD_DIM * (h + 1)].astype(F32))
            og_scr[rows, cols] = (_rms(ws_qs[h][c:] + o_intra[h], gn_ref[...]) * gate).astype(BF16)
    for h in heads:
        s_ref[h] = state[h]

    og = og_scr[...]
    ys = []
    for n in range(D_MODEL // ntile):
        ncols = slice(ntile * n, ntile * (n + 1))
        acc = _dot(og, wout_ref[hd:2 * hd, ncols])
        for part in range(len(free_parts)):
            acc = acc + partial[(part, n)]
        ys.append(x_ref[0, :, ncols] + acc)
    ssq = functools.reduce(jnp.add, [jnp.sum(y * y, axis=-1, keepdims=True) for y in ys])
    inv = lax.rsqrt(ssq * (1.0 / D_MODEL) + EPS)
    for n, y in enumerate(ys):
        ncols = slice(ntile * n, ntile * (n + 1))
        out_ref[0, :, ncols] = y * inv * nf_ref[:, ncols]


def _const_spec(shape):
    nd = len(shape)
    return pl.BlockSpec(shape, lambda *_: (0,) * nd)


def _pack_weights(w_in_all, layer, w_q_b, w_kv_b):
    tr = _tile(D_MODEL, 256)
    w_all, w_groups = pl.pallas_call(
        _pack_kernel,
        grid=(D_MODEL // tr,),
        in_specs=[pl.BlockSpec((1, tr, w_in_all.shape[2]), lambda r: (layer, r, 0))],
        out_specs=[pl.BlockSpec((tr, C_END), lambda r: (r, 0)),
                   pl.BlockSpec((GDN_GROUPS, tr, GDN_GROUP), lambda r: (0, r, 0))],
        out_shape=[jax.ShapeDtypeStruct((D_MODEL, C_END), BF16),
                   jax.ShapeDtypeStruct((GDN_GROUPS, D_MODEL, GDN_GROUP), BF16)],
        compiler_params=pltpu.CompilerParams(dimension_semantics=("arbitrary",), vmem_limit_bytes=VMEM_LIMIT),
        name="pack",
    )(w_in_all)
    half = MLA_ROPE // 2
    wq = w_q_b.reshape(Q_LORA, HEADS, HEAD_DIM + MLA_ROPE)
    nope, ropec = wq[..., :HEAD_DIM], wq[..., HEAD_DIM:]
    wqb = jnp.concatenate([nope, ropec, ropec[..., half:], ropec[..., :half]], axis=-1)
    wqb = wqb.reshape(Q_LORA, HEADS * QK_WIDTH).astype(BF16)
    wkv = w_kv_b.reshape(KV_LORA, HEADS, 2 * HEAD_DIM)
    wkk = wkv[..., :HEAD_DIM].reshape(KV_LORA, HEADS * HEAD_DIM).astype(BF16)
    wkvv = wkv[..., HEAD_DIM:].reshape(KV_LORA, HEADS * HEAD_DIM).astype(BF16)
    return w_all, w_groups, wqb, wkk, wkvv


def _lane_row(vec):
    return jnp.zeros((1, LANES), F32).at[0, :vec.shape[0]].set(vec.astype(F32))


def _layer(x, mk, mv, cos, sin, norm_in, w_in_all, layer, q_a_norm, w_q_b, kv_a_norm, w_kv_b, gdn_conv,
           gdn_a_log, gdn_dt_bias, gdn_norm, w_out, out_gain, *, ts_proj, ts_gdn, tq):
    B, S, D = x.shape
    hd = HEADS * HEAD_DIM
    w_all, w_groups, wqb, wkk, wkvv = _pack_weights(w_in_all, layer, w_q_b, w_kv_b)
    conv_groups = gdn_conv.reshape(GDN_CONV, GDN_GROUPS, GDN_GROUP).transpose(1, 0, 2)
    M = mk.shape[1]
    arb2 = pltpu.CompilerParams(dimension_semantics=("arbitrary", "arbitrary"), vmem_limit_bytes=VMEM_LIMIT)

    def row_spec(width, ts):
        return pl.BlockSpec((1, ts, width), lambda b, j: (b, j, 0))

    def bs_shape(width, dtype=BF16):
        return jax.ShapeDtypeStruct((B, S, width), dtype)

    tile_spec = pl.BlockSpec((1, MLA_ROPE // 2, ts_proj), lambda b, j: (b * (S // ts_proj) + j, 0, 0))

    q, k, v, gqkv, gg, ggt, om, gate = pl.pallas_call(
        functools.partial(_proj_kernel, ts=ts_proj),
        grid=(B, S // ts_proj),
        in_specs=[row_spec(D, ts_proj), tile_spec, tile_spec,
                  _const_spec((1, D)), _const_spec(w_all.shape), _const_spec(w_groups.shape),
                  _const_spec((1, Q_LORA)), _const_spec(wqb.shape), _const_spec((1, KV_LORA)),
                  _const_spec(wkk.shape), _const_spec(wkvv.shape), _const_spec(conv_groups.shape),
                  _const_spec((1, LANES)), _const_spec((1, LANES)),
                  pl.BlockSpec((1, M, hd), lambda b, j: (b, 0, 0)), pl.BlockSpec((1, M, hd), lambda b, j: (b, 0, 0))],
        out_specs=[row_spec(HEADS * QK_WIDTH, ts_proj), row_spec(HEADS * QK_WIDTH, ts_proj), row_spec(hd, ts_proj),
                   pl.BlockSpec((1, GDN_GROUPS, ts_proj, GDN_GROUP), lambda b, j: (b, 0, j, 0)),
                   row_spec(LANES, ts_proj), pl.BlockSpec((1, SUBLANES, ts_proj), lambda b, j: (b, 0, j)),
                   row_spec(hd, ts_proj), row_spec(2 * hd, ts_proj)],
        out_shape=[bs_shape(HEADS * QK_WIDTH), bs_shape(HEADS * QK_WIDTH), bs_shape(hd),
                   jax.ShapeDtypeStruct((B, GDN_GROUPS, S, GDN_GROUP), BF16),
                   bs_shape(LANES, F32), jax.ShapeDtypeStruct((B, SUBLANES, S), F32),
                   bs_shape(hd), bs_shape(2 * hd)],
        scratch_shapes=[pltpu.VMEM((GDN_GROUPS, ts_proj + SUBLANES, GDN_GROUP), F32)],
        compiler_params=arb2,
        name="proj",
    )(x, cos, sin, norm_in.reshape(1, D), w_all, w_groups, q_a_norm.reshape(1, Q_LORA), wqb,
      kv_a_norm.reshape(1, KV_LORA), wkk, wkvv, conv_groups, _lane_row(gdn_a_log), _lane_row(gdn_dt_bias), mk, mv)

    o_mla = pl.pallas_call(
        functools.partial(_mla_kernel, tq=tq),
        grid=(B, S // tq),
        in_specs=[pl.BlockSpec((1, tq, HEADS * QK_WIDTH), lambda b, i: (b, i, 0)),
                  pl.BlockSpec((1, S, HEADS * QK_WIDTH), lambda b, i: (b, 0, 0)),
                  pl.BlockSpec((1, S, hd), lambda b, i: (b, 0, 0))],
        out_specs=pl.BlockSpec((1, tq, hd), lambda b, i: (b, i, 0)),
        out_shape=bs_shape(hd),
        scratch_shapes=[pltpu.VMEM((HEADS, tq, tq), F32), pltpu.VMEM((HEADS, tq, tq), BF16),
                        pltpu.VMEM((HEADS, tq, LANES), F32), pltpu.VMEM((HEADS, tq, LANES), F32),
                        pltpu.VMEM((HEADS, tq, HEAD_DIM + LANES), F32)],
        compiler_params=arb2,
        name="mla",
    )(q, k, v)

    return pl.pallas_call(
        functools.partial(_gdn_merge_kernel, ts=ts_gdn),
        grid=(B, S // ts_gdn),
        in_specs=[pl.BlockSpec((1, GDN_GROUPS, ts_gdn, GDN_GROUP), lambda b, j: (b, 0, j, 0)), row_spec(LANES, ts_gdn),
                  pl.BlockSpec((1, SUBLANES, ts_gdn), lambda b, j: (b, 0, j)), _const_spec((1, HEAD_DIM)),
                  row_spec(D, ts_gdn), row_spec(hd, ts_gdn), row_spec(hd, ts_gdn), row_spec(2 * hd, ts_gdn),
                  _const_spec((D_MIX, D)), _const_spec((1, D))],
        out_specs=row_spec(D, ts_gdn),
        out_shape=bs_shape(D, F32),
        scratch_shapes=[pltpu.VMEM((HEADS, HEAD_DIM, HEAD_DIM), F32), pltpu.VMEM((ts_gdn, hd), BF16)],
        compiler_params=arb2,
        name="gdn_merge",
    )(gqkv, gg, ggt, gdn_norm.reshape(1, HEAD_DIM), x, o_mla, om, gate, w_out.astype(BF16),
      out_gain.reshape(1, D))


def _tile(n, pref):
    return pref if n % pref == 0 else n


def kernel(x, mem, positions, norm_in, w_in, q_a_norm, w_q_b, kv_a_norm, w_kv_b, gdn_conv, gdn_a_log,
           gdn_dt_bias, gdn_norm, mem_norm, w_mem_kv, w_out, norm_final):
    B, S, D = x.shape
    M = mem.shape[1]
    depth = norm_in.shape[0]
    assert depth == 1, "the final norm is fused into the single layer's last kernel"
    hd = HEADS * HEAD_DIM
    half = MLA_ROPE // 2
    inv_freq = 1.0 / (ROPE_THETA ** (jnp.arange(half, dtype=F32) / half))
    ts_proj = _tile(S, 512)
    n_tiles = B * S // ts_proj
    nt = _tile(n_tiles, 8)
    cos, sin = pl.pallas_call(
        _rope_kernel,
        grid=(n_tiles // nt,),
        in_specs=[pl.BlockSpec((nt, 1, ts_proj), lambda r: (r, 0, 0)), _const_spec((1, half, 1))],
        out_specs=[pl.BlockSpec((nt, half, ts_proj), lambda r: (r, 0, 0))] * 2,
        out_shape=[jax.ShapeDtypeStruct((n_tiles, half, ts_proj), F32)] * 2,
        compiler_params=pltpu.CompilerParams(dimension_semantics=("arbitrary",), vmem_limit_bytes=VMEM_LIMIT),
        name="rope",
    )(positions.reshape(n_tiles, 1, ts_proj), inv_freq.reshape(1, half, 1))
    l = 0
    nb_mem = _tile(B, 4)
    mk, mv = pl.pallas_call(
        _memkv_kernel,
        grid=(B // nb_mem,),
        in_specs=[pl.BlockSpec((nb_mem, M, D), lambda b: (b, 0, 0)), _const_spec((1, D)), _const_spec((D, 2 * hd))],
        out_specs=[pl.BlockSpec((nb_mem, M, hd), lambda b: (b, 0, 0))] * 2,
        out_shape=[jax.ShapeDtypeStruct((B, M, hd), BF16)] * 2,
        compiler_params=pltpu.CompilerParams(dimension_semantics=("arbitrary",), vmem_limit_bytes=VMEM_LIMIT),
        name="memkv",
    )(mem, mem_norm[l].reshape(1, D), w_mem_kv[l].astype(BF16))
    return _layer(x, mk, mv, cos, sin, norm_in[l], w_in, l, q_a_norm[l], w_q_b[l], kv_a_norm[l], w_kv_b[l],
                  gdn_conv[l], gdn_a_log[l], gdn_dt_bias[l], gdn_norm[l], w_out[l], norm_final,
                  ts_proj=ts_proj, ts_gdn=_tile(S, 512), tq=_tile(S, 512))
```
